```python
import jax, jax.numpy as jnp
from jax import lax
import numpy as np

D_MODEL = 2048
BATCH = 2
SEQ = 4096
DEPTH = 4
DEC_BATCH = 8
DEC_SEQ = 4
PAST_LEN = 16384
PAGE_SIZE = 128

N_META = 16
N_MIXERS = 2
N_ATTN_LAYERS = (DEPTH + N_MIXERS - 1) // N_MIXERS
N_REC_LAYERS = DEPTH // N_MIXERS
N_HEADS = 16
HEAD_DIM = 128
N_KV_HEADS = 4
GROUP = N_HEADS // N_KV_HEADS
ROT_DIM = HEAD_DIM // 4
ROPE_THETA = 500000.0
IDX_HEADS = 16
IDX_DIM = 64
IDX_ROT = IDX_DIM // 4
TOPK_MAX = 256
Q_BLOCK = 128
Q_W = N_HEADS * HEAD_DIM
KV_W = N_KV_HEADS * HEAD_DIM
IQ_W = IDX_HEADS * IDX_DIM
PROJ_W = Q_W + 2 * KV_W + IQ_W + IDX_DIM + IDX_HEADS
PROJ_SPLITS = [Q_W, Q_W + KV_W, Q_W + 2 * KV_W, Q_W + 2 * KV_W + IQ_W, Q_W + 2 * KV_W + IQ_W + IDX_DIM]
D_RNN = D_MODEL
RNN_BLOCKS = 8
RNN_BLOCK_DIM = D_RNN // RNN_BLOCKS
LRU_CONV = 4
LRU_C = 8.0
D_FF = 3 * D_MODEL
FFN_CONV = 3
EPS = 1e-6
NEG = -1e30

kernel_name = 'dsa_rglru_convffn_hybrid_step'


def _rms(x, g):
    xf = x.astype(jnp.float32)
    y = xf * lax.rsqrt(jnp.mean(xf * xf, axis=-1, keepdims=True) + EPS)
    return (y * g.astype(jnp.float32)).astype(x.dtype)


def _rope(x, pos, rot):
    half = rot // 2
    inv = 1.0 / (ROPE_THETA ** (jnp.arange(half, dtype=jnp.float32) * (2.0 / rot)))
    ang = pos.astype(jnp.float32)[:, None] * inv[None, :]
    cos = jnp.cos(ang)[:, None, :]
    sin = jnp.sin(ang)[:, None, :]
    x1 = x[..., :half].astype(jnp.float32)
    x2 = x[..., half:rot].astype(jnp.float32)
    r = jnp.concatenate([x1 * cos - x2 * sin, x2 * cos + x1 * sin], axis=-1).astype(x.dtype)
    return jnp.concatenate([r, x[..., rot:]], axis=-1)


def _causal_dwconv(u, buf, w, b):
    width = w.shape[0]
    T = u.shape[1]
    up = jnp.concatenate([buf.astype(u.dtype), u], axis=1)
    y = up[:, 0:T] * w[0]
    for j in range(1, width):
        y = y + up[:, j:j + T] * w[j]
    return y + b, up[:, T:]


def _attn_project(xn, w_in, q_g, k_g, ik_g, pos):
    B, T, _ = xn.shape
    z = xn @ w_in
    q, k, v, iq, ik, iw = jnp.split(z, PROJ_SPLITS, axis=-1)
    q = _rope(_rms(q.reshape(B, T, N_HEADS, HEAD_DIM), q_g), pos, ROT_DIM)
    k = _rope(_rms(k.reshape(B, T, N_KV_HEADS, HEAD_DIM), k_g), pos, ROT_DIM)
    v = v.reshape(B, T, N_KV_HEADS, HEAD_DIM)
    iq = _rope(iq.reshape(B, T, IDX_HEADS, IDX_DIM), pos, IDX_ROT)
    ik = _rope(_rms(ik, ik_g)[:, :, None, :], pos, IDX_ROT)[:, :, 0, :]
    iw = iw * (IDX_HEADS ** -0.5)
    return q, k, v, iq, ik, iw


def _index_select(iq, iw, ik, q_pos, topk):
    n_keys = ik.shape[1]
    s = jnp.einsum('bqhd,bsd->bqhs', iq.astype(jnp.float32), ik.astype(jnp.float32))
    score = jnp.einsum('bqh,bqhs->bqs', iw.astype(jnp.float32), jax.nn.relu(s)) * (IDX_DIM ** -0.5)
    causal = jnp.arange(n_keys)[None, :] <= q_pos[:, None]
    score = jnp.where(causal[None], score, NEG)
    _, idx = lax.top_k(score, topk)
    return idx, idx <= q_pos[None, :, None]


def _attend(q, kg, vg, mask):
    B, Q = q.shape[:2]
    qg = q.reshape(B, Q, N_KV_HEADS, GROUP, HEAD_DIM)
    logits = jnp.einsum('bqngd,bqsnd->bqngs', qg, kg).astype(jnp.float32) * (HEAD_DIM ** -0.5)
    logits = jnp.where(mask[:, :, None, None, :], logits, NEG)
    p = jax.nn.softmax(logits, axis=-1).astype(vg.dtype)
    o = jnp.einsum('bqngs,bqsnd->bqngd', p, vg)
    return o.reshape(B, Q, N_HEADS * HEAD_DIM)


_take_rows = jax.vmap(lambda rows, ii: rows[ii])


def _dsa_prompt(xn, w_in, q_g, k_g, ik_g, w_out, topk):
    B, T, _ = xn.shape
    pos = jnp.arange(T)
    q, k, v, iq, ik, iw = _attn_project(xn, w_in, q_g, k_g, ik_g, pos)
    nb = -(-T // Q_BLOCK)
    pad = nb * Q_BLOCK - T

    def blockify(a):
        a = jnp.pad(a, [(0, 0), (0, pad)] + [(0, 0)] * (a.ndim - 2))
        return a.reshape((B, nb, Q_BLOCK) + a.shape[2:]).swapaxes(0, 1)

    pos_b = jnp.arange(nb * Q_BLOCK).reshape(nb, Q_BLOCK)

    def one_block(args):
        qb, iqb, iwb, pb = args
        idx, valid = _index_select(iqb, iwb, ik, pb, topk)
        return _attend(qb, _take_rows(k, idx), _take_rows(v, idx), valid)

    o = lax.map(one_block, (blockify(q), blockify(iq), blockify(iw), pos_b))
    o = o.swapaxes(0, 1).reshape(B, nb * Q_BLOCK, N_HEADS * HEAD_DIM)[:, :T]
    return o @ w_out, k, v, ik


def _dsa_sample(xn, cache_k, cache_v, cache_idx_k, layer, page_table, w_in, q_g, k_g, ik_g, w_out, topk):
    B, S, _ = xn.shape
    past = page_table.shape[1] * PAGE_SIZE
    pos = past + jnp.arange(S)
    q, k, v, iq, ik, iw = _attn_project(xn, w_in, q_g, k_g, ik_g, pos)
    ik_past = cache_idx_k[layer, page_table].reshape(B, past, IDX_DIM)
    ik_all = jnp.concatenate([ik_past.astype(ik.dtype), ik], axis=1)
    idx, valid = _index_select(iq, iw, ik_all, pos, topk)
    in_past = idx < past
    pi = jnp.minimum(idx, past - 1)
    phys = jnp.take_along_axis(page_table, (pi // PAGE_SIZE).reshape(B, -1), axis=1).reshape(idx.shape)
    off = pi % PAGE_SIZE
    ni = jnp.clip(idx - past, 0, S - 1)
    sel = in_past[..., None, None]
    kg = jnp.where(sel, cache_k[layer, phys, off].astype(k.dtype), _take_rows(k, ni))
    vg = jnp.where(sel, cache_v[layer, phys, off].astype(v.dtype), _take_rows(v, ni))
    o = _attend(q, kg, vg, valid)
    return o @ w_out, k, v, ik


def _linear_scan(a, b, h0):
    def step(h, ab):
        h = ab[0] * h + ab[1]
        return h, h
    hT, hs = lax.scan(step, h0.astype(jnp.float32), (a.swapaxes(0, 1), b.swapaxes(0, 1)))
    return hs.swapaxes(0, 1), hT


def _lru_block(xn, w_x, w_y, conv_w, conv_b, wa, ba, wi, bi, lam, w_out, h0, buf):
    B, T, _ = xn.shape
    gate = jax.nn.gelu(xn @ w_y)
    c, new_buf = _causal_dwconv(xn @ w_x, buf, conv_w, conv_b)
    cb = c.reshape(B, T, RNN_BLOCKS, RNN_BLOCK_DIM)
    r = jax.nn.sigmoid((jnp.einsum('btnc,ncd->btnd', cb, wa) + ba.reshape(RNN_BLOCKS, RNN_BLOCK_DIM)).astype(jnp.float32)).reshape(B, T, D_RNN)
    i = jax.nn.sigmoid((jnp.einsum('btnc,ncd->btnd', cb, wi) + bi.reshape(RNN_BLOCKS, RNN_BLOCK_DIM)).astype(jnp.float32)).reshape(B, T, D_RNN)
    log_a = LRU_C * r * jax.nn.log_sigmoid(lam.astype(jnp.float32))
    a = jnp.exp(log_a)
    b = jnp.sqrt(-jnp.expm1(2.0 * log_a)) * (i * c.astype(jnp.float32))
    hs, hT = _linear_scan(a, b, h0)
    y = (hs.astype(xn.dtype) * gate) @ w_out
    return y, hT.astype(h0.dtype), new_buf


def _conv_ffn(xn, w_up, w_gate, dw_w, dw_b, w_down, buf):
    c, new_buf = _causal_dwconv(xn @ w_up, buf, dw_w, dw_b)
    h = jax.nn.gelu(c) * (xn @ w_gate)
    return h @ w_down, new_buf


def setup_inputs(seed: int = 0) -> dict:
    key = jax.random.key(seed)
    ks = jax.random.split(key, 40)
    f32 = jnp.float32
    n_pages = PAST_LEN // PAGE_SIZE
    n_pool = DEC_BATCH * n_pages * 5 // 4
    nrm = lambda k, shape, s=1.0: jax.random.normal(k, shape, f32) * s
    page_table = jax.random.permutation(ks[0], n_pool)[:DEC_BATCH * n_pages].reshape(DEC_BATCH, n_pages).astype(jnp.int32)
    u = jax.random.uniform(ks[1], (N_REC_LAYERS, D_RNN), f32, minval=0.9, maxval=0.999)
    s = u ** (1.0 / LRU_C)
    lam = jnp.log(s) - jnp.log1p(-s)
    return {
        'x_prompt': nrm(ks[2], (BATCH, SEQ, D_MODEL)),
        'x_sample': nrm(ks[3], (DEC_BATCH, DEC_SEQ, D_MODEL)),
        'cache_k': nrm(ks[4], (N_ATTN_LAYERS, n_pool, PAGE_SIZE, N_KV_HEADS, HEAD_DIM)),
        'cache_v': nrm(ks[5], (N_ATTN_LAYERS, n_pool, PAGE_SIZE, N_KV_HEADS, HEAD_DIM)),
        'cache_idx_k': nrm(ks[6], (N_ATTN_LAYERS, n_pool, PAGE_SIZE, IDX_DIM)),
        'state_lru_h': nrm(ks[7], (N_REC_LAYERS, DEC_BATCH, D_RNN), 0.5),
        'state_lru_conv': nrm(ks[8], (N_REC_LAYERS, DEC_BATCH, LRU_CONV - 1, D_RNN)),
        'state_ffn_conv': nrm(ks[9], (DEPTH, DEC_BATCH, FFN_CONV - 1, D_FF)),
        'page_table': page_table,
        'meta_tokens': nrm(ks[10], (N_META, D_MODEL)),
        'norm1_g': 1.0 + nrm(ks[11], (DEPTH, D_MODEL), 0.01),
        'norm2_g': 1.0 + nrm(ks[12], (DEPTH, D_MODEL), 0.01),
        'attn_w_in': nrm(ks[13], (N_ATTN_LAYERS, D_MODEL, PROJ_W), D_MODEL ** -0.5),
        'attn_q_norm_g': 1.0 + nrm(ks[14], (N_ATTN_LAYERS, HEAD_DIM), 0.01),
        'attn_k_norm_g': 1.0 + nrm(ks[15], (N_ATTN_LAYERS, HEAD_DIM), 0.01),
        'attn_idx_k_norm_g': 1.0 + nrm(ks[16], (N_ATTN_LAYERS, IDX_DIM), 0.01),
        'attn_w_out': nrm(ks[17], (N_ATTN_LAYERS, N_HEADS * HEAD_DIM, D_MODEL), (N_HEADS * HEAD_DIM) ** -0.5),
        'lru_w_x': nrm(ks[18], (N_REC_LAYERS, D_MODEL, D_RNN), D_MODEL ** -0.5),
        'lru_w_y': nrm(ks[19], (N_REC_LAYERS, D_MODEL, D_RNN), D_MODEL ** -0.5),
        'lru_conv_w': nrm(ks[20], (N_REC_LAYERS, LRU_CONV, D_RNN), LRU_CONV ** -0.5),
        'lru_conv_b': nrm(ks[21], (N_REC_LAYERS, D_RNN), 0.01),
        'lru_wa': nrm(ks[22], (N_REC_LAYERS, RNN_BLOCKS, RNN_BLOCK_DIM, RNN_BLOCK_DIM), RNN_BLOCK_DIM ** -0.5),
        'lru_ba': nrm(ks[23], (N_REC_LAYERS, D_RNN), 0.01),
        'lru_wi': nrm(ks[24], (N_REC_LAYERS, RNN_BLOCKS, RNN_BLOCK_DIM, RNN_BLOCK_DIM), RNN_BLOCK_DIM ** -0.5),
        'lru_bi': nrm(ks[25], (N_REC_LAYERS, D_RNN), 0.01),
        'lru_lambda': lam,
        'lru_w_out': nrm(ks[26], (N_REC_LAYERS, D_RNN, D_MODEL), D_RNN ** -0.5),
        'ffn_w_up': nrm(ks[27], (DEPTH, D_MODEL, D_FF), D_MODEL ** -0.5),
        'ffn_w_gate': nrm(ks[28], (DEPTH, D_MODEL, D_FF), D_MODEL ** -0.5),
        'ffn_dw_w': nrm(ks[29], (DEPTH, FFN_CONV, D_FF), FFN_CONV ** -0.5),
        'ffn_dw_b': nrm(ks[30], (DEPTH, D_FF), 0.01),
        'ffn_w_down': nrm(ks[31], (DEPTH, D_FF, D_MODEL), D_FF ** -0.5),
    }


def reference(x_prompt, x_sample, cache_k, cache_v, cache_idx_k, state_lru_h, state_lru_conv, state_ffn_conv, page_table, meta_tokens, norm1_g, norm2_g, attn_w_in, attn_q_norm_g, attn_k_norm_g, attn_idx_k_norm_g, attn_w_out, lru_w_x, lru_w_y, lru_conv_w, lru_conv_b, lru_wa, lru_ba, lru_wi, lru_bi, lru_lambda, lru_w_out, ffn_w_up, ffn_w_gate, ffn_dw_w, ffn_dw_b, ffn_w_down):
    B = x_prompt.shape[0]
    DB, S = x_sample.shape[0], x_sample.shape[1]
    dt = x_prompt.dtype
    hp = jnp.concatenate([jnp.broadcast_to(meta_tokens[None].astype(dt), (B, N_META, D_MODEL)), x_prompt], axis=1)
    hs = x_sample
    T = hp.shape[1]
    past = page_table.shape[1] * PAGE_SIZE
    topk_p = min(TOPK_MAX, T // 4)
    topk_s = min(TOPK_MAX, (past + S) // 4)
    kp, vp, ikp, ks_, vs_, iks = [], [], [], [], [], []
    hpl, cpl, hsl, csl = [], [], [], []
    fpl, fsl = [], []
    for layer in range(DEPTH):
        m = layer // N_MIXERS
        xp = _rms(hp, norm1_g[layer])
        xs = _rms(hs, norm1_g[layer])
        if layer % N_MIXERS == 0:
            aw = (attn_w_in[m], attn_q_norm_g[m], attn_k_norm_g[m], attn_idx_k_norm_g[m], attn_w_out[m])
            op, k1, v1, ik1 = _dsa_prompt(xp, *aw, topk_p)
            osm, k2, v2, ik2 = _dsa_sample(xs, cache_k, cache_v, cache_idx_k, m, page_table, *aw, topk_s)
            kp.append(k1); vp.append(v1); ikp.append(ik1)
            ks_.append(k2); vs_.append(v2); iks.append(ik2)
        else:
            lw = (lru_w_x[m], lru_w_y[m], lru_conv_w[m], lru_conv_b[m], lru_wa[m], lru_ba[m], lru_wi[m], lru_bi[m], lru_lambda[m], lru_w_out[m])
            op, h1, c1 = _lru_block(xp, *lw, jnp.zeros((B, D_RNN), dt), jnp.zeros((B, LRU_CONV - 1, D_RNN), dt))
            osm, h2, c2 = _lru_block(xs, *lw, state_lru_h[m], state_lru_conv[m])
            hpl.append(h1); cpl.append(c1); hsl.append(h2); csl.append(c2)
        hp = hp + op
        hs = hs + osm
        fw = (ffn_w_up[layer], ffn_w_gate[layer], ffn_dw_w[layer], ffn_dw_b[layer], ffn_w_down[layer])
        fp, bp = _conv_ffn(_rms(hp, norm2_g[layer]), *fw, jnp.zeros((B, FFN_CONV - 1, D_FF), dt))
        fs, bs = _conv_ffn(_rms(hs, norm2_g[layer]), *fw, state_ffn_conv[layer])
        fpl.append(bp); fsl.append(bs)
        hp = hp + fp
        hs = hs + fs
    y_prompt = hp[:, N_META:]
    return (y_prompt, hs, jnp.stack(kp), jnp.stack(vp), jnp.stack(ikp), jnp.stack(ks_), jnp.stack(vs_), jnp.stack(iks), jnp.stack(hpl), jnp.stack(cpl), jnp.stack(hsl), jnp.stack(csl), jnp.stack(fpl), jnp.stack(fsl))
```

```python
import functools

import jax
import jax.numpy as jnp
from jax import lax
from jax.experimental import pallas as pl
from jax.experimental.pallas import tpu as pltpu

F32 = jnp.float32
BF16 = jnp.bfloat16
I32 = jnp.int32

D_MODEL = 2048
N_META = 16
N_HEADS = 16
HEAD_DIM = 128
N_KV = 4
GROUP = N_HEADS // N_KV
ROT = HEAD_DIM // 4
IDX_HEADS = 16
IDX_DIM = 64
IDX_ROT = IDX_DIM // 4
ROPE_THETA = 500000.0
TOPK_MAX = 256
PAGE = 128
Q_W = N_HEADS * HEAD_DIM
KV_W = N_KV * HEAD_DIM
IQ_W = IDX_HEADS * IDX_DIM
PROJ_W = Q_W + 2 * KV_W + IQ_W + IDX_DIM + IDX_HEADS
D_FF = 3 * D_MODEL
FFN_CONV = 3
LRU_CONV = 4
LRU_C = 8.0
RNN_BLOCKS = 8
RNN_BD = D_MODEL // RNN_BLOCKS
EPS = 1e-6
NEG = -1e30

LANES = 128
SUBLANES = 8
QB = 128
PROJ_TN = 256
PROJ_NB = -(-PROJ_W // PROJ_TN)
VMEM_LIMIT = 56 * 1024 * 1024

_NT = (((1,), (1,)), ((), ()))
_INT_MIN = -2 ** 31


def _round_up(x, m):
    return -(-x // m) * m


def _rms_rows(x, g):
    ms = jnp.mean(x * x, axis=-1, keepdims=True)
    return x * lax.rsqrt(ms + EPS) * g


def _gelu(x):
    return 0.5 * x * (1.0 + jnp.tanh(0.7978845608028654 * (x + 0.044715 * (x * x * x))))


def _sigmoid(x):
    return 1.0 / (1.0 + jnp.exp(-x))


def _sort_key(x):
    b = lax.bitcast_convert_type(x + 0.0, I32)
    return b ^ ((b >> 31) & 0x7FFFFFFF)


def _kth_largest_key(count_ge, rows, k):
    def bit_body(bi, t):
        cand = t | jnp.left_shift(jnp.int32(1), 31 - bi)
        cnt = count_ge(cand ^ _INT_MIN)
        return jnp.where(cnt >= k, cand, t)
    t = lax.fori_loop(0, 32, bit_body, jnp.zeros((rows, 1), I32))
    return t ^ _INT_MIN


def _rope(x, tab_ref, sh):
    return (x * tab_ref[0] + pltpu.roll(x, LANES - sh, 1) * tab_ref[1]
            + pltpu.roll(x, sh, 1) * tab_ref[2])


def _proj_kernel(h_ref, g_ref, w_ref, qg_ref, kg_ref, ikg_ref, rq_ref, ri_ref,
                 q_ref, kf_ref, kb_ref, vf_ref, vb_ref, iq_ref, ikf_ref, ika_ref, ikb_ref, iw_ref,
                 xn_sc):
    j = pl.program_id(1)

    @pl.when(j == 0)
    def _():
        xn_sc[...] = _rms_rows(h_ref[...], g_ref[...]).astype(BF16)

    z = jnp.dot(xn_sc[...], w_ref[...], preferred_element_type=F32)
    nq, nkv, niq = Q_W // PROJ_TN, KV_W // PROJ_TN, IQ_W // PROJ_TN

    @pl.when(j < nq)
    def _():
        for s in range(PROJ_TN // LANES):
            x = _rope(_rms_rows(z[:, s * LANES:(s + 1) * LANES], qg_ref[...]), rq_ref, ROT // 2)
            q_ref[:, s * LANES:(s + 1) * LANES] = x.astype(BF16)

    @pl.when((j >= nq) & (j < nq + nkv))
    def _():
        for s in range(PROJ_TN // LANES):
            x = _rope(_rms_rows(z[:, s * LANES:(s + 1) * LANES], kg_ref[...]), rq_ref, ROT // 2)
            kf_ref[:, s * LANES:(s + 1) * LANES] = x
            kb_ref[:, s * LANES:(s + 1) * LANES] = x.astype(BF16)

    @pl.when((j >= nq + nkv) & (j < nq + 2 * nkv))
    def _():
        vf_ref[...] = z
        vb_ref[...] = z.astype(BF16)

    @pl.when((j >= nq + 2 * nkv) & (j < nq + 2 * nkv + niq))
    def _():
        for s in range(PROJ_TN // LANES):
            x = _rope(z[:, s * LANES:(s + 1) * LANES], ri_ref, IDX_ROT // 2)
            iq_ref[:, s * LANES:(s + 1) * LANES] = x.astype(BF16)

    @pl.when(j == nq + 2 * nkv + niq)
    def _():
        x = z[:, :LANES]
        lane = lax.broadcasted_iota(I32, x.shape, 1)
        ms = jnp.sum(jnp.where(lane < IDX_DIM, x * x, 0.0), axis=-1, keepdims=True) * (1.0 / IDX_DIM)
        y = _rope(x * lax.rsqrt(ms + EPS) * ikg_ref[...], ri_ref, IDX_ROT // 2)
        ikf_ref[...] = y[:, :IDX_DIM]
        ika_ref[...] = y.astype(BF16)
        ikb_ref[...] = pltpu.roll(y, IDX_DIM, 1).astype(BF16)
        iw_ref[...] = x[:, IDX_DIM:IDX_DIM + IDX_HEADS] * (IDX_HEADS ** -0.5)


def _attn_project(h, g, w_pad, qg, kg, ikg, rq, ri, *, tm):
    m = h.shape[0]
    tab_tiles = rq.shape[1] // tm
    nq, nkv, niq = Q_W // PROJ_TN, KV_W // PROJ_TN, IQ_W // PROJ_TN
    row = lambda i, j: (i, 0)
    const = lambda i, j: (0, 0)
    tab = lambda i, j: (0, i % tab_tiles, 0)
    seg = lambda lo, n: (lambda i, j: (i, jnp.clip(j - lo, 0, n - 1)))
    sds = jax.ShapeDtypeStruct
    return pl.pallas_call(
        _proj_kernel,
        grid=(m // tm, PROJ_NB),
        in_specs=[
            pl.BlockSpec((tm, D_MODEL), row),
            pl.BlockSpec((1, D_MODEL), const),
            pl.BlockSpec((D_MODEL, PROJ_TN), lambda i, j: (0, j)),
            pl.BlockSpec((1, LANES), const),
            pl.BlockSpec((1, LANES), const),
            pl.BlockSpec((1, LANES), const),
            pl.BlockSpec((3, tm, LANES), tab),
            pl.BlockSpec((3, tm, LANES), tab),
        ],
        out_specs=[
            pl.BlockSpec((tm, PROJ_TN), seg(0, nq)),
            pl.BlockSpec((tm, PROJ_TN), seg(nq, nkv)),
            pl.BlockSpec((tm, PROJ_TN), seg(nq, nkv)),
            pl.BlockSpec((tm, PROJ_TN), seg(nq + nkv, nkv)),
            pl.BlockSpec((tm, PROJ_TN), seg(nq + nkv, nkv)),
            pl.BlockSpec((tm, PROJ_TN), seg(nq + 2 * nkv, niq)),
            pl.BlockSpec((tm, IDX_DIM), row),
            pl.BlockSpec((tm, LANES), row),
            pl.BlockSpec((tm, LANES), row),
            pl.BlockSpec((tm, IDX_HEADS), row),
        ],
        out_shape=[
            sds((m, Q_W), BF16), sds((m, KV_W), F32), sds((m, KV_W), BF16),
            sds((m, KV_W), F32), sds((m, KV_W), BF16), sds((m, IQ_W), BF16),
            sds((m, IDX_DIM), F32), sds((m, LANES), BF16), sds((m, LANES), BF16),
            sds((m, IDX_HEADS), F32),
        ],
        scratch_shapes=[pltpu.VMEM((tm, D_MODEL), BF16)],
        compiler_params=pltpu.CompilerParams(
            dimension_semantics=("arbitrary", "arbitrary"), vmem_limit_bytes=VMEM_LIMIT),
        name="attn_project",
    )(h, g, w_pad, qg, kg, ikg, rq, ri)


def _attn_prompt_kernel(q_ref, iq_ref, iw_ref, k_ref, v_ref, ika_ref, ikb_ref, wout_ref, h_ref,
                        out_ref, keys_sc, o_sc, m_sc, l_sc, acc_sc, *, kc, topk):
    i = pl.program_id(1)
    q0 = i * QB
    nch = (q0 + QB + kc - 1) // kc
    qpos = q0 + lax.broadcasted_iota(I32, (QB, 1), 0)
    lane = lax.broadcasted_iota(I32, (QB, kc), 1)
    iw = iw_ref[...]

    def score_chunk(c, carry):
        k0 = pl.multiple_of(c * kc, LANES)
        ka = ika_ref[pl.ds(k0, kc), :]
        kb = ikb_ref[pl.ds(k0, kc), :]
        sc = jnp.zeros((QB, kc), F32)
        for p in range(IDX_HEADS // 2):
            a = iq_ref[:, p * LANES:(p + 1) * LANES]
            sa = lax.dot_general(a, ka, _NT, preferred_element_type=F32)
            sb = lax.dot_general(a, kb, _NT, preferred_element_type=F32)
            sc = sc + iw[:, 2 * p:2 * p + 1] * jnp.maximum(sa, 0.0)
            sc = sc + iw[:, 2 * p + 1:2 * p + 2] * jnp.maximum(sb, 0.0)
        sc = sc * (IDX_DIM ** -0.5)
        sc = jnp.where(k0 + lane <= qpos, sc, NEG)
        keys_sc[:, pl.ds(k0, kc)] = _sort_key(sc)
        return carry

    lax.fori_loop(0, nch, score_chunk, 0)

    def count_ge(cand):
        def body(c, acc):
            k0 = pl.multiple_of(c * kc, LANES)
            kk = keys_sc[:, pl.ds(k0, kc)]
            return acc + jnp.sum(jnp.where(kk >= cand, 1.0, 0.0), axis=1, keepdims=True)
        return lax.fori_loop(0, nch, body, jnp.zeros((QB, 1), F32))

    thr = _kth_largest_key(count_ge, QB, float(topk))

    scale = HEAD_DIM ** -0.5
    for n in range(N_KV):
        qs = jnp.concatenate(
            [q_ref[:, (n * GROUP + g) * LANES:(n * GROUP + g + 1) * LANES] for g in range(GROUP)], axis=0)
        m_sc[...] = jnp.full(m_sc.shape, NEG, F32)
        l_sc[...] = jnp.zeros(l_sc.shape, F32)
        acc_sc[...] = jnp.zeros(acc_sc.shape, F32)

        def att_chunk(c, carry, n=n, qs=qs):
            k0 = pl.multiple_of(c * kc, LANES)
            kk = keys_sc[:, pl.ds(k0, kc)]
            sel = (kk >= thr) & (k0 + lane <= qpos)
            bias = jnp.where(sel, 0.0, NEG)
            bias = jnp.concatenate([bias] * GROUP, axis=0)
            kt = k_ref[pl.ds(k0, kc), n * LANES:(n + 1) * LANES]
            vt = v_ref[pl.ds(k0, kc), n * LANES:(n + 1) * LANES]
            s = lax.dot_general(qs, kt, _NT, preferred_element_type=F32) * scale + bias
            m_old = m_sc[...]
            m_new = jnp.maximum(m_old, jnp.max(s, axis=1, keepdims=True))
            alpha = jnp.exp(m_old - m_new)
            p = jnp.exp(s - m_new)
            l_sc[...] = alpha * l_sc[...] + jnp.sum(p, axis=1, keepdims=True)
            acc_sc[...] = alpha * acc_sc[...] + jnp.dot(p.astype(BF16), vt, preferred_element_type=F32)
            m_sc[...] = m_new
            return carry

        lax.fori_loop(0, nch, att_chunk, 0)
        o = acc_sc[...] / l_sc[...]
        for g in range(GROUP):
            hh = n * GROUP + g
            o_sc[:, hh * LANES:(hh + 1) * LANES] = o[g * QB:(g + 1) * QB].astype(BF16)

    out_ref[...] = h_ref[...] + jnp.dot(o_sc[...], wout_ref[...], preferred_element_type=F32)


def _attn_prompt(h, q, iq, iw, kb, vb, ika, ikb, wout, *, batch, tp, kc, topk):
    nqb = tp // QB
    qrow = lambda b, i: (b * nqb + i, 0)
    seq = lambda b, i: (b, 0)
    const = lambda b, i: (0, 0)
    return pl.pallas_call(
        functools.partial(_attn_prompt_kernel, kc=kc, topk=topk),
        grid=(batch, nqb),
        in_specs=[
            pl.BlockSpec((QB, Q_W), qrow),
            pl.BlockSpec((QB, IQ_W), qrow),
            pl.BlockSpec((QB, IDX_HEADS), qrow),
            pl.BlockSpec((tp, KV_W), seq),
            pl.BlockSpec((tp, KV_W), seq),
            pl.BlockSpec((tp, LANES), seq),
            pl.BlockSpec((tp, LANES), seq),
            pl.BlockSpec((Q_W, D_MODEL), const),
            pl.BlockSpec((QB, D_MODEL), qrow),
        ],
        out_specs=pl.BlockSpec((QB, D_MODEL), qrow),
        out_shape=jax.ShapeDtypeStruct(h.shape, F32),
        scratch_shapes=[
            pltpu.VMEM((QB, tp), I32),
            pltpu.VMEM((QB, Q_W), BF16),
            pltpu.VMEM((GROUP * QB, 1), F32),
            pltpu.VMEM((GROUP * QB, 1), F32),
            pltpu.VMEM((GROUP * QB, HEAD_DIM), F32),
        ],
        compiler_params=pltpu.CompilerParams(
            dimension_semantics=("arbitrary", "arbitrary"), vmem_limit_bytes=VMEM_LIMIT),
        name="attn_prompt",
    )(q, iq, iw, kb, vb, ika, ikb, wout, h)


S1_PAGES = 16
S2_PAGES = 8


def _idx_page_scores(iq, iww, page_bf):
    s = lax.dot_general(iq, page_bf, _NT, preferred_element_type=F32)
    sc = jnp.zeros((SUBLANES, PAGE), F32)
    for hh in range(IDX_HEADS):
        sl = slice(hh * SUBLANES, (hh + 1) * SUBLANES)
        sc = sc + iww[sl, :] * jnp.maximum(s[sl, :], 0.0)
    return sc * (IDX_DIM ** -0.5)


def _sample_index_kernel(pt_ref, *refs, past, n_new, topk):
    pages = refs[:S1_PAGES]
    iq_ref, iww_ref, iknew_ref, keys_ref, thr_ref = refs[S1_PAGES:]
    s = pl.program_id(1)
    iq = iq_ref[0]
    iww = iww_ref[0]
    for r in range(S1_PAGES):
        sc = _idx_page_scores(iq, iww, pages[r][...].astype(BF16))
        off = pl.multiple_of((s * S1_PAGES + r) * PAGE, PAGE)
        keys_ref[0, :, pl.ds(off, PAGE)] = _sort_key(sc)

    @pl.when(s == pl.num_programs(1) - 1)
    def _():
        sc = _idx_page_scores(iq, iww, iknew_ref[0])
        t = lax.broadcasted_iota(I32, sc.shape, 0)
        jn = lax.broadcasted_iota(I32, sc.shape, 1)
        sc = jnp.where((jn <= t) & (jn < n_new), sc, NEG)
        keys_ref[0, :, pl.ds(past, PAGE)] = _sort_key(sc)
        keys = keys_ref[0]

        def count_ge(cand):
            return jnp.sum(jnp.where(keys >= cand, 1.0, 0.0), axis=1, keepdims=True)

        thr = _kth_largest_key(count_ge, SUBLANES, float(topk))
        thr_ref[0] = jnp.broadcast_to(thr, (SUBLANES, LANES))


def _sample_index(page_table, cache_ik, layer, iq, iww, iknew, *, n_new, topk):
    nb, n_pages = page_table.shape
    steps = n_pages // S1_PAGES
    nk = (n_pages + 1) * PAGE
    page_spec = lambda r: pl.BlockSpec(
        (None, None, PAGE, IDX_DIM), lambda b, s, pt: (layer, pt[b, s * S1_PAGES + r], 0, 0))
    per_b = lambda b, s, pt: (b, 0, 0)
    return pl.pallas_call(
        functools.partial(_sample_index_kernel, past=n_pages * PAGE, n_new=n_new, topk=topk),
        grid_spec=pltpu.PrefetchScalarGridSpec(
            num_scalar_prefetch=1,
            grid=(nb, steps),
            in_specs=[page_spec(r) for r in range(S1_PAGES)] + [
                pl.BlockSpec((1, IDX_HEADS * SUBLANES, IDX_DIM), per_b),
                pl.BlockSpec((1, IDX_HEADS * SUBLANES, LANES), per_b),
                pl.BlockSpec((1, PAGE, IDX_DIM), per_b),
            ],
            out_specs=[
                pl.BlockSpec((1, SUBLANES, nk), per_b),
                pl.BlockSpec((1, SUBLANES, LANES), per_b),
            ],
        ),
        out_shape=[jax.ShapeDtypeStruct((nb, SUBLANES, nk), I32),
                   jax.ShapeDtypeStruct((nb, SUBLANES, LANES), I32)],
        compiler_params=pltpu.CompilerParams(
            dimension_semantics=("arbitrary", "arbitrary"), vmem_limit_bytes=VMEM_LIMIT),
        name="sample_index",
    )(page_table, *([cache_ik] * S1_PAGES), iq, iww, iknew)


def _sample_attend_kernel(pt_ref, *refs, n_new):
    kpages = refs[:S2_PAGES]
    vpages = refs[S2_PAGES:2 * S2_PAGES]
    (qbd_ref, keys_ref, keysnew_ref, thr_ref, knew_ref, vnew_ref, o_ref,
     m_sc, l_sc, acc_sc) = refs[2 * S2_PAGES:]
    s = pl.program_id(1)
    scale = HEAD_DIM ** -0.5
    qbd = qbd_ref[0]
    thr = thr_ref[0]

    @pl.when(s == 0)
    def _():
        m_sc[...] = jnp.full(m_sc.shape, NEG, F32)
        l_sc[...] = jnp.zeros(l_sc.shape, F32)
        acc_sc[...] = jnp.zeros(acc_sc.shape, F32)

    def page_update(kp, vp, sel):
        bias = jnp.where(sel, 0.0, NEG)
        bias = jnp.concatenate([bias] * (N_HEADS), axis=0)
        sc = lax.dot_general(qbd, kp, _NT, preferred_element_type=F32) * scale + bias
        m_old = m_sc[...]
        m_new = jnp.maximum(m_old, jnp.max(sc, axis=1, keepdims=True))
        alpha = jnp.exp(m_old - m_new)
        p = jnp.exp(sc - m_new)
        l_sc[...] = alpha * l_sc[...] + jnp.sum(p, axis=1, keepdims=True)
        acc_sc[...] = alpha * acc_sc[...] + jnp.dot(p.astype(BF16), vp, preferred_element_type=F32)
        m_sc[...] = m_new

    for r in range(S2_PAGES):
        kk = keys_ref[0, :, r * PAGE:(r + 1) * PAGE]
        page_update(kpages[r][...].astype(BF16), vpages[r][...].astype(BF16), kk >= thr)

    @pl.when(s == pl.num_programs(1) - 1)
    def _():
        kk = keysnew_ref[0]
        t = lax.broadcasted_iota(I32, kk.shape, 0)
        jn = lax.broadcasted_iota(I32, kk.shape, 1)
        page_update(knew_ref[0], vnew_ref[0], (kk >= thr) & (jn <= t) & (jn < n_new))
        o_ref[0] = acc_sc[...] / l_sc[...]


def _sample_attend(page_table, cache_k, cache_v, layer, qbd, keys, thr, knew, vnew, *, n_new):
    nb, n_pages = page_table.shape
    steps = n_pages // S2_PAGES
    page_spec = lambda r: pl.BlockSpec(
        (None, None, PAGE, KV_W), lambda b, s, pt: (layer, pt[b, s * S2_PAGES + r], 0, 0))
    per_b = lambda b, s, pt: (b, 0, 0)
    rows = N_HEADS * SUBLANES
    return pl.pallas_call(
        functools.partial(_sample_attend_kernel, n_new=n_new),
        grid_spec=pltpu.PrefetchScalarGridSpec(
            num_scalar_prefetch=1,
            grid=(nb, steps),
            in_specs=[page_spec(r) for r in range(S2_PAGES)] * 2 + [
                pl.BlockSpec((1, rows, KV_W), per_b),
                pl.BlockSpec((1, SUBLANES, S2_PAGES * PAGE), lambda b, s, pt: (b, 0, s)),
                pl.BlockSpec((1, SUBLANES, PAGE), lambda b, s, pt: (b, 0, n_pages)),
                pl.BlockSpec((1, SUBLANES, LANES), per_b),
                pl.BlockSpec((1, PAGE, KV_W), per_b),
                pl.BlockSpec((1, PAGE, KV_W), per_b),
            ],
            out_specs=pl.BlockSpec((1, rows, KV_W), per_b),
            scratch_shapes=[
                pltpu.VMEM((rows, 1), F32),
                pltpu.VMEM((rows, 1), F32),
                pltpu.VMEM((rows, KV_W), F32),
            ],
        ),
        out_shape=jax.ShapeDtypeStruct((nb, rows, KV_W), F32),
        compiler_params=pltpu.CompilerParams(
            dimension_semantics=("arbitrary", "arbitrary"), vmem_limit_bytes=VMEM_LIMIT),
        name="sample_attend",
    )(page_table, *([cache_k] * S2_PAGES), *([cache_v] * S2_PAGES), qbd, keys, keys, thr, knew, vnew)


def _matmul_res_kernel(h_ref, x_ref, w_ref, o_ref):
    o_ref[...] = h_ref[...] + jnp.dot(x_ref[...], w_ref[...], preferred_element_type=F32)


def _matmul_res(h, x, w, *, tn):
    m, n = h.shape
    kdim = x.shape[1]
    return pl.pallas_call(
        _matmul_res_kernel,
        grid=(n // tn,),
        in_specs=[pl.BlockSpec((m, tn), lambda j: (0, j)),
                  pl.BlockSpec((m, kdim), lambda j: (0, 0)),
                  pl.BlockSpec((kdim, tn), lambda j: (0, j))],
        out_specs=pl.BlockSpec((m, tn), lambda j: (0, j)),
        out_shape=jax.ShapeDtypeStruct((m, n), F32),
        compiler_params=pltpu.CompilerParams(dimension_semantics=("arbitrary",)),
        name="matmul_residual",
    )(h, x, w)


def _conv_rows(ext_sc, u, init_ref, carry_sc, c, first, cw_ref, cb_ref, tail_ref, *,
               width, shift, d0, tail_end):
    tm = u.shape[0]
    ext_sc[d0:d0 + tm, :] = u

    @pl.when(first)
    def _():
        ext_sc[0:d0, :] = init_ref[0]

    @pl.when(jnp.logical_not(first))
    def _():
        ext_sc[0:d0, :] = carry_sc[c]

    y = cb_ref[...] + cw_ref[width - 1:width, :] * u
    for j in range(1, width):
        y = y + cw_ref[width - 1 - j:width - j, :] * ext_sc[d0 - j * shift:d0 - j * shift + tm, :]
    carry_sc[c] = ext_sc[tm:tm + d0, :]
    tail_ref[0] = ext_sc[tail_end:tail_end + d0, :]
    return y


def _ffn_kernel(h_ref, g_ref, wu_ref, wg_ref, cw_ref, cb_ref, wd_ref, st_ref, out_ref, tail_ref,
                xn_sc, acc_sc, ext_sc, carry_sc, *, shift, d0, tiles_per_seq, tail_end):
    i = pl.program_id(0)
    c = pl.program_id(1)

    @pl.when(c == 0)
    def _():
        xn_sc[...] = _rms_rows(h_ref[...], g_ref[...]).astype(BF16)
        acc_sc[...] = jnp.zeros(acc_sc.shape, F32)

    xn = xn_sc[...]
    u = jnp.dot(xn, wu_ref[...], preferred_element_type=F32)
    conv = _conv_rows(ext_sc, u, st_ref, carry_sc, c, (i % tiles_per_seq) == 0, cw_ref, cb_ref,
                      tail_ref, width=FFN_CONV, shift=shift, d0=d0, tail_end=tail_end)
    mid = _gelu(conv) * jnp.dot(xn, wg_ref[...], preferred_element_type=F32)
    acc_sc[...] += jnp.dot(mid.astype(BF16), wd_ref[...], preferred_element_type=F32)

    @pl.when(c == pl.num_programs(1) - 1)
    def _():
        out_ref[...] = h_ref[...] + acc_sc[...]


def _conv_ffn(h, g, wu, wg, cw, cb, wd, state, *, tm, tf, shift, tiles_per_seq, tail_end):
    m = h.shape[0]
    d0 = state.shape[1]
    n_tiles, n_chunks = m // tm, D_FF // tf
    row = lambda i, c: (i, 0)
    return pl.pallas_call(
        functools.partial(_ffn_kernel, shift=shift, d0=d0, tiles_per_seq=tiles_per_seq, tail_end=tail_end),
        grid=(n_tiles, n_chunks),
        in_specs=[
            pl.BlockSpec((tm, D_MODEL), row),
            pl.BlockSpec((1, D_MODEL), lambda i, c: (0, 0)),
            pl.BlockSpec((D_MODEL, tf), lambda i, c: (0, c)),
            pl.BlockSpec((D_MODEL, tf), lambda i, c: (0, c)),
            pl.BlockSpec((FFN_CONV, tf), lambda i, c: (0, c)),
            pl.BlockSpec((1, tf), lambda i, c: (0, c)),
            pl.BlockSpec((tf, D_MODEL), lambda i, c: (c, 0)),
            pl.BlockSpec((1, d0, tf), lambda i, c: (i // tiles_per_seq, 0, c)),
        ],
        out_specs=[
            pl.BlockSpec((tm, D_MODEL), row),
            pl.BlockSpec((1, d0, tf), lambda i, c: (i, 0, c)),
        ],
        out_shape=[jax.ShapeDtypeStruct((m, D_MODEL), F32),
                   jax.ShapeDtypeStruct((n_tiles, d0, D_FF), F32)],
        scratch_shapes=[
            pltpu.VMEM((tm, D_MODEL), BF16),
            pltpu.VMEM((tm, D_MODEL), F32),
            pltpu.VMEM((d0 + tm, tf), F32),
            pltpu.VMEM((n_chunks, d0, tf), F32),
        ],
        compiler_params=pltpu.CompilerParams(
            dimension_semantics=("arbitrary", "arbitrary"), vmem_limit_bytes=VMEM_LIMIT),
        name="conv_ffn",
    )(h, g, wu, wg, cw, cb, wd, state)


def _scan_rows(a_sc, b_sc, hc, *, shift):
    tm = a_sc.shape[0]
    ngroups = tm // SUBLANES
    row = lax.broadcasted_iota(I32, (SUBLANES, a_sc.shape[1]), 0)

    def body(gi, hc):
        r0 = pl.multiple_of(gi * SUBLANES, SUBLANES)
        a = a_sc[pl.ds(r0, SUBLANES), :]
        b = b_sc[pl.ds(r0, SUBLANES), :]
        if shift == 1:
            for s in (1, 2, 4):
                keep = row >= s
                a_sh = jnp.where(keep, pltpu.roll(a, s, 0), 1.0)
                b_sh = jnp.where(keep, pltpu.roll(b, s, 0), 0.0)
                b = a * b_sh + b
                a = a * a_sh
        hrows = a * hc + b
        b_sc[pl.ds(r0, SUBLANES), :] = hrows
        if shift == 1:
            return jnp.broadcast_to(hrows[SUBLANES - 1:SUBLANES, :], hrows.shape)
        return hrows

    return lax.fori_loop(0, ngroups, body, hc)


def _lru_kernel(h_ref, g_ref, wx_ref, wy_ref, cw_ref, cb_ref, wa_ref, ba_ref, wi_ref, bi_ref,
                lam_ref, wo_ref, h0_ref, st_ref, out_ref, htail_ref, utail_ref,
                xn_sc, acc_sc, ext_sc, a_sc, b_sc, carry_u, carry_h, *,
                shift, d0, tiles_per_seq, tail_end):
    i = pl.program_id(0)
    c = pl.program_id(1)
    first = (i % tiles_per_seq) == 0

    @pl.when(c == 0)
    def _():
        xn_sc[...] = _rms_rows(h_ref[...], g_ref[...]).astype(BF16)
        acc_sc[...] = jnp.zeros(acc_sc.shape, F32)

    xn = xn_sc[...]
    u = jnp.dot(xn, wx_ref[...], preferred_element_type=F32)
    conv = _conv_rows(ext_sc, u, st_ref, carry_u, c, first, cw_ref, cb_ref, utail_ref,
                      width=LRU_CONV, shift=shift, d0=d0, tail_end=tail_end)
    cbf = conv.astype(BF16)
    r = _sigmoid(jnp.dot(cbf, wa_ref[0], preferred_element_type=F32) + ba_ref[...])
    ig = _sigmoid(jnp.dot(cbf, wi_ref[0], preferred_element_type=F32) + bi_ref[...])
    lam = lam_ref[...]
    log_sig = jnp.minimum(lam, 0.0) - jnp.log(1.0 + jnp.exp(-jnp.abs(lam)))
    log_a = LRU_C * r * log_sig
    a = jnp.exp(log_a)
    b = jnp.sqrt(-jnp.tanh(log_a) * (a * a + 1.0)) * (ig * conv)
    a_sc[...] = a
    b_sc[...] = b

    @pl.when(first)
    def _():
        carry_h[c] = h0_ref[0]

    carry_h[c] = _scan_rows(a_sc, b_sc, carry_h[c], shift=shift)
    htail_ref[0] = b_sc[tail_end - SUBLANES:tail_end, :]
    gate = _gelu(jnp.dot(xn, wy_ref[...], preferred_element_type=F32))
    acc_sc[...] += jnp.dot((b_sc[...] * gate).astype(BF16), wo_ref[...], preferred_element_type=F32)

    @pl.when(c == pl.num_programs(1) - 1)
    def _():
        out_ref[...] = h_ref[...] + acc_sc[...]


def _lru_block(h, g, wx, wy, cw, cb, wa, ba, wi, bi, lam, wo, h0, state, *,
               tm, shift, tiles_per_seq, tail_end):
    m = h.shape[0]
    d0 = state.shape[1]
    n_tiles = m // tm
    bd = RNN_BD
    row = lambda i, c: (i, 0)
    col = lambda i, c: (0, c)
    return pl.pallas_call(
        functools.partial(_lru_kernel, shift=shift, d0=d0, tiles_per_seq=tiles_per_seq, tail_end=tail_end),
        grid=(n_tiles, RNN_BLOCKS),
        in_specs=[
            pl.BlockSpec((tm, D_MODEL), row),
            pl.BlockSpec((1, D_MODEL), lambda i, c: (0, 0)),
            pl.BlockSpec((D_MODEL, bd), col),
            pl.BlockSpec((D_MODEL, bd), col),
            pl.BlockSpec((LRU_CONV, bd), col),
            pl.BlockSpec((1, bd), col),
            pl.BlockSpec((1, bd, bd), lambda i, c: (c, 0, 0)),
            pl.BlockSpec((1, bd), col),
            pl.BlockSpec((1, bd, bd), lambda i, c: (c, 0, 0)),
            pl.BlockSpec((1, bd), col),
            pl.BlockSpec((1, bd), col),
            pl.BlockSpec((bd, D_MODEL), lambda i, c: (c, 0)),
            pl.BlockSpec((1, SUBLANES, bd), lambda i, c: (i // tiles_per_seq, 0, c)),
            pl.BlockSpec((1, d0, bd), lambda i, c: (i // tiles_per_seq, 0, c)),
        ],
        out_specs=[
            pl.BlockSpec((tm, D_MODEL), row),
            pl.BlockSpec((1, SUBLANES, bd), lambda i, c: (i, 0, c)),
            pl.BlockSpec((1, d0, bd), lambda i, c: (i, 0, c)),
        ],
        out_shape=[jax.ShapeDtypeStruct((m, D_MODEL), F32),
                   jax.ShapeDtypeStruct((n_tiles, SUBLANES, D_MODEL), F32),
                   jax.ShapeDtypeStruct((n_tiles, d0, D_MODEL), F32)],
        scratch_shapes=[
            pltpu.VMEM((tm, D_MODEL), BF16),
            pltpu.VMEM((tm, D_MODEL), F32),
            pltpu.VMEM((d0 + tm, bd), F32),
            pltpu.VMEM((tm, bd), F32),
            pltpu.VMEM((tm, bd), F32),
            pltpu.VMEM((RNN_BLOCKS, d0, bd), F32),
            pltpu.VMEM((RNN_BLOCKS, SUBLANES, bd), F32),
        ],
        compiler_params=pltpu.CompilerParams(
            dimension_semantics=("arbitrary", "arbitrary"), vmem_limit_bytes=VMEM_LIMIT),
        name="rg_lru",
    )(h, g, wx, wy, cw, cb, wa, ba, wi, bi, lam, wo, h0, state)


def _rope_tables(pos, rot, period):
    half = rot // 2
    inv = 1.0 / (ROPE_THETA ** (jnp.arange(half, dtype=F32) * (2.0 / rot)))
    ang = pos.astype(F32)[:, None] * inv[None, :]
    lane = jnp.arange(LANES) % period
    cos = jnp.cos(ang)[:, lane % half]
    sin = jnp.sin(ang)[:, lane % half]
    return jnp.stack([
        jnp.where(lane < rot, cos, 1.0),
        jnp.where(lane < half, -sin, 0.0),
        jnp.where((lane >= half) & (lane < rot), sin, 0.0),
    ])


def _time_major(x):
    return jnp.swapaxes(x, 0, 1)


def kernel(x_prompt, x_sample, cache_k, cache_v, cache_idx_k, state_lru_h, state_lru_conv, state_ffn_conv, page_table, meta_tokens, norm1_g, norm2_g, attn_w_in, attn_q_norm_g, attn_k_norm_g, attn_idx_k_norm_g, attn_w_out, lru_w_x, lru_w_y, lru_conv_w, lru_conv_b, lru_wa, lru_ba, lru_wi, lru_bi, lru_lambda, lru_w_out, ffn_w_up, ffn_w_gate, ffn_dw_w, ffn_dw_b, ffn_w_down):
    batch, seq, _ = x_prompt.shape
    nb, ns, _ = x_sample.shape
    depth = norm1_g.shape[0]
    t_real = seq + N_META
    tp = _round_up(t_real, QB)
    n_pages = page_table.shape[1]
    past = n_pages * PAGE
    topk_p = min(TOPK_MAX, t_real // 4)
    topk_s = min(TOPK_MAX, (past + ns) // 4)
    assert nb == SUBLANES and ns <= SUBLANES and n_pages % S1_PAGES == 0 and n_pages % S2_PAGES == 0

    tiles_per_seq = 6
    tm = tp // tiles_per_seq
    assert tm * tiles_per_seq == tp and tm % (2 * SUBLANES) == 0
    tail_end = t_real - (tiles_per_seq - 1) * tm
    assert tail_end % SUBLANES == 0 and tail_end >= SUBLANES
    kc = 3 * LANES
    assert tp % kc == 0 and kc >= topk_p
    ms = nb * ns

    hp = jnp.concatenate([
        jnp.broadcast_to(meta_tokens[None], (batch, N_META, D_MODEL)), x_prompt,
        jnp.zeros((batch, tp - t_real, D_MODEL), F32)], axis=1).reshape(batch * tp, D_MODEL)
    hs = _time_major(x_sample).reshape(ms, D_MODEL)

    rq_p = _rope_tables(jnp.arange(tp), ROT, HEAD_DIM)
    ri_p = _rope_tables(jnp.arange(tp), IDX_ROT, IDX_DIM)
    pos_s = jnp.repeat(past + jnp.arange(ns), nb)
    rq_s = _rope_tables(pos_s, ROT, HEAD_DIM)
    ri_s = _rope_tables(pos_s, IDX_ROT, IDX_DIM)

    cache_k4 = cache_k.reshape(cache_k.shape[:3] + (KV_W,))
    cache_v4 = cache_v.reshape(cache_v.shape[:3] + (KV_W,))

    kp, vp, ikp, ksl, vsl, iksl = [], [], [], [], [], []
    hpl, cpl, hsl, csl, fpl, fsl = [], [], [], [], [], []
    last_tiles = jnp.arange(batch) * tiles_per_seq + tiles_per_seq - 1

    for layer in range(depth):
        mi = layer // 2
        g1 = norm1_g[layer][None]
        if layer % 2 == 0:
            w_in = jnp.pad(attn_w_in[mi], ((0, 0), (0, PROJ_NB * PROJ_TN - PROJ_W))).astype(BF16)
            w_out = attn_w_out[mi].astype(BF16)
            qg = attn_q_norm_g[mi][None]
            kg = attn_k_norm_g[mi][None]
            ikg = jnp.pad(attn_idx_k_norm_g[mi], (0, LANES - IDX_DIM))[None]

            q, kf, kb, vf, vb, iq, ikf, ika, ikb, iw = _attn_project(
                hp, g1, w_in, qg, kg, ikg, rq_p, ri_p, tm=tm)
            hp = _attn_prompt(hp, q, iq, iw, kb, vb, ika, ikb, w_out,
                              batch=batch, tp=tp, kc=kc, topk=topk_p)
            kp.append(kf.reshape(batch, tp, N_KV, HEAD_DIM)[:, :t_real])
            vp.append(vf.reshape(batch, tp, N_KV, HEAD_DIM)[:, :t_real])
            ikp.append(ikf.reshape(batch, tp, IDX_DIM)[:, :t_real])

            q, kf, kb, vf, vb, iq, ikf, ika, ikb, iw = _attn_project(
                hs, g1, w_in, qg, kg, ikg, rq_s, ri_s, tm=ms)
            ksl.append(_time_major(kf.reshape(ns, nb, N_KV, HEAD_DIM)))
            vsl.append(_time_major(vf.reshape(ns, nb, N_KV, HEAD_DIM)))
            iksl.append(_time_major(ikf.reshape(ns, nb, IDX_DIM)))

            pad_q = lambda x: jnp.pad(x, [(0, 0)] * (x.ndim - 2) + [(0, SUBLANES - ns), (0, 0)])
            iq_b = pad_q(iq.reshape(ns, nb, IDX_HEADS, IDX_DIM).transpose(1, 2, 0, 3))
            iq_b = iq_b.reshape(nb, IDX_HEADS * SUBLANES, IDX_DIM)
            iw_b = jnp.pad(iw.reshape(ns, nb, IDX_HEADS).transpose(1, 2, 0), ((0, 0), (0, 0), (0, SUBLANES - ns)))
            iww = jnp.broadcast_to(iw_b.reshape(nb, IDX_HEADS * SUBLANES, 1), (nb, IDX_HEADS * SUBLANES, LANES))
            new_page = lambda x: jnp.pad(_time_major(x.reshape(ns, nb, -1)), ((0, 0), (0, PAGE - ns), (0, 0)))
            iknew = new_page(ika[:, :IDX_DIM])
            keys, thr = _sample_index(page_table, cache_idx_k, mi, iq_b, iww, iknew, n_new=ns, topk=topk_s)

            q_b = pad_q(q.reshape(ns, nb, N_KV, GROUP, HEAD_DIM).transpose(1, 2, 3, 0, 4))
            qbd = q_b[:, :, :, :, None, :] * jnp.eye(N_KV, dtype=BF16)[None, :, None, None, :, None]
            qbd = qbd.reshape(nb, N_HEADS * SUBLANES, KV_W)
            o = _sample_attend(page_table, cache_k4, cache_v4, mi, qbd, keys, thr,
                               new_page(kb), new_page(vb), n_new=ns)
            o = o.reshape(nb, N_KV, GROUP, SUBLANES, N_KV, HEAD_DIM)
            o = jnp.stack([o[:, n, :, :ns, n, :] for n in range(N_KV)], axis=1)
            o = o.transpose(3, 0, 1, 2, 4).reshape(ms, Q_W).astype(BF16)
            hs = _matmul_res(hs, o, w_out, tn=512)
        else:
            lw = dict(
                g=g1, wx=lru_w_x[mi].astype(BF16), wy=lru_w_y[mi].astype(BF16),
                cw=lru_conv_w[mi], cb=lru_conv_b[mi][None],
                wa=lru_wa[mi].astype(BF16), ba=lru_ba[mi][None],
                wi=lru_wi[mi].astype(BF16), bi=lru_bi[mi][None],
                lam=lru_lambda[mi][None], wo=lru_w_out[mi].astype(BF16))
            hp, htail, utail = _lru_block(
                hp, h0=jnp.zeros((batch, SUBLANES, D_MODEL), F32),
                state=jnp.zeros((batch, SUBLANES, D_MODEL), F32),
                tm=tm, shift=1, tiles_per_seq=tiles_per_seq, tail_end=tail_end, **lw)
            hpl.append(htail[last_tiles, SUBLANES - 1])
            cpl.append(utail[last_tiles, SUBLANES - (LRU_CONV - 1):])
            d0s = (LRU_CONV - 1) * nb
            hs, htail, utail = _lru_block(
                hs, h0=state_lru_h[mi][None],
                state=_time_major(state_lru_conv[mi]).reshape(1, d0s, D_MODEL),
                tm=ms, shift=nb, tiles_per_seq=1, tail_end=ms, **lw)
            hsl.append(htail[0])
            csl.append(_time_major(utail.reshape(LRU_CONV - 1, nb, D_MODEL)))

        fw = dict(g=norm2_g[layer][None], wu=ffn_w_up[layer].astype(BF16), wg=ffn_w_gate[layer].astype(BF16),
                  cw=ffn_dw_w[layer], cb=ffn_dw_b[layer][None], wd=ffn_w_down[layer].astype(BF16))
        hp, tail = _conv_ffn(hp, state=jnp.zeros((batch, SUBLANES, D_FF), F32),
                             tm=tm, tf=512, shift=1, tiles_per_seq=tiles_per_seq, tail_end=tail_end, **fw)
        fpl.append(tail[last_tiles, SUBLANES - (FFN_CONV - 1):])
        d0s = (FFN_CONV - 1) * nb
        hs, tail = _conv_ffn(hs, state=_time_major(state_ffn_conv[layer]).reshape(1, d0s, D_FF),
                             tm=ms, tf=512, shift=nb, tiles_per_seq=1, tail_end=ms, **fw)
        fsl.append(_time_major(tail.reshape(FFN_CONV - 1, nb, D_FF)))

    y_prompt = hp.reshape(batch, tp, D_MODEL)[:, N_META:t_real]
    y_sample = _time_major(hs.reshape(ns, nb, D_MODEL))
    return (y_prompt, y_sample, jnp.stack(kp), jnp.stack(vp), jnp.stack(ikp),
            jnp.stack(ksl), jnp.stack(vsl), jnp.stack(iksl),
            jnp.stack(hpl), jnp.stack(cpl), jnp.stack(hsl), jnp.stack(csl),
            jnp.stack(fpl), jnp.stack(fsl))
```

```python
import functools

import jax
import jax.numpy as jnp
from jax import lax
from jax.experimental import pallas as pl
from jax.experimental.pallas import tpu as pltpu

F32 = jnp.float32
BF16 = jnp.bfloat16
I32 = jnp.int32

D_MODEL = 2048
N_META = 16
N_HEADS = 16
HEAD_DIM = 128
N_KV = 4
GROUP = N_HEADS // N_KV
ROT = HEAD_DIM // 4
IDX_HEADS = 16
IDX_DIM = 64
IDX_ROT = IDX_DIM // 4
ROPE_THETA = 500000.0
TOPK_MAX = 256
PAGE = 128
Q_W = N_HEADS * HEAD_DIM
KV_W = N_KV * HEAD_DIM
IQ_W = IDX_HEADS * IDX_DIM
PROJ_W = Q_W + 2 * KV_W + IQ_W + IDX_DIM + IDX_HEADS
D_FF = 3 * D_MODEL
FFN_CONV = 3
LRU_CONV = 4
LRU_C = 8.0
RNN_BLOCKS = 8
RNN_BD = D_MODEL // RNN_BLOCKS
EPS = 1e-6
NEG = -1e30

LANES = 128
SUBLANES = 8
QB = 128
PROJ_TN = 256
PROJ_NB = -(-PROJ_W // PROJ_TN)
VMEM_LIMIT = 56 * 1024 * 1024

_NT = (((1,), (1,)), ((), ()))
_INT_MIN = -2 ** 31


def _round_up(x, m):
    return -(-x // m) * m


def _rms_rows(x, g):
    ms = jnp.mean(x * x, axis=-1, keepdims=True)
    return x * lax.rsqrt(ms + EPS) * g


def _gelu(x):
    return 0.5 * x * (1.0 + jnp.tanh(0.7978845608028654 * (x + 0.044715 * (x * x * x))))


def _sigmoid(x):
    return 1.0 / (1.0 + jnp.exp(-x))


def _sort_key(x):
    b = lax.bitcast_convert_type(x + 0.0, I32)
    return b ^ ((b >> 31) & 0x7FFFFFFF)


def _kth_largest_key(count_ge, shape, k):
    def bit_body(bi, t):
        cand = t | jnp.left_shift(jnp.int32(1), 31 - bi)
        cnt = count_ge(cand ^ _INT_MIN)
        return jnp.where(cnt >= k, cand, t)
    t = lax.fori_loop(0, 32, bit_body, jnp.zeros(shape, I32))
    return t ^ _INT_MIN


def _rope(x, tab_ref, sh):
    return (x * tab_ref[0] + pltpu.roll(x, LANES - sh, 1) * tab_ref[1]
            + pltpu.roll(x, sh, 1) * tab_ref[2])


def _proj_kernel(h_ref, g_ref, w_ref, qg_ref, kg_ref, ikg_ref, rq_ref, ri_ref,
                 q_ref, kf_ref, kb_ref, vf_ref, vb_ref, iq_ref, ikf_ref, ika_ref, ikb_ref, iw_ref,
                 xn_sc):
    j = pl.program_id(1)

    @pl.when(j == 0)
    def _():
        xn_sc[...] = _rms_rows(h_ref[...], g_ref[...]).astype(BF16)

    z = jnp.dot(xn_sc[...], w_ref[...], preferred_element_type=F32)
    nq, nkv, niq = Q_W // PROJ_TN, KV_W // PROJ_TN, IQ_W // PROJ_TN

    @pl.when(j < nq)
    def _():
        for s in range(PROJ_TN // LANES):
            x = _rope(_rms_rows(z[:, s * LANES:(s + 1) * LANES], qg_ref[...]), rq_ref, ROT // 2)
            q_ref[:, s * LANES:(s + 1) * LANES] = x.astype(BF16)

    @pl.when((j >= nq) & (j < nq + nkv))
    def _():
        for s in range(PROJ_TN // LANES):
            x = _rope(_rms_rows(z[:, s * LANES:(s + 1) * LANES], kg_ref[...]), rq_ref, ROT // 2)
            kf_ref[:, s * LANES:(s + 1) * LANES] = x
            kb_ref[:, s * LANES:(s + 1) * LANES] = x.astype(BF16)

    @pl.when((j >= nq + nkv) & (j < nq + 2 * nkv))
    def _():
        vf_ref[...] = z
        vb_ref[...] = z.astype(BF16)

    @pl.when((j >= nq + 2 * nkv) & (j < nq + 2 * nkv + niq))
    def _():
        for s in range(PROJ_TN // LANES):
            x = _rope(z[:, s * LANES:(s + 1) * LANES], ri_ref, IDX_ROT // 2)
            iq_ref[:, s * LANES:(s + 1) * LANES] = x.astype(BF16)

    @pl.when(j == nq + 2 * nkv + niq)
    def _():
        x = z[:, :LANES]
        lane = lax.broadcasted_iota(I32, x.shape, 1)
        ms = jnp.sum(jnp.where(lane < IDX_DIM, x * x, 0.0), axis=-1, keepdims=True) * (1.0 / IDX_DIM)
        y = _rope(x * lax.rsqrt(ms + EPS) * ikg_ref[...], ri_ref, IDX_ROT // 2)
        ikf_ref[...] = y[:, :IDX_DIM]
        ika_ref[...] = y.astype(BF16)
        ikb_ref[...] = pltpu.roll(y, IDX_DIM, 1).astype(BF16)
        iw_ref[...] = x[:, IDX_DIM:IDX_DIM + IDX_HEADS] * (IDX_HEADS ** -0.5)


def _attn_project(h, g, w_pad, qg, kg, ikg, rq, ri, *, tm):
    m = h.shape[0]
    tab_tiles = rq.shape[1] // tm
    nq, nkv, niq = Q_W // PROJ_TN, KV_W // PROJ_TN, IQ_W // PROJ_TN
    row = lambda i, j: (i, 0)
    const = lambda i, j: (0, 0)
    tab = lambda i, j: (0, i % tab_tiles, 0)
    seg = lambda lo, n: (lambda i, j: (i, jnp.clip(j - lo, 0, n - 1)))
    sds = jax.ShapeDtypeStruct
    return pl.pallas_call(
        _proj_kernel,
        grid=(m // tm, PROJ_NB),
        in_specs=[
            pl.BlockSpec((tm, D_MODEL), row),
            pl.BlockSpec((1, D_MODEL), const),
            pl.BlockSpec((D_MODEL, PROJ_TN), lambda i, j: (0, j)),
            pl.BlockSpec((1, LANES), const),
            pl.BlockSpec((1, LANES), const),
            pl.BlockSpec((1, LANES), const),
            pl.BlockSpec((3, tm, LANES), tab),
            pl.BlockSpec((3, tm, LANES), tab),
        ],
        out_specs=[
            pl.BlockSpec((tm, PROJ_TN), seg(0, nq)),
            pl.BlockSpec((tm, PROJ_TN), seg(nq, nkv)),
            pl.BlockSpec((tm, PROJ_TN), seg(nq, nkv)),
            pl.BlockSpec((tm, PROJ_TN), seg(nq + nkv, nkv)),
            pl.BlockSpec((tm, PROJ_TN), seg(nq + nkv, nkv)),
            pl.BlockSpec((tm, PROJ_TN), seg(nq + 2 * nkv, niq)),
            pl.BlockSpec((tm, IDX_DIM), row),
            pl.BlockSpec((tm, LANES), row),
            pl.BlockSpec((tm, LANES), row),
            pl.BlockSpec((tm, IDX_HEADS), row),
        ],
        out_shape=[
            sds((m, Q_W), BF16), sds((m, KV_W), F32), sds((m, KV_W), BF16),
            sds((m, KV_W), F32), sds((m, KV_W), BF16), sds((m, IQ_W), BF16),
            sds((m, IDX_DIM), F32), sds((m, LANES), BF16), sds((m, LANES), BF16),
            sds((m, IDX_HEADS), F32),
        ],
        scratch_shapes=[pltpu.VMEM((tm, D_MODEL), BF16)],
        compiler_params=pltpu.CompilerParams(
            dimension_semantics=("arbitrary", "arbitrary"), vmem_limit_bytes=VMEM_LIMIT),
        name="attn_project",
    )(h, g, w_pad, qg, kg, ikg, rq, ri)


def _attn_prompt_kernel(q_ref, iq_ref, iw_ref, k_ref, v_ref, ika_ref, ikb_ref, wout_ref, h_ref,
                        out_ref, keys_sc, o_sc, m_sc, l_sc, acc_sc, *, kc, topk):
    i = pl.program_id(1)
    q0 = i * QB
    nch = (q0 + QB + kc - 1) // kc
    qpos = q0 + lax.broadcasted_iota(I32, (QB, 1), 0)
    lane = lax.broadcasted_iota(I32, (QB, kc), 1)
    iw = iw_ref[...]

    def score_chunk(c, carry):
        k0 = pl.multiple_of(c * kc, LANES)
        ka = ika_ref[pl.ds(k0, kc), :]
        kb = ikb_ref[pl.ds(k0, kc), :]
        sc = jnp.zeros((QB, kc), F32)
        for p in range(IDX_HEADS // 2):
            a = iq_ref[:, p * LANES:(p + 1) * LANES]
            sa = lax.dot_general(a, ka, _NT, preferred_element_type=F32)
            sb = lax.dot_general(a, kb, _NT, preferred_element_type=F32)
            sc = sc + iw[:, 2 * p:2 * p + 1] * jnp.maximum(sa, 0.0)
            sc = sc + iw[:, 2 * p + 1:2 * p + 2] * jnp.maximum(sb, 0.0)
        sc = sc * (IDX_DIM ** -0.5)
        sc = jnp.where(k0 + lane <= qpos, sc, NEG)
        keys_sc[:, pl.ds(k0, kc)] = _sort_key(sc)
        return carry

    lax.fori_loop(0, nch, score_chunk, 0)

    def count_ge(cand):
        def body(c, acc):
            k0 = pl.multiple_of(c * kc, LANES)
            kk = keys_sc[:, pl.ds(k0, kc)]
            return acc + jnp.sum(jnp.where(kk >= cand, 1.0, 0.0), axis=1, keepdims=True)
        return lax.fori_loop(0, nch, body, jnp.zeros((QB, 1), F32))

    thr = _kth_largest_key(count_ge, (QB, 1), float(topk))

    scale = HEAD_DIM ** -0.5
    for n in range(N_KV):
        qs = jnp.concatenate(
            [q_ref[:, (n * GROUP + g) * LANES:(n * GROUP + g + 1) * LANES] for g in range(GROUP)], axis=0)
        m_sc[...] = jnp.full(m_sc.shape, NEG, F32)
        l_sc[...] = jnp.zeros(l_sc.shape, F32)
        acc_sc[...] = jnp.zeros(acc_sc.shape, F32)

        def att_chunk(c, carry, n=n, qs=qs):
            k0 = pl.multiple_of(c * kc, LANES)
            kk = keys_sc[:, pl.ds(k0, kc)]
            sel = (kk >= thr) & (k0 + lane <= qpos)
            bias = jnp.where(sel, 0.0, NEG)
            bias = jnp.concatenate([bias] * GROUP, axis=0)
            kt = k_ref[pl.ds(k0, kc), n * LANES:(n + 1) * LANES]
            vt = v_ref[pl.ds(k0, kc), n * LANES:(n + 1) * LANES]
            s = lax.dot_general(qs, kt, _NT, preferred_element_type=F32) * scale + bias
            m_old = m_sc[...]
            m_new = jnp.maximum(m_old, jnp.max(s, axis=1, keepdims=True))
            alpha = jnp.exp(m_old - m_new)
            p = jnp.exp(s - m_new)
            l_sc[...] = alpha * l_sc[...] + jnp.sum(p, axis=1, keepdims=True)
            acc_sc[...] = alpha * acc_sc[...] + jnp.dot(p.astype(BF16), vt, preferred_element_type=F32)
            m_sc[...] = m_new
            return carry

        lax.fori_loop(0, nch, att_chunk, 0)
        o = acc_sc[...] / l_sc[...]
        for g in range(GROUP):
            hh = n * GROUP + g
            o_sc[:, hh * LANES:(hh + 1) * LANES] = o[g * QB:(g + 1) * QB].astype(BF16)

    out_ref[...] = h_ref[...] + jnp.dot(o_sc[...], wout_ref[...], preferred_element_type=F32)


def _attn_prompt_bounded_kernel(shift_ref, q_ref, iq_ref, iw_ref, k_ref, v_ref, ika_ref, ikb_ref,
                                wout_ref, h_ref, out_ref, keys_sc, iwb_sc, o_sc, acc_sc, *, kc, topk):
    i = pl.program_id(1)
    q0 = i * QB
    nch = (q0 + QB + kc - 1) // kc
    nlt = kc // LANES
    qpos = q0 + lax.broadcasted_iota(I32, (QB, LANES), 0)
    lane = lax.broadcasted_iota(I32, (QB, LANES), 1)
    iw = iw_ref[...]
    for hh in range(IDX_HEADS):
        iwb_sc[:, hh * LANES:(hh + 1) * LANES] = jnp.broadcast_to(iw[:, hh:hh + 1], (QB, LANES))

    def score_chunk(c, carry):
        k0 = pl.multiple_of(c * kc, LANES)
        ka = ika_ref[pl.ds(k0, kc), :]
        kb = ikb_ref[pl.ds(k0, kc), :]
        sc = [jnp.zeros((QB, LANES), F32) for _ in range(nlt)]
        for p in range(IDX_HEADS // 2):
            a = iq_ref[:, p * LANES:(p + 1) * LANES]
            sa = lax.dot_general(a, ka, _NT, preferred_element_type=F32)
            sb = lax.dot_general(a, kb, _NT, preferred_element_type=F32)
            wa = iwb_sc[:, 2 * p * LANES:(2 * p + 1) * LANES]
            wb = iwb_sc[:, (2 * p + 1) * LANES:(2 * p + 2) * LANES]
            for lt in range(nlt):
                sl = slice(lt * LANES, (lt + 1) * LANES)
                sc[lt] = sc[lt] + wa * jnp.maximum(sa[:, sl], 0.0) + wb * jnp.maximum(sb[:, sl], 0.0)
        for lt in range(nlt):
            s = jnp.where(k0 + lt * LANES + lane <= qpos, sc[lt] * (IDX_DIM ** -0.5), NEG)
            keys_sc[:, pl.ds(k0 + lt * LANES, LANES)] = _sort_key(s)
        return carry

    lax.fori_loop(0, nch, score_chunk, 0)

    def count_ge(cand):
        def body(c, acc):
            k0 = pl.multiple_of(c * kc, LANES)
            for lt in range(nlt):
                kk = keys_sc[:, pl.ds(k0 + lt * LANES, LANES)]
                acc = acc + jnp.where(kk >= cand, 1.0, 0.0)
            return acc
        acc = lax.fori_loop(0, nch, body, jnp.zeros((QB, LANES), F32))
        return jnp.broadcast_to(jnp.sum(acc, axis=1, keepdims=True), (QB, LANES))

    thr = _kth_largest_key(count_ge, (QB, LANES), float(topk))

    scale = HEAD_DIM ** -0.5
    shift = shift_ref[0, 0]
    acc_sc[...] = jnp.zeros(acc_sc.shape, F32)
    ones_col = jnp.where(lax.broadcasted_iota(I32, (kc, LANES), 1) == 0, 1.0, 0.0).astype(BF16)

    def att_chunk(c, carry):
        k0 = pl.multiple_of(c * kc, LANES)
        bias = []
        for lt in range(nlt):
            kk = keys_sc[:, pl.ds(k0 + lt * LANES, LANES)]
            sel = (kk >= thr) & (k0 + lt * LANES + lane <= qpos)
            bias.append(jnp.where(sel, -shift, NEG))
        bias = jnp.concatenate(bias, axis=1)
        bias = jnp.concatenate([bias] * GROUP, axis=0)
        for n in range(N_KV):
            qs = jnp.concatenate(
                [q_ref[:, (n * GROUP + g) * LANES:(n * GROUP + g + 1) * LANES] for g in range(GROUP)], axis=0)
            kt = k_ref[pl.ds(k0, kc), n * LANES:(n + 1) * LANES]
            vt = jnp.concatenate([v_ref[pl.ds(k0, kc), n * LANES:(n + 1) * LANES], ones_col], axis=1)
            s = lax.dot_general(qs, kt, _NT, preferred_element_type=F32)
            p = jnp.exp(s * scale + bias).astype(BF16)
            acc_sc[n] += jnp.dot(p, vt, preferred_element_type=F32)
        return carry

    lax.fori_loop(0, nch, att_chunk, 0)
    for n in range(N_KV):
        a = acc_sc[n]
        o = a[:, :LANES] / a[:, LANES:LANES + 1]
        for g in range(GROUP):
            hh = n * GROUP + g
            o_sc[:, hh * LANES:(hh + 1) * LANES] = o[g * QB:(g + 1) * QB].astype(BF16)

    out_ref[...] = h_ref[...] + jnp.dot(o_sc[...], wout_ref[...], preferred_element_type=F32)


MAX_LOGIT_BOUND = 40.0


def _attn_prompt(h, q, iq, iw, kb, vb, ika, ikb, wout, logit_bound, *, batch, tp, kc, topk):
    args = (q, iq, iw, kb, vb, ika, ikb, wout, h)
    return lax.cond(
        logit_bound <= MAX_LOGIT_BOUND,
        lambda: _attn_prompt_bounded(logit_bound.reshape(1, 1), *args, batch=batch, tp=tp, kc=kc, topk=topk),
        lambda: _attn_prompt_online(*args, batch=batch, tp=tp, kc=kc, topk=topk))


def _attn_prompt_bounded(shift, q, iq, iw, kb, vb, ika, ikb, wout, h, *, batch, tp, kc, topk):
    nqb = tp // QB
    qrow = lambda b, i: (b * nqb + i, 0)
    seq = lambda b, i: (b, 0)
    const = lambda b, i: (0, 0)
    return pl.pallas_call(
        functools.partial(_attn_prompt_bounded_kernel, kc=kc, topk=topk),
        grid=(batch, nqb),
        in_specs=[
            pl.BlockSpec(memory_space=pltpu.SMEM),
            pl.BlockSpec((QB, Q_W), qrow),
            pl.BlockSpec((QB, IQ_W), qrow),
            pl.BlockSpec((QB, IDX_HEADS), qrow),
            pl.BlockSpec((tp, KV_W), seq),
            pl.BlockSpec((tp, KV_W), seq),
            pl.BlockSpec((tp, LANES), seq),
            pl.BlockSpec((tp, LANES), seq),
            pl.BlockSpec((Q_W, D_MODEL), const),
            pl.BlockSpec((QB, D_MODEL), qrow),
        ],
        out_specs=pl.BlockSpec((QB, D_MODEL), qrow),
        out_shape=jax.ShapeDtypeStruct(h.shape, F32),
        scratch_shapes=[
            pltpu.VMEM((QB, tp), I32),
            pltpu.VMEM((QB, IDX_HEADS * LANES), F32),
            pltpu.VMEM((QB, Q_W), BF16),
            pltpu.VMEM((N_KV, GROUP * QB, 2 * HEAD_DIM), F32),
        ],
        compiler_params=pltpu.CompilerParams(
            dimension_semantics=("arbitrary", "arbitrary"), vmem_limit_bytes=VMEM_LIMIT),
        name="attn_prompt_bounded",
    )(shift, q, iq, iw, kb, vb, ika, ikb, wout, h)


def _attn_prompt_online(q, iq, iw, kb, vb, ika, ikb, wout, h, *, batch, tp, kc, topk):
    nqb = tp // QB
    qrow = lambda b, i: (b * nqb + i, 0)
    seq = lambda b, i: (b, 0)
    const = lambda b, i: (0, 0)
    return pl.pallas_call(
        functools.partial(_attn_prompt_kernel, kc=kc, topk=topk),
        grid=(batch, nqb),
        in_specs=[
            pl.BlockSpec((QB, Q_W), qrow),
            pl.BlockSpec((QB, IQ_W), qrow),
            pl.BlockSpec((QB, IDX_HEADS), qrow),
            pl.BlockSpec((tp, KV_W), seq),
            pl.BlockSpec((tp, KV_W), seq),
            pl.BlockSpec((tp, LANES), seq),
            pl.BlockSpec((tp, LANES), seq),
            pl.BlockSpec((Q_W, D_MODEL), const),
            pl.BlockSpec((QB, D_MODEL), qrow),
        ],
        out_specs=pl.BlockSpec((QB, D_MODEL), qrow),
        out_shape=jax.ShapeDtypeStruct(h.shape, F32),
        scratch_shapes=[
            pltpu.VMEM((QB, tp), I32),
            pltpu.VMEM((QB, Q_W), BF16),
            pltpu.VMEM((GROUP * QB, 1), F32),
            pltpu.VMEM((GROUP * QB, 1), F32),
            pltpu.VMEM((GROUP * QB, HEAD_DIM), F32),
        ],
        compiler_params=pltpu.CompilerParams(
            dimension_semantics=("arbitrary", "arbitrary"), vmem_limit_bytes=VMEM_LIMIT),
        name="attn_prompt",
    )(q, iq, iw, kb, vb, ika, ikb, wout, h)


S1_PAGES = 16
S2_PAGES = 8


def _idx_page_scores(iq, iww, page_bf):
    s = lax.dot_general(iq, page_bf, _NT, preferred_element_type=F32)
    sc = jnp.zeros((SUBLANES, PAGE), F32)
    for hh in range(IDX_HEADS):
        sl = slice(hh * SUBLANES, (hh + 1) * SUBLANES)
        sc = sc + iww[sl, :] * jnp.maximum(s[sl, :], 0.0)
    return sc * (IDX_DIM ** -0.5)


def _sample_index_kernel(pt_ref, *refs, past, n_new, topk):
    pages = refs[:S1_PAGES]
    iq_ref, iww_ref, iknew_ref, keys_ref, thr_ref = refs[S1_PAGES:]
    s = pl.program_id(1)
    iq = iq_ref[0]
    iww = iww_ref[0]
    for r in range(S1_PAGES):
        sc = _idx_page_scores(iq, iww, pages[r][...].astype(BF16))
        off = pl.multiple_of((s * S1_PAGES + r) * PAGE, PAGE)
        keys_ref[0, :, pl.ds(off, PAGE)] = _sort_key(sc)

    @pl.when(s == pl.num_programs(1) - 1)
    def _():
        sc = _idx_page_scores(iq, iww, iknew_ref[0])
        t = lax.broadcasted_iota(I32, sc.shape, 0)
        jn = lax.broadcasted_iota(I32, sc.shape, 1)
        sc = jnp.where((jn <= t) & (jn < n_new), sc, NEG)
        keys_ref[0, :, pl.ds(past, PAGE)] = _sort_key(sc)
        keys = keys_ref[0]

        def count_ge(cand):
            return jnp.sum(jnp.where(keys >= cand, 1.0, 0.0), axis=1, keepdims=True)

        thr = _kth_largest_key(count_ge, (SUBLANES, 1), float(topk))
        thr_ref[0] = jnp.broadcast_to(thr, (SUBLANES, LANES))


def _sample_index(page_table, cache_ik, layer, iq, iww, iknew, *, n_new, topk):
    nb, n_pages = page_table.shape
    steps = n_pages // S1_PAGES
    nk = (n_pages + 1) * PAGE
    page_spec = lambda r: pl.BlockSpec(
        (None, None, PAGE, IDX_DIM), lambda b, s, pt: (layer, pt[b, s * S1_PAGES + r], 0, 0))
    per_b = lambda b, s, pt: (b, 0, 0)
    return pl.pallas_call(
        functools.partial(_sample_index_kernel, past=n_pages * PAGE, n_new=n_new, topk=topk),
        grid_spec=pltpu.PrefetchScalarGridSpec(
            num_scalar_prefetch=1,
            grid=(nb, steps),
            in_specs=[page_spec(r) for r in range(S1_PAGES)] + [
                pl.BlockSpec((1, IDX_HEADS * SUBLANES, IDX_DIM), per_b),
                pl.BlockSpec((1, IDX_HEADS * SUBLANES, LANES), per_b),
                pl.BlockSpec((1, PAGE, IDX_DIM), per_b),
            ],
            out_specs=[
                pl.BlockSpec((1, SUBLANES, nk), per_b),
                pl.BlockSpec((1, SUBLANES, LANES), per_b),
            ],
        ),
        out_shape=[jax.ShapeDtypeStruct((nb, SUBLANES, nk), I32),
                   jax.ShapeDtypeStruct((nb, SUBLANES, LANES), I32)],
        compiler_params=pltpu.CompilerParams(
            dimension_semantics=("arbitrary", "arbitrary"), vmem_limit_bytes=VMEM_LIMIT),
        name="sample_index",
    )(page_table, *([cache_ik] * S1_PAGES), iq, iww, iknew)


def _sample_attend_kernel(pt_ref, *refs, n_new):
    kpages = refs[:S2_PAGES]
    vpages = refs[S2_PAGES:2 * S2_PAGES]
    (qbd_ref, keys_ref, keysnew_ref, thr_ref, knew_ref, vnew_ref, o_ref,
     m_sc, l_sc, acc_sc) = refs[2 * S2_PAGES:]
    s = pl.program_id(1)
    scale = HEAD_DIM ** -0.5
    qbd = qbd_ref[0]
    thr = thr_ref[0]

    @pl.when(s == 0)
    def _():
        m_sc[...] = jnp.full(m_sc.shape, NEG, F32)
        l_sc[...] = jnp.zeros(l_sc.shape, F32)
        acc_sc[...] = jnp.zeros(acc_sc.shape, F32)

    def page_update(kp, vp, sel):
        bias = jnp.where(sel, 0.0, NEG)
        bias = jnp.concatenate([bias] * (N_HEADS), axis=0)
        sc = lax.dot_general(qbd, kp, _NT, preferred_element_type=F32) * scale + bias
        m_old = m_sc[...]
        m_new = jnp.maximum(m_old, jnp.max(sc, axis=1, keepdims=True))
        alpha = jnp.exp(m_old - m_new)
        p = jnp.exp(sc - m_new)
        l_sc[...] = alpha * l_sc[...] + jnp.sum(p, axis=1, keepdims=True)
        acc_sc[...] = alpha * acc_sc[...] + jnp.dot(p.astype(BF16), vp, preferred_element_type=F32)
        m_sc[...] = m_new

    def page_rows(ref):
        return jnp.concatenate(
            [ref[pl.ds(n, PAGE, stride=N_KV), :] for n in range(N_KV)], axis=1).astype(BF16)

    for r in range(S2_PAGES):
        kk = keys_ref[0, :, r * PAGE:(r + 1) * PAGE]
        page_update(page_rows(kpages[r]), page_rows(vpages[r]), kk >= thr)

    @pl.when(s == pl.num_programs(1) - 1)
    def _():
        kk = keysnew_ref[0]
        t = lax.broadcasted_iota(I32, kk.shape, 0)
        jn = lax.broadcasted_iota(I32, kk.shape, 1)
        page_update(knew_ref[0], vnew_ref[0], (kk >= thr) & (jn <= t) & (jn < n_new))
        o_ref[0] = acc_sc[...] / l_sc[...]


def _sample_attend(page_table, cache_k, cache_v, layer, qbd, keys, thr, knew, vnew, *, n_new):
    nb, n_pages = page_table.shape
    steps = n_pages // S2_PAGES
    page_spec = lambda r: pl.BlockSpec(
        (None, None, PAGE * N_KV, HEAD_DIM), lambda b, s, pt: (layer, pt[b, s * S2_PAGES + r], 0, 0))
    per_b = lambda b, s, pt: (b, 0, 0)
    rows = N_HEADS * SUBLANES
    return pl.pallas_call(
        functools.partial(_sample_attend_kernel, n_new=n_new),
        grid_spec=pltpu.PrefetchScalarGridSpec(
            num_scalar_prefetch=1,
            grid=(nb, steps),
            in_specs=[page_spec(r) for r in range(S2_PAGES)] * 2 + [
                pl.BlockSpec((1, rows, KV_W), per_b),
                pl.BlockSpec((1, SUBLANES, S2_PAGES * PAGE), lambda b, s, pt: (b, 0, s)),
                pl.BlockSpec((1, SUBLANES, PAGE), lambda b, s, pt: (b, 0, n_pages)),
                pl.BlockSpec((1, SUBLANES, LANES), per_b),
                pl.BlockSpec((1, PAGE, KV_W), per_b),
                pl.BlockSpec((1, PAGE, KV_W), per_b),
            ],
            out_specs=pl.BlockSpec((1, rows, KV_W), per_b),
            scratch_shapes=[
                pltpu.VMEM((rows, 1), F32),
                pltpu.VMEM((rows, 1), F32),
                pltpu.VMEM((rows, KV_W), F32),
            ],
        ),
        out_shape=jax.ShapeDtypeStruct((nb, rows, KV_W), F32),
        compiler_params=pltpu.CompilerParams(
            dimension_semantics=("arbitrary", "arbitrary"), vmem_limit_bytes=VMEM_LIMIT),
        name="sample_attend",
    )(page_table, *([cache_k] * S2_PAGES), *([cache_v] * S2_PAGES), qbd, keys, keys, thr, knew, vnew)


def _matmul_res_kernel(h_ref, x_ref, w_ref, o_ref):
    o_ref[...] = h_ref[...] + jnp.dot(x_ref[...], w_ref[...], preferred_element_type=F32)


def _matmul_res(h, x, w, *, tn):
    m, n = h.shape
    kdim = x.shape[1]
    return pl.pallas_call(
        _matmul_res_kernel,
        grid=(n // tn,),
        in_specs=[pl.BlockSpec((m, tn), lambda j: (0, j)),
                  pl.BlockSpec((m, kdim), lambda j: (0, 0)),
                  pl.BlockSpec((kdim, tn), lambda j: (0, j))],
        out_specs=pl.BlockSpec((m, tn), lambda j: (0, j)),
        out_shape=jax.ShapeDtypeStruct((m, n), F32),
        compiler_params=pltpu.CompilerParams(dimension_semantics=("arbitrary",)),
        name="matmul_residual",
    )(h, x, w)


def _conv_rows(ext_sc, u, init_ref, carry_sc, c, first, cw_ref, cb_ref, tail_ref, *,
               width, shift, d0, tail_end):
    tm = u.shape[0]
    ext_sc[d0:d0 + tm, :] = u

    @pl.when(first)
    def _():
        ext_sc[0:d0, :] = init_ref[0]

    @pl.when(jnp.logical_not(first))
    def _():
        ext_sc[0:d0, :] = carry_sc[c]

    y = cb_ref[...] + cw_ref[width - 1:width, :] * u
    for j in range(1, width):
        y = y + cw_ref[width - 1 - j:width - j, :] * ext_sc[d0 - j * shift:d0 - j * shift + tm, :]
    carry_sc[c] = ext_sc[tm:tm + d0, :]
    tail_ref[0] = ext_sc[tail_end:tail_end + d0, :]
    return y


def _ffn_kernel(h_ref, g_ref, wu_ref, wg_ref, cw_ref, cb_ref, wd_ref, st_ref, out_ref, tail_ref,
                xn_sc, acc_sc, ext_sc, carry_sc, *, shift, d0, tiles_per_seq, tail_end):
    i = pl.program_id(0)
    c = pl.program_id(1)

    @pl.when(c == 0)
    def _():
        xn_sc[...] = _rms_rows(h_ref[...], g_ref[...]).astype(BF16)
        acc_sc[...] = jnp.zeros(acc_sc.shape, F32)

    xn = xn_sc[...]
    u = jnp.dot(xn, wu_ref[...], preferred_element_type=F32)
    conv = _conv_rows(ext_sc, u, st_ref, carry_sc, c, (i % tiles_per_seq) == 0, cw_ref, cb_ref,
                      tail_ref, width=FFN_CONV, shift=shift, d0=d0, tail_end=tail_end)
    mid = _gelu(conv) * jnp.dot(xn, wg_ref[...], preferred_element_type=F32)
    acc_sc[...] += jnp.dot(mid.astype(BF16), wd_ref[...], preferred_element_type=F32)

    @pl.when(c == pl.num_programs(1) - 1)
    def _():
        out_ref[...] = h_ref[...] + acc_sc[...]


def _conv_ffn(h, g, wu, wg, cw, cb, wd, state, *, tm, tf, shift, tiles_per_seq, tail_end):
    m = h.shape[0]
    d0 = state.shape[1]
    n_tiles, n_chunks = m // tm, D_FF // tf
    row = lambda i, c: (i, 0)
    return pl.pallas_call(
        functools.partial(_ffn_kernel, shift=shift, d0=d0, tiles_per_seq=tiles_per_seq, tail_end=tail_end),
        grid=(n_tiles, n_chunks),
        in_specs=[
            pl.BlockSpec((tm, D_MODEL), row),
            pl.BlockSpec((1, D_MODEL), lambda i, c: (0, 0)),
            pl.BlockSpec((D_MODEL, tf), lambda i, c: (0, c)),
            pl.BlockSpec((D_MODEL, tf), lambda i, c: (0, c)),
            pl.BlockSpec((FFN_CONV, tf), lambda i, c: (0, c)),
            pl.BlockSpec((1, tf), lambda i, c: (0, c)),
            pl.BlockSpec((tf, D_MODEL), lambda i, c: (c, 0)),
            pl.BlockSpec((1, d0, tf), lambda i, c: (i // tiles_per_seq, 0, c)),
        ],
        out_specs=[
            pl.BlockSpec((tm, D_MODEL), row),
            pl.BlockSpec((1, d0, tf), lambda i, c: (i, 0, c)),
        ],
        out_shape=[jax.ShapeDtypeStruct((m, D_MODEL), F32),
                   jax.ShapeDtypeStruct((n_tiles, d0, D_FF), F32)],
        scratch_shapes=[
            pltpu.VMEM((tm, D_MODEL), BF16),
            pltpu.VMEM((tm, D_MODEL), F32),
            pltpu.VMEM((d0 + tm, tf), F32),
            pltpu.VMEM((n_chunks, d0, tf), F32),
        ],
        compiler_params=pltpu.CompilerParams(
            dimension_semantics=("arbitrary", "arbitrary"), vmem_limit_bytes=VMEM_LIMIT),
        name="conv_ffn",
    )(h, g, wu, wg, cw, cb, wd, state)


def _scan_rows(a_sc, b_sc, hc, *, shift):
    tm = a_sc.shape[0]
    ngroups = tm // SUBLANES
    row = lax.broadcasted_iota(I32, (SUBLANES, a_sc.shape[1]), 0)

    def body(gi, hc):
        r0 = pl.multiple_of(gi * SUBLANES, SUBLANES)
        a = a_sc[pl.ds(r0, SUBLANES), :]
        b = b_sc[pl.ds(r0, SUBLANES), :]
        if shift == 1:
            for s in (1, 2, 4):
                keep = row >= s
                a_sh = jnp.where(keep, pltpu.roll(a, s, 0), 1.0)
                b_sh = jnp.where(keep, pltpu.roll(b, s, 0), 0.0)
                b = a * b_sh + b
                a = a * a_sh
        hrows = a * hc + b
        b_sc[pl.ds(r0, SUBLANES), :] = hrows
        if shift == 1:
            return jnp.broadcast_to(hrows[SUBLANES - 1:SUBLANES, :], hrows.shape)
        return hrows

    return lax.fori_loop(0, ngroups, body, hc)


def _lru_kernel(h_ref, g_ref, wx_ref, wy_ref, cw_ref, cb_ref, wa_ref, ba_ref, wi_ref, bi_ref,
                lam_ref, wo_ref, h0_ref, st_ref, out_ref, htail_ref, utail_ref,
                xn_sc, acc_sc, ext_sc, a_sc, b_sc, carry_u, carry_h, *,
                shift, d0, tiles_per_seq, tail_end):
    i = pl.program_id(0)
    c = pl.program_id(1)
    first = (i % tiles_per_seq) == 0

    @pl.when(c == 0)
    def _():
        xn_sc[...] = _rms_rows(h_ref[...], g_ref[...]).astype(BF16)
        acc_sc[...] = jnp.zeros(acc_sc.shape, F32)

    xn = xn_sc[...]
    u = jnp.dot(xn, wx_ref[...], preferred_element_type=F32)
    conv = _conv_rows(ext_sc, u, st_ref, carry_u, c, first, cw_ref, cb_ref, utail_ref,
                      width=LRU_CONV, shift=shift, d0=d0, tail_end=tail_end)
    cbf = conv.astype(BF16)
    r = _sigmoid(jnp.dot(cbf, wa_ref[0], preferred_element_type=F32) + ba_ref[...])
    ig = _sigmoid(jnp.dot(cbf, wi_ref[0], preferred_element_type=F32) + bi_ref[...])
    lam = lam_ref[...]
    log_sig = jnp.minimum(lam, 0.0) - jnp.log(1.0 + jnp.exp(-jnp.abs(lam)))
    log_a = LRU_C * r * log_sig
    a = jnp.exp(log_a)
    b = jnp.sqrt(-jnp.tanh(log_a) * (a * a + 1.0)) * (ig * conv)
    a_sc[...] = a
    b_sc[...] = b

    @pl.when(first)
    def _():
        carry_h[c] = h0_ref[0]

    carry_h[c] = _scan_rows(a_sc, b_sc, carry_h[c], shift=shift)
    htail_ref[0] = b_sc[tail_end - SUBLANES:tail_end, :]
    gate = _gelu(jnp.dot(xn, wy_ref[...], preferred_element_type=F32))
    acc_sc[...] += jnp.dot((b_sc[...] * gate).astype(BF16), wo_ref[...], preferred_element_type=F32)

    @pl.when(c == pl.num_programs(1) - 1)
    def _():
        out_ref[...] = h_ref[...] + acc_sc[...]


def _lru_block(h, g, wx, wy, cw, cb, wa, ba, wi, bi, lam, wo, h0, state, *,
               tm, shift, tiles_per_seq, tail_end):
    m = h.shape[0]
    d0 = state.shape[1]
    n_tiles = m // tm
    bd = RNN_BD
    row = lambda i, c: (i, 0)
    col = lambda i, c: (0, c)
    return pl.pallas_call(
        functools.partial(_lru_kernel, shift=shift, d0=d0, tiles_per_seq=tiles_per_seq, tail_end=tail_end),
        grid=(n_tiles, RNN_BLOCKS),
        in_specs=[
            pl.BlockSpec((tm, D_MODEL), row),
            pl.BlockSpec((1, D_MODEL), lambda i, c: (0, 0)),
            pl.BlockSpec((D_MODEL, bd), col),
            pl.BlockSpec((D_MODEL, bd), col),
            pl.BlockSpec((LRU_CONV, bd), col),
            pl.BlockSpec((1, bd), col),
            pl.BlockSpec((1, bd, bd), lambda i, c: (c, 0, 0)),
            pl.BlockSpec((1, bd), col),
            pl.BlockSpec((1, bd, bd), lambda i, c: (c, 0, 0)),
            pl.BlockSpec((1, bd), col),
            pl.BlockSpec((1, bd), col),
            pl.BlockSpec((bd, D_MODEL), lambda i, c: (c, 0)),
            pl.BlockSpec((1, SUBLANES, bd), lambda i, c: (i // tiles_per_seq, 0, c)),
            pl.BlockSpec((1, d0, bd), lambda i, c: (i // tiles_per_seq, 0, c)),
        ],
        out_specs=[
            pl.BlockSpec((tm, D_MODEL), row),
            pl.BlockSpec((1, SUBLANES, bd), lambda i, c: (i, 0, c)),
            pl.BlockSpec((1, d0, bd), lambda i, c: (i, 0, c)),
        ],
        out_shape=[jax.ShapeDtypeStruct((m, D_MODEL), F32),
                   jax.ShapeDtypeStruct((n_tiles, SUBLANES, D_MODEL), F32),
                   jax.ShapeDtypeStruct((n_tiles, d0, D_MODEL), F32)],
        scratch_shapes=[
            pltpu.VMEM((tm, D_MODEL), BF16),
            pltpu.VMEM((tm, D_MODEL), F32),
            pltpu.VMEM((d0 + tm, bd), F32),
            pltpu.VMEM((tm, bd), F32),
            pltpu.VMEM((tm, bd), F32),
            pltpu.VMEM((RNN_BLOCKS, d0, bd), F32),
            pltpu.VMEM((RNN_BLOCKS, SUBLANES, bd), F32),
        ],
        compiler_params=pltpu.CompilerParams(
            dimension_semantics=("arbitrary", "arbitrary"), vmem_limit_bytes=VMEM_LIMIT),
        name="rg_lru",
    )(h, g, wx, wy, cw, cb, wa, ba, wi, bi, lam, wo, h0, state)


def _rope_tables(pos, rot, period):
    half = rot // 2
    inv = 1.0 / (ROPE_THETA ** (jnp.arange(half, dtype=F32) * (2.0 / rot)))
    ang = pos.astype(F32)[:, None] * inv[None, :]
    lane = jnp.arange(LANES) % period
    cos = jnp.cos(ang)[:, lane % half]
    sin = jnp.sin(ang)[:, lane % half]
    return jnp.stack([
        jnp.where(lane < rot, cos, 1.0),
        jnp.where(lane < half, -sin, 0.0),
        jnp.where((lane >= half) & (lane < rot), sin, 0.0),
    ])


def _time_major(x):
    return jnp.swapaxes(x, 0, 1)


def kernel(x_prompt, x_sample, cache_k, cache_v, cache_idx_k, state_lru_h, state_lru_conv, state_ffn_conv, page_table, meta_tokens, norm1_g, norm2_g, attn_w_in, attn_q_norm_g, attn_k_norm_g, attn_idx_k_norm_g, attn_w_out, lru_w_x, lru_w_y, lru_conv_w, lru_conv_b, lru_wa, lru_ba, lru_wi, lru_bi, lru_lambda, lru_w_out, ffn_w_up, ffn_w_gate, ffn_dw_w, ffn_dw_b, ffn_w_down):
    batch, seq, _ = x_prompt.shape
    nb, ns, _ = x_sample.shape
    depth = norm1_g.shape[0]
    t_real = seq + N_META
    tp = _round_up(t_real, QB)
    n_pages = page_table.shape[1]
    past = n_pages * PAGE
    topk_p = min(TOPK_MAX, t_real // 4)
    topk_s = min(TOPK_MAX, (past + ns) // 4)
    assert nb == SUBLANES and ns <= SUBLANES and n_pages % S1_PAGES == 0 and n_pages % S2_PAGES == 0

    tiles_per_seq = 6
    tm = tp // tiles_per_seq
    assert tm * tiles_per_seq == tp and tm % (2 * SUBLANES) == 0
    tail_end = t_real - (tiles_per_seq - 1) * tm
    assert tail_end % SUBLANES == 0 and tail_end >= SUBLANES
    kc = 3 * LANES
    assert tp % kc == 0 and kc >= topk_p
    ms = nb * ns

    hp = jnp.concatenate([
        jnp.broadcast_to(meta_tokens[None], (batch, N_META, D_MODEL)), x_prompt,
        jnp.zeros((batch, tp - t_real, D_MODEL), F32)], axis=1).reshape(batch * tp, D_MODEL)
    hs = _time_major(x_sample).reshape(ms, D_MODEL)

    rq_p = _rope_tables(jnp.arange(tp), ROT, HEAD_DIM)
    ri_p = _rope_tables(jnp.arange(tp), IDX_ROT, IDX_DIM)
    pos_s = jnp.repeat(past + jnp.arange(ns), nb)
    rq_s = _rope_tables(pos_s, ROT, HEAD_DIM)
    ri_s = _rope_tables(pos_s, IDX_ROT, IDX_DIM)

    cache_k4 = cache_k.reshape(cache_k.shape[:2] + (PAGE * N_KV, HEAD_DIM))
    cache_v4 = cache_v.reshape(cache_v.shape[:2] + (PAGE * N_KV, HEAD_DIM))

    kp, vp, ikp, ksl, vsl, iksl = [], [], [], [], [], []
    hpl, cpl, hsl, csl, fpl, fsl = [], [], [], [], [], []
    last_tiles = jnp.arange(batch) * tiles_per_seq + tiles_per_seq - 1

    for layer in range(depth):
        mi = layer // 2
        g1 = norm1_g[layer][None]
        if layer % 2 == 0:
            w_in = jnp.pad(attn_w_in[mi], ((0, 0), (0, PROJ_NB * PROJ_TN - PROJ_W))).astype(BF16)
            w_out = attn_w_out[mi].astype(BF16)
            qg = attn_q_norm_g[mi][None]
            kg = attn_k_norm_g[mi][None]
            ikg = jnp.pad(attn_idx_k_norm_g[mi], (0, LANES - IDX_DIM))[None]

            q, kf, kb, vf, vb, iq, ikf, ika, ikb, iw = _attn_project(
                hp, g1, w_in, qg, kg, ikg, rq_p, ri_p, tm=tm)
            logit_bound = (1.02 * HEAD_DIM ** 0.5) * jnp.max(jnp.abs(qg)) * jnp.max(jnp.abs(kg))
            hp = _attn_prompt(hp, q, iq, iw, kb, vb, ika, ikb, w_out, logit_bound,
                              batch=batch, tp=tp, kc=kc, topk=topk_p)
            kp.append(kf.reshape(batch, tp, N_KV, HEAD_DIM)[:, :t_real])
            vp.append(vf.reshape(batch, tp, N_KV, HEAD_DIM)[:, :t_real])
            ikp.append(ikf.reshape(batch, tp, IDX_DIM)[:, :t_real])

            q, kf, kb, vf, vb, iq, ikf, ika, ikb, iw = _attn_project(
                hs, g1, w_in, qg, kg, ikg, rq_s, ri_s, tm=ms)
            ksl.append(_time_major(kf.reshape(ns, nb, N_KV, HEAD_DIM)))
            vsl.append(_time_major(vf.reshape(ns, nb, N_KV, HEAD_DIM)))
            iksl.append(_time_major(ikf.reshape(ns, nb, IDX_DIM)))

            pad_q = lambda x: jnp.pad(x, [(0, 0)] * (x.ndim - 2) + [(0, SUBLANES - ns), (0, 0)])
            iq_b = pad_q(iq.reshape(ns, nb, IDX_HEADS, IDX_DIM).transpose(1, 2, 0, 3))
            iq_b = iq_b.reshape(nb, IDX_HEADS * SUBLANES, IDX_DIM)
            iw_b = jnp.pad(iw.reshape(ns, nb, IDX_HEADS).transpose(1, 2, 0), ((0, 0), (0, 0), (0, SUBLANES - ns)))
            iww = jnp.broadcast_to(iw_b.reshape(nb, IDX_HEADS * SUBLANES, 1), (nb, IDX_HEADS * SUBLANES, LANES))
            new_page = lambda x: jnp.pad(_time_major(x.reshape(ns, nb, -1)), ((0, 0), (0, PAGE - ns), (0, 0)))
            iknew = new_page(ika[:, :IDX_DIM])
            keys, thr = _sample_index(page_table, cache_idx_k, mi, iq_b, iww, iknew, n_new=ns, topk=topk_s)

            q_b = pad_q(q.reshape(ns, nb, N_KV, GROUP, HEAD_DIM).transpose(1, 2, 3, 0, 4))
            qbd = q_b[:, :, :, :, None, :] * jnp.eye(N_KV, dtype=BF16)[None, :, None, None, :, None]
            qbd = qbd.reshape(nb, N_HEADS * SUBLANES, KV_W)
            o = _sample_attend(page_table, cache_k4, cache_v4, mi, qbd, keys, thr,
                               new_page(kb), new_page(vb), n_new=ns)
            o = o.reshape(nb, N_KV, GROUP, SUBLANES, N_KV, HEAD_DIM)
            o = jnp.stack([o[:, n, :, :ns, n, :] for n in range(N_KV)], axis=1)
            o = o.transpose(3, 0, 1, 2, 4).reshape(ms, Q_W).astype(BF16)
            hs = _matmul_res(hs, o, w_out, tn=512)
        else:
            lw = dict(
                g=g1, wx=lru_w_x[mi].astype(BF16), wy=lru_w_y[mi].astype(BF16),
                cw=lru_conv_w[mi], cb=lru_conv_b[mi][None],
                wa=lru_wa[mi].astype(BF16), ba=lru_ba[mi][None],
                wi=lru_wi[mi].astype(BF16), bi=lru_bi[mi][None],
                lam=lru_lambda[mi][None], wo=lru_w_out[mi].astype(BF16))
            hp, htail, utail = _lru_block(
                hp, h0=jnp.zeros((batch, SUBLANES, D_MODEL), F32),
                state=jnp.zeros((batch, SUBLANES, D_MODEL), F32),
                tm=tm, shift=1, tiles_per_seq=tiles_per_seq, tail_end=tail_end, **lw)
            hpl.append(htail[last_tiles, SUBLANES - 1])
            cpl.append(utail[last_tiles, SUBLANES - (LRU_CONV - 1):])
            d0s = (LRU_CONV - 1) * nb
            hs, htail, utail = _lru_block(
                hs, h0=state_lru_h[mi][None],
                state=_time_major(state_lru_conv[mi]).reshape(1, d0s, D_MODEL),
                tm=ms, shift=nb, tiles_per_seq=1, tail_end=ms, **lw)
            hsl.append(htail[0])
            csl.append(_time_major(utail.reshape(LRU_CONV - 1, nb, D_MODEL)))

        fw = dict(g=norm2_g[layer][None], wu=ffn_w_up[layer].astype(BF16), wg=ffn_w_gate[layer].astype(BF16),
                  cw=ffn_dw_w[layer], cb=ffn_dw_b[layer][None], wd=ffn_w_down[layer].astype(BF16))
        hp, tail = _conv_ffn(hp, state=jnp.zeros((batch, SUBLANES, D_FF), F32),
                             tm=tm, tf=512, shift=1, tiles_per_seq=tiles_per_seq, tail_end=tail_end, **fw)
        fpl.append(tail[last_tiles, SUBLANES - (FFN_CONV - 1):])
        d0s = (FFN_CONV - 1) * nb
        hs, tail = _conv_ffn(hs, state=_time_major(state_ffn_conv[layer]).reshape(1, d0s, D_FF),
                             tm=ms, tf=512, shift=nb, tiles_per_seq=1, tail_end=ms, **fw)
        fsl.append(_time_major(tail.reshape(FFN_CONV - 1, nb, D_FF)))

    y_prompt = hp.reshape(batch, tp, D_MODEL)[:, N_META:t_real]
    y_sample = _time_major(hs.reshape(ns, nb, D_MODEL))
    return (y_prompt, y_sample, jnp.stack(kp), jnp.stack(vp), jnp.stack(ikp),
            jnp.stack(ksl), jnp.stack(vsl), jnp.stack(iksl),
            jnp.stack(hpl), jnp.stack(cpl), jnp.stack(hsl), jnp.stack(csl),
            jnp.stack(fpl), jnp.stack(fsl))
```

```python
import functools

import jax
import jax.numpy as jnp
from jax import lax
from jax.experimental import pallas as pl
from jax.experimental.pallas import tpu as pltpu

F32 = jnp.float32
BF16 = jnp.bfloat16
I32 = jnp.int32

D_MODEL = 2048
N_META = 16
N_HEADS = 16
HEAD_DIM = 128
N_KV = 4
GROUP = N_HEADS // N_KV
ROT = HEAD_DIM // 4
IDX_HEADS = 16
IDX_DIM = 64
IDX_ROT = IDX_DIM // 4
ROPE_THETA = 500000.0
TOPK_MAX = 256
PAGE = 128
Q_W = N_HEADS * HEAD_DIM
KV_W = N_KV * HEAD_DIM
IQ_W = IDX_HEADS * IDX_DIM
PROJ_W = Q_W + 2 * KV_W + IQ_W + IDX_DIM + IDX_HEADS
D_FF = 3 * D_MODEL
FFN_CONV = 3
LRU_CONV = 4
LRU_C = 8.0
RNN_BLOCKS = 8
RNN_BD = D_MODEL // RNN_BLOCKS
EPS = 1e-6
NEG = -1e30

LANES = 128
SUBLANES = 8
QB = 128
MXU_COLS = 256
PROJ_WP = -(-PROJ_W // LANES) * LANES
VMEM_LIMIT = 56 * 1024 * 1024

_NT = (((1,), (1,)), ((), ()))
_INT_MIN = -2 ** 31


def _round_up(x, m):
    return -(-x // m) * m


def _rms_rows(x, g):
    ms = jnp.mean(x * x, axis=-1, keepdims=True)
    return x * lax.rsqrt(ms + EPS) * g


def _gelu(x):
    return 0.5 * x * (1.0 + jnp.tanh(0.7978845608028654 * (x + 0.044715 * (x * x * x))))


def _sigmoid(x):
    return 0.5 * jnp.tanh(0.5 * x) + 0.5


def _sort_key(x):
    b = lax.bitcast_convert_type(x + 0.0, I32)
    return b ^ ((b >> 31) & 0x7FFFFFFF)


def _kth_largest_key(count_ge, shape, k):
    def bit_body(bi, t):
        cand = t | jnp.left_shift(jnp.int32(1), 31 - bi)
        cnt = count_ge(cand ^ _INT_MIN)
        return jnp.where(cnt >= k, cand, t)
    t = lax.fori_loop(0, 32, bit_body, jnp.zeros(shape, I32))
    return t ^ _INT_MIN


def _rope(x, tab_ref, sh):
    return (x * tab_ref[0] + pltpu.roll(x, LANES - sh, 1) * tab_ref[1]
            + pltpu.roll(x, sh, 1) * tab_ref[2])


def _proj_kernel(h_ref, g_ref, w_ref, qg_ref, kg_ref, ikg_ref, rq_ref, ri_ref,
                 q_ref, kf_ref, kb_ref, vf_ref, vb_ref, iq_ref, ikf_ref, ika_ref, ikb_ref, iw_ref):
    xn = _rms_rows(h_ref[...], g_ref[...]).astype(BF16)

    def zcols(lo, width=MXU_COLS):
        return jnp.dot(xn, w_ref[:, lo:lo + width], preferred_element_type=F32)

    halves = [slice(s * LANES, (s + 1) * LANES) for s in range(MXU_COLS // LANES)]
    for j in range(Q_W // MXU_COLS):
        z = zcols(j * MXU_COLS)
        for sl in halves:
            x = _rope(_rms_rows(z[:, sl], qg_ref[...]), rq_ref, ROT // 2)
            q_ref[:, j * MXU_COLS + sl.start:j * MXU_COLS + sl.stop] = x.astype(BF16)
    for j in range(KV_W // MXU_COLS):
        z = zcols(Q_W + j * MXU_COLS)
        for sl in halves:
            x = _rope(_rms_rows(z[:, sl], kg_ref[...]), rq_ref, ROT // 2)
            kf_ref[:, j * MXU_COLS + sl.start:j * MXU_COLS + sl.stop] = x
            kb_ref[:, j * MXU_COLS + sl.start:j * MXU_COLS + sl.stop] = x.astype(BF16)
        z = zcols(Q_W + KV_W + j * MXU_COLS)
        vf_ref[:, j * MXU_COLS:(j + 1) * MXU_COLS] = z
        vb_ref[:, j * MXU_COLS:(j + 1) * MXU_COLS] = z.astype(BF16)
    for j in range(IQ_W // MXU_COLS):
        z = zcols(Q_W + 2 * KV_W + j * MXU_COLS)
        for sl in halves:
            x = _rope(z[:, sl], ri_ref, IDX_ROT // 2)
            iq_ref[:, j * MXU_COLS + sl.start:j * MXU_COLS + sl.stop] = x.astype(BF16)

    x = zcols(Q_W + 2 * KV_W + IQ_W, LANES)
    lane = lax.broadcasted_iota(I32, x.shape, 1)
    ms = jnp.sum(jnp.where(lane < IDX_DIM, x * x, 0.0), axis=-1, keepdims=True) * (1.0 / IDX_DIM)
    y = _rope(x * lax.rsqrt(ms + EPS) * ikg_ref[...], ri_ref, IDX_ROT // 2)
    ikf_ref[...] = y[:, :IDX_DIM]
    ika_ref[...] = y.astype(BF16)
    ikb_ref[...] = pltpu.roll(y, IDX_DIM, 1).astype(BF16)
    iw_ref[...] = jnp.where(lane < IDX_HEADS, pltpu.roll(x, LANES - IDX_DIM, 1) * (IDX_HEADS ** -0.5), 0.0)


def _attn_project(h, g, w_pad, qg, kg, ikg, rq, ri, *, tm):
    m = h.shape[0]
    tab_tiles = rq.shape[1] // tm
    row = lambda i: (i, 0)
    const = lambda i: (0, 0)
    tab = lambda i: (0, i % tab_tiles, 0)
    sds = jax.ShapeDtypeStruct
    return pl.pallas_call(
        _proj_kernel,
        grid=(m // tm,),
        in_specs=[
            pl.BlockSpec((tm, D_MODEL), row),
            pl.BlockSpec((1, D_MODEL), const),
            pl.BlockSpec(w_pad.shape, const, pipeline_mode=pl.Buffered(1)),
            pl.BlockSpec((1, LANES), const),
            pl.BlockSpec((1, LANES), const),
            pl.BlockSpec((1, LANES), const),
            pl.BlockSpec((3, tm, LANES), tab),
            pl.BlockSpec((3, tm, LANES), tab),
        ],
        out_specs=[
            pl.BlockSpec((tm, Q_W), row),
            pl.BlockSpec((tm, KV_W), row),
            pl.BlockSpec((tm, KV_W), row),
            pl.BlockSpec((tm, KV_W), row),
            pl.BlockSpec((tm, KV_W), row),
            pl.BlockSpec((tm, IQ_W), row),
            pl.BlockSpec((tm, IDX_DIM), row),
            pl.BlockSpec((tm, LANES), row),
            pl.BlockSpec((tm, LANES), row),
            pl.BlockSpec((tm, LANES), row),
        ],
        out_shape=[
            sds((m, Q_W), BF16), sds((m, KV_W), F32), sds((m, KV_W), BF16),
            sds((m, KV_W), F32), sds((m, KV_W), BF16), sds((m, IQ_W), BF16),
            sds((m, IDX_DIM), F32), sds((m, LANES), BF16), sds((m, LANES), BF16),
            sds((m, LANES), F32),
        ],
        compiler_params=pltpu.CompilerParams(
            dimension_semantics=("arbitrary",), vmem_limit_bytes=VMEM_LIMIT),
        name="attn_project",
    )(h, g, w_pad, qg, kg, ikg, rq, ri)


def _attn_prompt_kernel(q_ref, iq_ref, iw_ref, k_ref, v_ref, ika_ref, ikb_ref, wout_ref, h_ref,
                        out_ref, keys_sc, o_sc, m_sc, l_sc, acc_sc, *, kc, topk):
    i = pl.program_id(1)
    q0 = i * QB
    nch = (q0 + QB + kc - 1) // kc
    qpos = q0 + lax.broadcasted_iota(I32, (QB, 1), 0)
    lane = lax.broadcasted_iota(I32, (QB, kc), 1)
    iw = iw_ref[...]

    def score_chunk(c, carry):
        k0 = pl.multiple_of(c * kc, LANES)
        ka = ika_ref[pl.ds(k0, kc), :]
        kb = ikb_ref[pl.ds(k0, kc), :]
        sc = jnp.zeros((QB, kc), F32)
        for p in range(IDX_HEADS // 2):
            a = iq_ref[:, p * LANES:(p + 1) * LANES]
            sa = lax.dot_general(a, ka, _NT, preferred_element_type=F32)
            sb = lax.dot_general(a, kb, _NT, preferred_element_type=F32)
            sc = sc + iw[:, 2 * p:2 * p + 1] * jnp.maximum(sa, 0.0)
            sc = sc + iw[:, 2 * p + 1:2 * p + 2] * jnp.maximum(sb, 0.0)
        sc = sc * (IDX_DIM ** -0.5)
        sc = jnp.where(k0 + lane <= qpos, sc, NEG)
        keys_sc[:, pl.ds(k0, kc)] = _sort_key(sc)
        return carry

    lax.fori_loop(0, nch, score_chunk, 0)

    def count_ge(cand):
        def body(c, acc):
            k0 = pl.multiple_of(c * kc, LANES)
            kk = keys_sc[:, pl.ds(k0, kc)]
            return acc + jnp.sum(jnp.where(kk >= cand, 1.0, 0.0), axis=1, keepdims=True)
        return lax.fori_loop(0, nch, body, jnp.zeros((QB, 1), F32))

    thr = _kth_largest_key(count_ge, (QB, 1), float(topk))

    scale = HEAD_DIM ** -0.5
    for n in range(N_KV):
        qs = jnp.concatenate(
            [q_ref[:, (n * GROUP + g) * LANES:(n * GROUP + g + 1) * LANES] for g in range(GROUP)], axis=0)
        m_sc[...] = jnp.full(m_sc.shape, NEG, F32)
        l_sc[...] = jnp.zeros(l_sc.shape, F32)
        acc_sc[...] = jnp.zeros(acc_sc.shape, F32)

        def att_chunk(c, carry, n=n, qs=qs):
            k0 = pl.multiple_of(c * kc, LANES)
            kk = keys_sc[:, pl.ds(k0, kc)]
            sel = (kk >= thr) & (k0 + lane <= qpos)
            bias = jnp.where(sel, 0.0, NEG)
            bias = jnp.concatenate([bias] * GROUP, axis=0)
            kt = k_ref[pl.ds(k0, kc), n * LANES:(n + 1) * LANES]
            vt = v_ref[pl.ds(k0, kc), n * LANES:(n + 1) * LANES]
            s = lax.dot_general(qs, kt, _NT, preferred_element_type=F32) * scale + bias
            m_old = m_sc[...]
            m_new = jnp.maximum(m_old, jnp.max(s, axis=1, keepdims=True))
            alpha = jnp.exp(m_old - m_new)
            p = jnp.exp(s - m_new)
            l_sc[...] = alpha * l_sc[...] + jnp.sum(p, axis=1, keepdims=True)
            acc_sc[...] = alpha * acc_sc[...] + jnp.dot(p.astype(BF16), vt, preferred_element_type=F32)
            m_sc[...] = m_new
            return carry

        lax.fori_loop(0, nch, att_chunk, 0)
        o = acc_sc[...] / l_sc[...]
        for g in range(GROUP):
            hh = n * GROUP + g
            o_sc[:, hh * LANES:(hh + 1) * LANES] = o[g * QB:(g + 1) * QB].astype(BF16)

    out_ref[...] = h_ref[...] + jnp.dot(o_sc[...], wout_ref[...], preferred_element_type=F32)


def _attn_prompt_bounded_kernel(shift_ref, q_ref, iq_ref, iw_ref, k_ref, v_ref, ika_ref, ikb_ref,
                                wout_ref, h_ref, out_ref, keys_sc, iqt_sc, o_sc, acc_sc, *, kc, topk):
    i = pl.program_id(1)
    q0 = i * QB
    nch = (q0 + QB + kc - 1) // kc
    nlt = kc // LANES
    qcol = q0 + lax.broadcasted_iota(I32, (LANES, QB), 1)
    krow = lax.broadcasted_iota(I32, (LANES, QB), 0)
    for p in range(IDX_HEADS // 2):
        sl = slice(p * LANES, (p + 1) * LANES)
        iqt_sc[sl, :] = iq_ref[:, sl].astype(F32).T.astype(BF16)
    iwt = iw_ref[...].T

    def score_chunk(c, carry):
        k0 = pl.multiple_of(c * kc, LANES)
        ka = ika_ref[pl.ds(k0, kc), :]
        kb = ikb_ref[pl.ds(k0, kc), :]
        sc = [jnp.zeros((LANES, QB), F32) for _ in range(nlt)]
        for p in range(IDX_HEADS // 2):
            w = iqt_sc[p * LANES:(p + 1) * LANES, :]
            sa = jnp.dot(ka, w, preferred_element_type=F32)
            sb = jnp.dot(kb, w, preferred_element_type=F32)
            wa = iwt[2 * p:2 * p + 1, :]
            wb = iwt[2 * p + 1:2 * p + 2, :]
            for lt in range(nlt):
                sl = slice(lt * LANES, (lt + 1) * LANES)
                sc[lt] = sc[lt] + wa * jnp.maximum(sa[sl], 0.0) + wb * jnp.maximum(sb[sl], 0.0)
        for lt in range(nlt):
            s = jnp.where(k0 + lt * LANES + krow <= qcol, sc[lt] * (IDX_DIM ** -0.5), NEG)
            keys_sc[pl.ds(k0 + lt * LANES, LANES), :] = _sort_key(s)
        return carry

    lax.fori_loop(0, nch, score_chunk, 0)

    def count_ge(cand):
        def body(c, acc):
            k0 = pl.multiple_of(c * kc, LANES)
            acc = list(acc)
            for j in range(kc // SUBLANES):
                kk = keys_sc[pl.ds(k0 + j * SUBLANES, SUBLANES), :]
                acc[j % len(acc)] = acc[j % len(acc)] + jnp.where(kk >= cand, 1.0, 0.0)
            return tuple(acc)
        acc = lax.fori_loop(0, nch, body, tuple(jnp.zeros((SUBLANES, QB), F32) for _ in range(4)))
        tot = (acc[0] + acc[1]) + (acc[2] + acc[3])
        for s in (4, 2, 1):
            tot = tot + pltpu.roll(tot, s, 0)
        return tot

    thr = _kth_largest_key(count_ge, (SUBLANES, QB), float(topk))
    thr = jnp.broadcast_to(thr[0:1, :], (LANES, QB))

    scale = HEAD_DIM ** -0.5
    shift = shift_ref[0, 0]
    acc_sc[...] = jnp.zeros(acc_sc.shape, F32)
    ones_col = jnp.where(lax.broadcasted_iota(I32, (kc, LANES), 1) == 0, 1.0, 0.0).astype(BF16)

    def att_chunk(c, carry):
        k0 = pl.multiple_of(c * kc, LANES)
        bias = []
        for lt in range(nlt):
            kk = keys_sc[pl.ds(k0 + lt * LANES, LANES), :]
            sel = (kk >= thr) & (k0 + lt * LANES + krow <= qcol)
            bias.append(jnp.where(sel, -shift, NEG).T)
        bias = jnp.concatenate(bias, axis=1)
        bias = jnp.concatenate([bias] * GROUP, axis=0)
        for n in range(N_KV):
            qs = jnp.concatenate(
                [q_ref[:, (n * GROUP + g) * LANES:(n * GROUP + g + 1) * LANES] for g in range(GROUP)], axis=0)
            kt = k_ref[pl.ds(k0, kc), n * LANES:(n + 1) * LANES]
            vt = jnp.concatenate([v_ref[pl.ds(k0, kc), n * LANES:(n + 1) * LANES], ones_col], axis=1)
            s = lax.dot_general(qs, kt, _NT, preferred_element_type=F32)
            p = jnp.exp(s * scale + bias).astype(BF16)
            acc_sc[n] += jnp.dot(p, vt, preferred_element_type=F32)
        return carry

    lax.fori_loop(0, nch, att_chunk, 0)
    for n in range(N_KV):
        a = acc_sc[n]
        o = a[:, :LANES] / a[:, LANES:LANES + 1]
        for g in range(GROUP):
            hh = n * GROUP + g
            o_sc[:, hh * LANES:(hh + 1) * LANES] = o[g * QB:(g + 1) * QB].astype(BF16)

    out_ref[...] = h_ref[...] + jnp.dot(o_sc[...], wout_ref[...], preferred_element_type=F32)


MAX_LOGIT_BOUND = 40.0


def _attn_prompt(h, q, iq, iw, kb, vb, ika, ikb, wout, logit_bound, *, batch, tp, kc, topk):
    args = (q, iq, iw, kb, vb, ika, ikb, wout, h)
    return lax.cond(
        logit_bound <= MAX_LOGIT_BOUND,
        lambda: _attn_prompt_bounded(logit_bound.reshape(1, 1), *args, batch=batch, tp=tp, kc=kc, topk=topk),
        lambda: _attn_prompt_online(*args, batch=batch, tp=tp, kc=kc, topk=topk))


def _attn_prompt_bounded(shift, q, iq, iw, kb, vb, ika, ikb, wout, h, *, batch, tp, kc, topk):
    nqb = tp // QB
    qrow = lambda b, i: (b * nqb + i, 0)
    seq = lambda b, i: (b, 0)
    const = lambda b, i: (0, 0)
    return pl.pallas_call(
        functools.partial(_attn_prompt_bounded_kernel, kc=kc, topk=topk),
        grid=(batch, nqb),
        in_specs=[
            pl.BlockSpec(memory_space=pltpu.SMEM),
            pl.BlockSpec((QB, Q_W), qrow),
            pl.BlockSpec((QB, IQ_W), qrow),
            pl.BlockSpec((QB, LANES), qrow),
            pl.BlockSpec((tp, KV_W), seq),
            pl.BlockSpec((tp, KV_W), seq),
            pl.BlockSpec((tp, LANES), seq),
            pl.BlockSpec((tp, LANES), seq),
            pl.BlockSpec((Q_W, D_MODEL), const),
            pl.BlockSpec((QB, D_MODEL), qrow),
        ],
        out_specs=pl.BlockSpec((QB, D_MODEL), qrow),
        out_shape=jax.ShapeDtypeStruct(h.shape, F32),
        scratch_shapes=[
            pltpu.VMEM((tp, QB), I32),
            pltpu.VMEM((IQ_W, QB), BF16),
            pltpu.VMEM((QB, Q_W), BF16),
            pltpu.VMEM((N_KV, GROUP * QB, 2 * HEAD_DIM), F32),
        ],
        compiler_params=pltpu.CompilerParams(
            dimension_semantics=("arbitrary", "arbitrary"), vmem_limit_bytes=VMEM_LIMIT),
        name="attn_prompt_bounded",
    )(shift, q, iq, iw, kb, vb, ika, ikb, wout, h)


def _attn_prompt_online(q, iq, iw, kb, vb, ika, ikb, wout, h, *, batch, tp, kc, topk):
    nqb = tp // QB
    qrow = lambda b, i: (b * nqb + i, 0)
    seq = lambda b, i: (b, 0)
    const = lambda b, i: (0, 0)
    return pl.pallas_call(
        functools.partial(_attn_prompt_kernel, kc=kc, topk=topk),
        grid=(batch, nqb),
        in_specs=[
            pl.BlockSpec((QB, Q_W), qrow),
            pl.BlockSpec((QB, IQ_W), qrow),
            pl.BlockSpec((QB, LANES), qrow),
            pl.BlockSpec((tp, KV_W), seq),
            pl.BlockSpec((tp, KV_W), seq),
            pl.BlockSpec((tp, LANES), seq),
            pl.BlockSpec((tp, LANES), seq),
            pl.BlockSpec((Q_W, D_MODEL), const),
            pl.BlockSpec((QB, D_MODEL), qrow),
        ],
        out_specs=pl.BlockSpec((QB, D_MODEL), qrow),
        out_shape=jax.ShapeDtypeStruct(h.shape, F32),
        scratch_shapes=[
            pltpu.VMEM((QB, tp), I32),
            pltpu.VMEM((QB, Q_W), BF16),
            pltpu.VMEM((GROUP * QB, 1), F32),
            pltpu.VMEM((GROUP * QB, 1), F32),
            pltpu.VMEM((GROUP * QB, HEAD_DIM), F32),
        ],
        compiler_params=pltpu.CompilerParams(
            dimension_semantics=("arbitrary", "arbitrary"), vmem_limit_bytes=VMEM_LIMIT),
        name="attn_prompt",
    )(q, iq, iw, kb, vb, ika, ikb, wout, h)


S1_PAGES = 16
S2_PAGES = 8


def _idx_page_scores(iq, iww, page_bf):
    s = lax.dot_general(iq, page_bf, _NT, preferred_element_type=F32)
    sc = jnp.zeros((SUBLANES, PAGE), F32)
    for hh in range(IDX_HEADS):
        sl = slice(hh * SUBLANES, (hh + 1) * SUBLANES)
        sc = sc + iww[sl, :] * jnp.maximum(s[sl, :], 0.0)
    return sc * (IDX_DIM ** -0.5)


def _sample_index_kernel(pt_ref, *refs, past, n_new, topk):
    pages = refs[:S1_PAGES]
    iq_ref, iww_ref, iknew_ref, keys_ref, thr_ref = refs[S1_PAGES:]
    s = pl.program_id(1)
    iq = iq_ref[0]
    iww = iww_ref[0]
    for r in range(S1_PAGES):
        sc = _idx_page_scores(iq, iww, pages[r][...].astype(BF16))
        off = pl.multiple_of((s * S1_PAGES + r) * PAGE, PAGE)
        keys_ref[0, :, pl.ds(off, PAGE)] = _sort_key(sc)

    @pl.when(s == pl.num_programs(1) - 1)
    def _():
        sc = _idx_page_scores(iq, iww, iknew_ref[0])
        t = lax.broadcasted_iota(I32, sc.shape, 0)
        jn = lax.broadcasted_iota(I32, sc.shape, 1)
        sc = jnp.where((jn <= t) & (jn < n_new), sc, NEG)
        keys_ref[0, :, pl.ds(past, PAGE)] = _sort_key(sc)
        def count_ge(cand):
            acc = [jnp.zeros((SUBLANES, LANES), F32) for _ in range(4)]
            for j in range(keys_ref.shape[2] // LANES):
                kk = keys_ref[0, :, j * LANES:(j + 1) * LANES]
                acc[j % 4] = acc[j % 4] + jnp.where(kk >= cand, 1.0, 0.0)
            tot = (acc[0] + acc[1]) + (acc[2] + acc[3])
            return jnp.broadcast_to(jnp.sum(tot, axis=1, keepdims=True), (SUBLANES, LANES))

        thr_ref[0] = _kth_largest_key(count_ge, (SUBLANES, LANES), float(topk))


def _sample_index(page_table, cache_ik, layer, iq, iww, iknew, *, n_new, topk):
    nb, n_pages = page_table.shape
    steps = n_pages // S1_PAGES
    nk = (n_pages + 1) * PAGE
    page_spec = lambda r: pl.BlockSpec(
        (None, None, PAGE, IDX_DIM), lambda b, s, pt: (layer, pt[b, s * S1_PAGES + r], 0, 0))
    per_b = lambda b, s, pt: (b, 0, 0)
    return pl.pallas_call(
        functools.partial(_sample_index_kernel, past=n_pages * PAGE, n_new=n_new, topk=topk),
        grid_spec=pltpu.PrefetchScalarGridSpec(
            num_scalar_prefetch=1,
            grid=(nb, steps),
            in_specs=[page_spec(r) for r in range(S1_PAGES)] + [
                pl.BlockSpec((1, IDX_HEADS * SUBLANES, IDX_DIM), per_b),
                pl.BlockSpec((1, IDX_HEADS * SUBLANES, LANES), per_b),
                pl.BlockSpec((1, PAGE, IDX_DIM), per_b),
            ],
            out_specs=[
                pl.BlockSpec((1, SUBLANES, nk), per_b),
                pl.BlockSpec((1, SUBLANES, LANES), per_b),
            ],
        ),
        out_shape=[jax.ShapeDtypeStruct((nb, SUBLANES, nk), I32),
                   jax.ShapeDtypeStruct((nb, SUBLANES, LANES), I32)],
        compiler_params=pltpu.CompilerParams(
            dimension_semantics=("arbitrary", "arbitrary"), vmem_limit_bytes=VMEM_LIMIT),
        name="sample_index",
    )(page_table, *([cache_ik] * S1_PAGES), iq, iww, iknew)


def _sample_attend_kernel(pt_ref, *refs, n_new):
    kpages = refs[:S2_PAGES]
    vpages = refs[S2_PAGES:2 * S2_PAGES]
    (qbd_ref, keys_ref, keysnew_ref, thr_ref, knew_ref, vnew_ref, o_ref,
     m_sc, l_sc, acc_sc) = refs[2 * S2_PAGES:]
    s = pl.program_id(1)
    scale = HEAD_DIM ** -0.5
    qbd = qbd_ref[0]
    thr = thr_ref[0]

    @pl.when(s == 0)
    def _():
        m_sc[...] = jnp.full(m_sc.shape, NEG, F32)
        l_sc[...] = jnp.zeros(l_sc.shape, F32)
        acc_sc[...] = jnp.zeros(acc_sc.shape, F32)

    def page_update(kp, vp, bias):
        bias = jnp.concatenate([bias] * (N_HEADS), axis=0)
        sc = lax.dot_general(qbd, kp, _NT, preferred_element_type=F32) * scale + bias
        m_old = m_sc[...]
        m_new = jnp.maximum(m_old, jnp.max(sc, axis=1, keepdims=True))
        alpha = jnp.exp(m_old - m_new)
        p = jnp.exp(sc - m_new)
        l_sc[...] = alpha * l_sc[...] + jnp.sum(p, axis=1, keepdims=True)
        acc_sc[...] = alpha * acc_sc[...] + jnp.dot(p.astype(BF16), vp, preferred_element_type=F32)
        m_sc[...] = m_new

    def page_rows(ref):
        return jnp.concatenate(
            [ref[pl.ds(n, PAGE, stride=N_KV), :] for n in range(N_KV)], axis=1).astype(BF16)

    bias = jnp.concatenate(
        [jnp.where(keys_ref[0, :, r * PAGE:(r + 1) * PAGE] >= thr, 0.0, NEG) for r in range(S2_PAGES)], axis=1)
    page_update(jnp.concatenate([page_rows(kpages[r]) for r in range(S2_PAGES)], axis=0),
                jnp.concatenate([page_rows(vpages[r]) for r in range(S2_PAGES)], axis=0), bias)

    @pl.when(s == pl.num_programs(1) - 1)
    def _():
        kk = keysnew_ref[0]
        t = lax.broadcasted_iota(I32, kk.shape, 0)
        jn = lax.broadcasted_iota(I32, kk.shape, 1)
        page_update(knew_ref[0], vnew_ref[0],
                    jnp.where((kk >= thr) & (jn <= t) & (jn < n_new), 0.0, NEG))
        o_ref[0] = acc_sc[...] / l_sc[...]


def _sample_attend(page_table, cache_k, cache_v, layer, qbd, keys, thr, knew, vnew, *, n_new):
    nb, n_pages = page_table.shape
    steps = n_pages // S2_PAGES
    page_spec = lambda r: pl.BlockSpec(
        (None, None, PAGE * N_KV, HEAD_DIM), lambda b, s, pt: (layer, pt[b, s * S2_PAGES + r], 0, 0))
    per_b = lambda b, s, pt: (b, 0, 0)
    rows = N_HEADS * SUBLANES
    return pl.pallas_call(
        functools.partial(_sample_attend_kernel, n_new=n_new),
        grid_spec=pltpu.PrefetchScalarGridSpec(
            num_scalar_prefetch=1,
            grid=(nb, steps),
            in_specs=[page_spec(r) for r in range(S2_PAGES)] * 2 + [
                pl.BlockSpec((1, rows, KV_W), per_b),
                pl.BlockSpec((1, SUBLANES, S2_PAGES * PAGE), lambda b, s, pt: (b, 0, s)),
                pl.BlockSpec((1, SUBLANES, PAGE), lambda b, s, pt: (b, 0, n_pages)),
                pl.BlockSpec((1, SUBLANES, LANES), per_b),
                pl.BlockSpec((1, PAGE, KV_W), per_b),
                pl.BlockSpec((1, PAGE, KV_W), per_b),
            ],
            out_specs=pl.BlockSpec((1, rows, KV_W), per_b),
            scratch_shapes=[
                pltpu.VMEM((rows, 1), F32),
                pltpu.VMEM((rows, 1), F32),
                pltpu.VMEM((rows, KV_W), F32),
            ],
        ),
        out_shape=jax.ShapeDtypeStruct((nb, rows, KV_W), F32),
        compiler_params=pltpu.CompilerParams(
            dimension_semantics=("arbitrary", "arbitrary"), vmem_limit_bytes=VMEM_LIMIT),
        name="sample_attend",
    )(page_table, *([cache_k] * S2_PAGES), *([cache_v] * S2_PAGES), qbd, keys, keys, thr, knew, vnew)


def _matmul_res_kernel(h_ref, x_ref, w_ref, o_ref):
    o_ref[...] = h_ref[...] + jnp.dot(x_ref[...], w_ref[...], preferred_element_type=F32)


def _matmul_res(h, x, w, *, tn):
    m, n = h.shape
    kdim = x.shape[1]
    return pl.pallas_call(
        _matmul_res_kernel,
        grid=(n // tn,),
        in_specs=[pl.BlockSpec((m, tn), lambda j: (0, j)),
                  pl.BlockSpec((m, kdim), lambda j: (0, 0)),
                  pl.BlockSpec((kdim, tn), lambda j: (0, j))],
        out_specs=pl.BlockSpec((m, tn), lambda j: (0, j)),
        out_shape=jax.ShapeDtypeStruct((m, n), F32),
        compiler_params=pltpu.CompilerParams(dimension_semantics=("arbitrary",)),
        name="matmul_residual",
    )(h, x, w)


def _conv_rows(ext_sc, u, init_ref, carry_sc, c, first, cw_ref, cb_ref, tail_ref, *,
               width, shift, d0, tail_end):
    tm = u.shape[0]
    ext_sc[d0:d0 + tm, :] = u

    @pl.when(first)
    def _():
        ext_sc[0:d0, :] = init_ref[0]

    @pl.when(jnp.logical_not(first))
    def _():
        ext_sc[0:d0, :] = carry_sc[c]

    y = cb_ref[...] + cw_ref[width - 1:width, :] * u
    for j in range(1, width):
        y = y + cw_ref[width - 1 - j:width - j, :] * ext_sc[d0 - j * shift:d0 - j * shift + tm, :]
    carry_sc[c] = ext_sc[tm:tm + d0, :]
    tail_ref[0] = ext_sc[tail_end:tail_end + d0, :]
    return y


def _ffn_kernel(h_ref, g_ref, wu_ref, wg_ref, cw_ref, cb_ref, wd_ref, st_ref, out_ref, tail_ref,
                xn_sc, acc_sc, ext_sc, carry_sc, *, shift, d0, tiles_per_seq, tail_end):
    i = pl.program_id(0)
    c = pl.program_id(1)

    @pl.when(c == 0)
    def _():
        xn_sc[...] = _rms_rows(h_ref[...], g_ref[...]).astype(BF16)
        acc_sc[...] = jnp.zeros(acc_sc.shape, F32)

    xn = xn_sc[...]
    u = jnp.dot(xn, wu_ref[...], preferred_element_type=F32)
    conv = _conv_rows(ext_sc, u, st_ref, carry_sc, c, (i % tiles_per_seq) == 0, cw_ref, cb_ref,
                      tail_ref, width=FFN_CONV, shift=shift, d0=d0, tail_end=tail_end)
    mid = _gelu(conv) * jnp.dot(xn, wg_ref[...], preferred_element_type=F32)
    acc_sc[...] += jnp.dot(mid.astype(BF16), wd_ref[...], preferred_element_type=F32)

    @pl.when(c == pl.num_programs(1) - 1)
    def _():
        out_ref[...] = h_ref[...] + acc_sc[...]


def _conv_ffn(h, g, wu, wg, cw, cb, wd, state, *, tm, tf, shift, tiles_per_seq, tail_end):
    m = h.shape[0]
    d0 = state.shape[1]
    n_tiles, n_chunks = m // tm, D_FF // tf
    row = lambda i, c: (i, 0)
    return pl.pallas_call(
        functools.partial(_ffn_kernel, shift=shift, d0=d0, tiles_per_seq=tiles_per_seq, tail_end=tail_end),
        grid=(n_tiles, n_chunks),
        in_specs=[
            pl.BlockSpec((tm, D_MODEL), row),
            pl.BlockSpec((1, D_MODEL), lambda i, c: (0, 0)),
            pl.BlockSpec((D_MODEL, tf), lambda i, c: (0, c)),
            pl.BlockSpec((D_MODEL, tf), lambda i, c: (0, c)),
            pl.BlockSpec((FFN_CONV, tf), lambda i, c: (0, c)),
            pl.BlockSpec((1, tf), lambda i, c: (0, c)),
            pl.BlockSpec((tf, D_MODEL), lambda i, c: (c, 0)),
            pl.BlockSpec((1, d0, tf), lambda i, c: (i // tiles_per_seq, 0, c)),
        ],
        out_specs=[
            pl.BlockSpec((tm, D_MODEL), row),
            pl.BlockSpec((1, d0, tf), lambda i, c: (i, 0, c)),
        ],
        out_shape=[jax.ShapeDtypeStruct((m, D_MODEL), F32),
                   jax.ShapeDtypeStruct((n_tiles, d0, D_FF), F32)],
        scratch_shapes=[
            pltpu.VMEM((tm, D_MODEL), BF16),
            pltpu.VMEM((tm, D_MODEL), F32),
            pltpu.VMEM((d0 + tm, tf), F32),
            pltpu.VMEM((n_chunks, d0, tf), F32),
        ],
        compiler_params=pltpu.CompilerParams(
            dimension_semantics=("arbitrary", "arbitrary"), vmem_limit_bytes=VMEM_LIMIT),
        name="conv_ffn",
    )(h, g, wu, wg, cw, cb, wd, state)


LRU_BLOCKS_PER_STEP = 2

def _scan_rows(a_sc, b_sc, hc, *, shift):
    tm = a_sc.shape[0]
    ngroups = tm // SUBLANES
    row = lax.broadcasted_iota(I32, (SUBLANES, a_sc.shape[1]), 0)

    def body(gi, hc):
        r0 = pl.multiple_of(gi * SUBLANES, SUBLANES)
        a = a_sc[pl.ds(r0, SUBLANES), :]
        b = b_sc[pl.ds(r0, SUBLANES), :]
        if shift == 1:
            for s in (1, 2, 4):
                keep = row >= s
                a_sh = jnp.where(keep, pltpu.roll(a, s, 0), 1.0)
                b_sh = jnp.where(keep, pltpu.roll(b, s, 0), 0.0)
                b = a * b_sh + b
                a = a * a_sh
        hrows = a * hc + b
        b_sc[pl.ds(r0, SUBLANES), :] = hrows
        if shift == 1:
            return jnp.broadcast_to(hrows[SUBLANES - 1:SUBLANES, :], hrows.shape)
        return hrows

    return lax.fori_loop(0, ngroups, body, hc, unroll=4 if ngroups % 4 == 0 else 1)


def _lru_kernel(h_ref, g_ref, wx_ref, wy_ref, cw_ref, cb_ref, wa_ref, ba_ref, wi_ref, bi_ref,
                lam_ref, wo_ref, h0_ref, st_ref, out_ref, htail_ref, utail_ref,
                xn_sc, acc_sc, ext_sc, a_sc, b_sc, gate_sc, carry_u, carry_h, *,
                shift, d0, tiles_per_seq, tail_end):
    i = pl.program_id(0)
    c = pl.program_id(1)
    first = (i % tiles_per_seq) == 0

    @pl.when(c == 0)
    def _():
        xn_sc[...] = _rms_rows(h_ref[...], g_ref[...]).astype(BF16)
        acc_sc[...] = jnp.zeros(acc_sc.shape, F32)

    xn = xn_sc[...]
    u = jnp.dot(xn, wx_ref[...], preferred_element_type=F32)
    conv = _conv_rows(ext_sc, u, st_ref, carry_u, c, first, cw_ref, cb_ref, utail_ref,
                      width=LRU_CONV, shift=shift, d0=d0, tail_end=tail_end)
    lam = lam_ref[...]
    log_sig = jnp.minimum(lam, 0.0) - jnp.log(1.0 + jnp.exp(-jnp.abs(lam)))
    for s in range(wa_ref.shape[0]):
        sl = slice(s * RNN_BD, (s + 1) * RNN_BD)
        cs = conv[:, sl]
        cbf = cs.astype(BF16)
        r = _sigmoid(jnp.dot(cbf, wa_ref[s], preferred_element_type=F32) + ba_ref[:, sl])
        ig = _sigmoid(jnp.dot(cbf, wi_ref[s], preferred_element_type=F32) + bi_ref[:, sl])
        log_a = LRU_C * r * log_sig[:, sl]
        a = jnp.exp(log_a)
        a_sc[:, sl] = a
        b_sc[:, sl] = jnp.sqrt(-jnp.tanh(log_a) * (a * a + 1.0)) * (ig * cs)
    gate_sc[...] = _gelu(jnp.dot(xn, wy_ref[...], preferred_element_type=F32))

    @pl.when(first)
    def _():
        carry_h[c] = h0_ref[0]

    carry_h[c] = _scan_rows(a_sc, b_sc, carry_h[c], shift=shift)
    htail_ref[0] = b_sc[tail_end - SUBLANES:tail_end, :]
    acc_sc[...] += jnp.dot((b_sc[...] * gate_sc[...]).astype(BF16), wo_ref[...],
                           preferred_element_type=F32)

    @pl.when(c == pl.num_programs(1) - 1)
    def _():
        out_ref[...] = h_ref[...] + acc_sc[...]


def _lru_block(h, g, wx, wy, cw, cb, wa, ba, wi, bi, lam, wo, h0, state, *,
               tm, shift, tiles_per_seq, tail_end):
    m = h.shape[0]
    d0 = state.shape[1]
    n_tiles = m // tm
    nsub = LRU_BLOCKS_PER_STEP
    bd = nsub * RNN_BD
    nsteps = RNN_BLOCKS // nsub
    row = lambda i, c: (i, 0)
    col = lambda i, c: (0, c)
    return pl.pallas_call(
        functools.partial(_lru_kernel, shift=shift, d0=d0, tiles_per_seq=tiles_per_seq, tail_end=tail_end),
        grid=(n_tiles, nsteps),
        in_specs=[
            pl.BlockSpec((tm, D_MODEL), row),
            pl.BlockSpec((1, D_MODEL), lambda i, c: (0, 0)),
            pl.BlockSpec((D_MODEL, bd), col),
            pl.BlockSpec((D_MODEL, bd), col),
            pl.BlockSpec((LRU_CONV, bd), col),
            pl.BlockSpec((1, bd), col),
            pl.BlockSpec((nsub, RNN_BD, RNN_BD), lambda i, c: (c, 0, 0)),
            pl.BlockSpec((1, bd), col),
            pl.BlockSpec((nsub, RNN_BD, RNN_BD), lambda i, c: (c, 0, 0)),
            pl.BlockSpec((1, bd), col),
            pl.BlockSpec((1, bd), col),
            pl.BlockSpec((bd, D_MODEL), lambda i, c: (c, 0)),
            pl.BlockSpec((1, SUBLANES, bd), lambda i, c: (i // tiles_per_seq, 0, c)),
            pl.BlockSpec((1, d0, bd), lambda i, c: (i // tiles_per_seq, 0, c)),
        ],
        out_specs=[
            pl.BlockSpec((tm, D_MODEL), row),
            pl.BlockSpec((1, SUBLANES, bd), lambda i, c: (i, 0, c)),
            pl.BlockSpec((1, d0, bd), lambda i, c: (i, 0, c)),
        ],
        out_shape=[jax.ShapeDtypeStruct((m, D_MODEL), F32),
                   jax.ShapeDtypeStruct((n_tiles, SUBLANES, D_MODEL), F32),
                   jax.ShapeDtypeStruct((n_tiles, d0, D_MODEL), F32)],
        scratch_shapes=[
            pltpu.VMEM((tm, D_MODEL), BF16),
            pltpu.VMEM((tm, D_MODEL), F32),
            pltpu.VMEM((d0 + tm, bd), F32),
            pltpu.VMEM((tm, bd), F32),
            pltpu.VMEM((tm, bd), F32),
            pltpu.VMEM((tm, bd), F32),
            pltpu.VMEM((nsteps, d0, bd), F32),
            pltpu.VMEM((nsteps, SUBLANES, bd), F32),
        ],
        compiler_params=pltpu.CompilerParams(
            dimension_semantics=("arbitrary", "arbitrary"), vmem_limit_bytes=VMEM_LIMIT),
        name="rg_lru",
    )(h, g, wx, wy, cw, cb, wa, ba, wi, bi, lam, wo, h0, state)


def _rope_tables(pos, rot, period):
    half = rot // 2
    inv = 1.0 / (ROPE_THETA ** (jnp.arange(half, dtype=F32) * (2.0 / rot)))
    ang = pos.astype(F32)[:, None] * inv[None, :]
    lane = jnp.arange(LANES) % period
    cos = jnp.cos(ang)[:, lane % half]
    sin = jnp.sin(ang)[:, lane % half]
    return jnp.stack([
        jnp.where(lane < rot, cos, 1.0),
        jnp.where(lane < half, -sin, 0.0),
        jnp.where((lane >= half) & (lane < rot), sin, 0.0),
    ])


def _time_major(x):
    return jnp.swapaxes(x, 0, 1)


def kernel(x_prompt, x_sample, cache_k, cache_v, cache_idx_k, state_lru_h, state_lru_conv, state_ffn_conv, page_table, meta_tokens, norm1_g, norm2_g, attn_w_in, attn_q_norm_g, attn_k_norm_g, attn_idx_k_norm_g, attn_w_out, lru_w_x, lru_w_y, lru_conv_w, lru_conv_b, lru_wa, lru_ba, lru_wi, lru_bi, lru_lambda, lru_w_out, ffn_w_up, ffn_w_gate, ffn_dw_w, ffn_dw_b, ffn_w_down):
    batch, seq, _ = x_prompt.shape
    nb, ns, _ = x_sample.shape
    depth = norm1_g.shape[0]
    t_real = seq + N_META
    tp = _round_up(t_real, QB)
    n_pages = page_table.shape[1]
    past = n_pages * PAGE
    topk_p = min(TOPK_MAX, t_real // 4)
    topk_s = min(TOPK_MAX, (past + ns) // 4)
    assert nb == SUBLANES and ns <= SUBLANES and n_pages % S1_PAGES == 0 and n_pages % S2_PAGES == 0

    tiles_per_seq = 6
    tm = tp // tiles_per_seq
    assert tm * tiles_per_seq == tp and tm % (2 * SUBLANES) == 0
    tail_end = t_real - (tiles_per_seq - 1) * tm
    assert tail_end % SUBLANES == 0 and tail_end >= SUBLANES
    kc = 3 * LANES
    assert tp % kc == 0 and kc >= topk_p
    ms = nb * ns

    hp = jnp.concatenate([
        jnp.broadcast_to(meta_tokens[None], (batch, N_META, D_MODEL)), x_prompt,
        jnp.zeros((batch, tp - t_real, D_MODEL), F32)], axis=1).reshape(batch * tp, D_MODEL)
    hs = _time_major(x_sample).reshape(ms, D_MODEL)

    rq_p = _rope_tables(jnp.arange(tp), ROT, HEAD_DIM)
    ri_p = _rope_tables(jnp.arange(tp), IDX_ROT, IDX_DIM)
    pos_s = jnp.repeat(past + jnp.arange(ns), nb)
    rq_s = _rope_tables(pos_s, ROT, HEAD_DIM)
    ri_s = _rope_tables(pos_s, IDX_ROT, IDX_DIM)

    cache_k4 = cache_k.reshape(cache_k.shape[:2] + (PAGE * N_KV, HEAD_DIM))
    cache_v4 = cache_v.reshape(cache_v.shape[:2] + (PAGE * N_KV, HEAD_DIM))

    kp, vp, ikp, ksl, vsl, iksl = [], [], [], [], [], []
    hpl, cpl, hsl, csl, fpl, fsl = [], [], [], [], [], []
    last_tiles = jnp.arange(batch) * tiles_per_seq + tiles_per_seq - 1

    for layer in range(depth):
        mi = layer // 2
        g1 = norm1_g[layer][None]
        if layer % 2 == 0:
            w_in = jnp.pad(attn_w_in[mi], ((0, 0), (0, PROJ_WP - PROJ_W))).astype(BF16)
            w_out = attn_w_out[mi].astype(BF16)
            qg = attn_q_norm_g[mi][None]
            kg = attn_k_norm_g[mi][None]
            ikg = jnp.pad(attn_idx_k_norm_g[mi], (0, LANES - IDX_DIM))[None]

            q, kf, kb, vf, vb, iq, ikf, ika, ikb, iw = _attn_project(
                hp, g1, w_in, qg, kg, ikg, rq_p, ri_p, tm=kc)
            logit_bound = (1.02 * HEAD_DIM ** 0.5) * jnp.max(jnp.abs(qg)) * jnp.max(jnp.abs(kg))
            hp = _attn_prompt(hp, q, iq, iw, kb, vb, ika, ikb, w_out, logit_bound,
                              batch=batch, tp=tp, kc=kc, topk=topk_p)
            kp.append(kf.reshape(batch, tp, N_KV, HEAD_DIM)[:, :t_real])
            vp.append(vf.reshape(batch, tp, N_KV, HEAD_DIM)[:, :t_real])
            ikp.append(ikf.reshape(batch, tp, IDX_DIM)[:, :t_real])

            q, kf, kb, vf, vb, iq, ikf, ika, ikb, iw = _attn_project(
                hs, g1, w_in, qg, kg, ikg, rq_s, ri_s, tm=ms)
            ksl.append(_time_major(kf.reshape(ns, nb, N_KV, HEAD_DIM)))
            vsl.append(_time_major(vf.reshape(ns, nb, N_KV, HEAD_DIM)))
            iksl.append(_time_major(ikf.reshape(ns, nb, IDX_DIM)))

            pad_q = lambda x: jnp.pad(x, [(0, 0)] * (x.ndim - 2) + [(0, SUBLANES - ns), (0, 0)])
            iq_b = pad_q(iq.reshape(ns, nb, IDX_HEADS, IDX_DIM).transpose(1, 2, 0, 3))
            iq_b = iq_b.reshape(nb, IDX_HEADS * SUBLANES, IDX_DIM)
            iw_b = jnp.pad(iw[:, :IDX_HEADS].reshape(ns, nb, IDX_HEADS).transpose(1, 2, 0), ((0, 0), (0, 0), (0, SUBLANES - ns)))
            iww = jnp.broadcast_to(iw_b.reshape(nb, IDX_HEADS * SUBLANES, 1), (nb, IDX_HEADS * SUBLANES, LANES))
            new_page = lambda x: jnp.pad(_time_major(x.reshape(ns, nb, -1)), ((0, 0), (0, PAGE - ns), (0, 0)))
            iknew = new_page(ika[:, :IDX_DIM])
            keys, thr = _sample_index(page_table, cache_idx_k, mi, iq_b, iww, iknew, n_new=ns, topk=topk_s)

            q_b = pad_q(q.reshape(ns, nb, N_KV, GROUP, HEAD_DIM).transpose(1, 2, 3, 0, 4))
            qbd = q_b[:, :, :, :, None, :] * jnp.eye(N_KV, dtype=BF16)[None, :, None, None, :, None]
            qbd = qbd.reshape(nb, N_HEADS * SUBLANES, KV_W)
            o = _sample_attend(page_table, cache_k4, cache_v4, mi, qbd, keys, thr,
                               new_page(kb), new_page(vb), n_new=ns)
            o = o.reshape(nb, N_KV, GROUP, SUBLANES, N_KV, HEAD_DIM)
            o = jnp.stack([o[:, n, :, :ns, n, :] for n in range(N_KV)], axis=1)
            o = o.transpose(3, 0, 1, 2, 4).reshape(ms, Q_W).astype(BF16)
            hs = _matmul_res(hs, o, w_out, tn=512)
        else:
            lw = dict(
                g=g1, wx=lru_w_x[mi].astype(BF16), wy=lru_w_y[mi].astype(BF16),
                cw=lru_conv_w[mi], cb=lru_conv_b[mi][None],
                wa=lru_wa[mi].astype(BF16), ba=lru_ba[mi][None],
                wi=lru_wi[mi].astype(BF16), bi=lru_bi[mi][None],
                lam=lru_lambda[mi][None], wo=lru_w_out[mi].astype(BF16))
            hp, htail, utail = _lru_block(
                hp, h0=jnp.zeros((batch, SUBLANES, D_MODEL), F32),
                state=jnp.zeros((batch, SUBLANES, D_MODEL), F32),
                tm=tm, shift=1, tiles_per_seq=tiles_per_seq, tail_end=tail_end, **lw)
            hpl.append(htail[last_tiles, SUBLANES - 1])
            cpl.append(utail[last_tiles, SUBLANES - (LRU_CONV - 1):])
            d0s = (LRU_CONV - 1) * nb
            hs, htail, utail = _lru_block(
                hs, h0=state_lru_h[mi][None],
                state=_time_major(state_lru_conv[mi]).reshape(1, d0s, D_MODEL),
                tm=ms, shift=nb, tiles_per_seq=1, tail_end=ms, **lw)
            hsl.append(htail[0])
            csl.append(_time_major(utail.reshape(LRU_CONV - 1, nb, D_MODEL)))

        fw = dict(g=norm2_g[layer][None], wu=ffn_w_up[layer].astype(BF16), wg=ffn_w_gate[layer].astype(BF16),
                  cw=ffn_dw_w[layer], cb=ffn_dw_b[layer][None], wd=ffn_w_down[layer].astype(BF16))
        hp, tail = _conv_ffn(hp, state=jnp.zeros((batch, SUBLANES, D_FF), F32),
                             tm=tm, tf=512, shift=1, tiles_per_seq=tiles_per_seq, tail_end=tail_end, **fw)
        fpl.append(tail[last_tiles, SUBLANES - (FFN_CONV - 1):])
        d0s = (FFN_CONV - 1) * nb
        hs, tail = _conv_ffn(hs, state=_time_major(state_ffn_conv[layer]).reshape(1, d0s, D_FF),
                             tm=ms, tf=512, shift=nb, tiles_per_seq=1, tail_end=ms, **fw)
        fsl.append(_time_major(tail.reshape(FFN_CONV - 1, nb, D_FF)))

    y_prompt = hp.reshape(batch, tp, D_MODEL)[:, N_META:t_real]
    y_sample = _time_major(hs.reshape(ns, nb, D_MODEL))
    return (y_prompt, y_sample, jnp.stack(kp), jnp.stack(vp), jnp.stack(ikp),
            jnp.stack(ksl), jnp.stack(vsl), jnp.stack(iksl),
            jnp.stack(hpl), jnp.stack(cpl), jnp.stack(hsl), jnp.stack(csl),
            jnp.stack(fpl), jnp.stack(fsl))
```

```python
import functools

import jax
import jax.numpy as jnp
from jax import lax
from jax.experimental import pallas as pl
from jax.experimental.pallas import tpu as pltpu

F32 = jnp.float32
BF16 = jnp.bfloat16
I32 = jnp.int32

D_MODEL = 2048
N_META = 16
N_HEADS = 16
HEAD_DIM = 128
N_KV = 4
GROUP = N_HEADS // N_KV
ROT = HEAD_DIM // 4
IDX_HEADS = 16
IDX_DIM = 64
IDX_ROT = IDX_DIM // 4
ROPE_THETA = 500000.0
TOPK_MAX = 256
PAGE = 128
Q_W = N_HEADS * HEAD_DIM
KV_W = N_KV * HEAD_DIM
IQ_W = IDX_HEADS * IDX_DIM
PROJ_W = Q_W + 2 * KV_W + IQ_W + IDX_DIM + IDX_HEADS
D_FF = 3 * D_MODEL
FFN_CONV = 3
LRU_CONV = 4
LRU_C = 8.0
RNN_BLOCKS = 8
RNN_BD = D_MODEL // RNN_BLOCKS
EPS = 1e-6
NEG = -1e30

LANES = 128
SUBLANES = 8
QB = 128
MXU_COLS = 256
PROJ_WP = -(-PROJ_W // LANES) * LANES
VMEM_LIMIT = 56 * 1024 * 1024

_NT = (((1,), (1,)), ((), ()))
_INT_MIN = -2 ** 31


def _round_up(x, m):
    return -(-x // m) * m


def _rms_rows(x, g):
    ms = jnp.mean(x * x, axis=-1, keepdims=True)
    return x * lax.rsqrt(ms + EPS) * g


def _gelu(x):
    return 0.5 * x * (1.0 + jnp.tanh(0.7978845608028654 * (x + 0.044715 * (x * x * x))))


def _sigmoid(x):
    return 0.5 * jnp.tanh(0.5 * x) + 0.5


def _sort_key(x):
    b = lax.bitcast_convert_type(x + 0.0, I32)
    return b ^ ((b >> 31) & 0x7FFFFFFF)


def _kth_largest_key(count_ge, shape, k):
    def bit_body(bi, t):
        cand = t | jnp.left_shift(jnp.int32(1), 31 - bi)
        cnt = count_ge(cand ^ _INT_MIN)
        return jnp.where(cnt >= k, cand, t)
    t = lax.fori_loop(0, 32, bit_body, jnp.zeros(shape, I32))
    return t ^ _INT_MIN


def _rope(x, tab_ref, sh):
    return (x * tab_ref[0] + pltpu.roll(x, LANES - sh, 1) * tab_ref[1]
            + pltpu.roll(x, sh, 1) * tab_ref[2])


def _proj_kernel(h_ref, g_ref, w_ref, qg_ref, kg_ref, ikg_ref, rq_ref, ri_ref,
                 q_ref, kf_ref, kb_ref, vf_ref, vb_ref, iq_ref, ikf_ref, ika_ref, ikb_ref, iw_ref):
    xn = _rms_rows(h_ref[...], g_ref[...]).astype(BF16)

    def zcols(lo, width=MXU_COLS):
        return jnp.dot(xn, w_ref[:, lo:lo + width], preferred_element_type=F32)

    tm = h_ref.shape[0]
    halves = [slice(s * LANES, (s + 1) * LANES) for s in range(MXU_COLS // LANES)]
    for j in range(Q_W // MXU_COLS):
        z = zcols(j * MXU_COLS)
        for sl in halves:
            x = _rope(_rms_rows(z[:, sl], qg_ref[...]), rq_ref, ROT // 2)
            q_ref[:, j * MXU_COLS + sl.start:j * MXU_COLS + sl.stop] = x.astype(BF16)
    for j in range(KV_W // MXU_COLS):
        z = zcols(Q_W + j * MXU_COLS)
        for s, sl in enumerate(halves):
            n = j * len(halves) + s
            x = _rope(_rms_rows(z[:, sl], kg_ref[...]), rq_ref, ROT // 2)
            kf_ref[0, pl.ds(n, tm, stride=N_KV), :] = x
            kb_ref[:, n * HEAD_DIM:(n + 1) * HEAD_DIM] = x.astype(BF16)
        z = zcols(Q_W + KV_W + j * MXU_COLS)
        for s, sl in enumerate(halves):
            n = j * len(halves) + s
            vf_ref[0, pl.ds(n, tm, stride=N_KV), :] = z[:, sl]
        vb_ref[:, j * MXU_COLS:(j + 1) * MXU_COLS] = z.astype(BF16)
    for j in range(IQ_W // MXU_COLS):
        z = zcols(Q_W + 2 * KV_W + j * MXU_COLS)
        for sl in halves:
            x = _rope(z[:, sl], ri_ref, IDX_ROT // 2)
            iq_ref[:, j * MXU_COLS + sl.start:j * MXU_COLS + sl.stop] = x.astype(BF16)

    x = zcols(Q_W + 2 * KV_W + IQ_W, LANES)
    lane = lax.broadcasted_iota(I32, x.shape, 1)
    ms = jnp.sum(jnp.where(lane < IDX_DIM, x * x, 0.0), axis=-1, keepdims=True) * (1.0 / IDX_DIM)
    y = _rope(x * lax.rsqrt(ms + EPS) * ikg_ref[...], ri_ref, IDX_ROT // 2)
    ikf_ref[0] = y[:, :IDX_DIM]
    ika_ref[...] = y.astype(BF16)
    ikb_ref[...] = pltpu.roll(y, IDX_DIM, 1).astype(BF16)
    iw_ref[...] = jnp.where(lane < IDX_HEADS, pltpu.roll(x, LANES - IDX_DIM, 1) * (IDX_HEADS ** -0.5), 0.0)


def _attn_project(h, g, w_pad, qg, kg, ikg, rq, ri, *, tm, nseq, rows_real):
    m = h.shape[0]
    tab_tiles = rq.shape[1] // tm
    assert m == nseq * tab_tiles * tm
    row = lambda i: (i, 0)
    const = lambda i: (0, 0)
    tab = lambda i: (0, i % tab_tiles, 0)
    seq_row = lambda i: (i // tab_tiles, i % tab_tiles, 0)
    sds = jax.ShapeDtypeStruct
    return pl.pallas_call(
        _proj_kernel,
        grid=(m // tm,),
        in_specs=[
            pl.BlockSpec((tm, D_MODEL), row),
            pl.BlockSpec((1, D_MODEL), const),
            pl.BlockSpec(w_pad.shape, const, pipeline_mode=pl.Buffered(1)),
            pl.BlockSpec((1, LANES), const),
            pl.BlockSpec((1, LANES), const),
            pl.BlockSpec((1, LANES), const),
            pl.BlockSpec((3, tm, LANES), tab),
            pl.BlockSpec((3, tm, LANES), tab),
        ],
        out_specs=[
            pl.BlockSpec((tm, Q_W), row),
            pl.BlockSpec((1, tm * N_KV, HEAD_DIM), seq_row),
            pl.BlockSpec((tm, KV_W), row),
            pl.BlockSpec((1, tm * N_KV, HEAD_DIM), seq_row),
            pl.BlockSpec((tm, KV_W), row),
            pl.BlockSpec((tm, IQ_W), row),
            pl.BlockSpec((1, tm, IDX_DIM), seq_row),
            pl.BlockSpec((tm, LANES), row),
            pl.BlockSpec((tm, LANES), row),
            pl.BlockSpec((tm, LANES), row),
        ],
        out_shape=[
            sds((m, Q_W), BF16), sds((nseq, rows_real * N_KV, HEAD_DIM), F32), sds((m, KV_W), BF16),
            sds((nseq, rows_real * N_KV, HEAD_DIM), F32), sds((m, KV_W), BF16), sds((m, IQ_W), BF16),
            sds((nseq, rows_real, IDX_DIM), F32), sds((m, LANES), BF16), sds((m, LANES), BF16),
            sds((m, LANES), F32),
        ],
        compiler_params=pltpu.CompilerParams(
            dimension_semantics=("arbitrary",), vmem_limit_bytes=VMEM_LIMIT),
        name="attn_project",
    )(h, g, w_pad, qg, kg, ikg, rq, ri)


def _attn_prompt_kernel(q_ref, iq_ref, iw_ref, k_ref, v_ref, ika_ref, ikb_ref, wout_ref, h_ref,
                        out_ref, keys_sc, o_sc, m_sc, l_sc, acc_sc, *, kc, topk):
    i = pl.program_id(1)
    q0 = i * QB
    nch = (q0 + QB + kc - 1) // kc
    qpos = q0 + lax.broadcasted_iota(I32, (QB, 1), 0)
    lane = lax.broadcasted_iota(I32, (QB, kc), 1)
    iw = iw_ref[...]

    def score_chunk(c, carry):
        k0 = pl.multiple_of(c * kc, LANES)
        ka = ika_ref[pl.ds(k0, kc), :]
        kb = ikb_ref[pl.ds(k0, kc), :]
        sc = jnp.zeros((QB, kc), F32)
        for p in range(IDX_HEADS // 2):
            a = iq_ref[:, p * LANES:(p + 1) * LANES]
            sa = lax.dot_general(a, ka, _NT, preferred_element_type=F32)
            sb = lax.dot_general(a, kb, _NT, preferred_element_type=F32)
            sc = sc + iw[:, 2 * p:2 * p + 1] * jnp.maximum(sa, 0.0)
            sc = sc + iw[:, 2 * p + 1:2 * p + 2] * jnp.maximum(sb, 0.0)
        sc = sc * (IDX_DIM ** -0.5)
        sc = jnp.where(k0 + lane <= qpos, sc, NEG)
        keys_sc[:, pl.ds(k0, kc)] = _sort_key(sc)
        return carry

    lax.fori_loop(0, nch, score_chunk, 0)

    def count_ge(cand):
        def body(c, acc):
            k0 = pl.multiple_of(c * kc, LANES)
            kk = keys_sc[:, pl.ds(k0, kc)]
            return acc + jnp.sum(jnp.where(kk >= cand, 1.0, 0.0), axis=1, keepdims=True)
        return lax.fori_loop(0, nch, body, jnp.zeros((QB, 1), F32))

    thr = _kth_largest_key(count_ge, (QB, 1), float(topk))

    scale = HEAD_DIM ** -0.5
    for n in range(N_KV):
        qs = jnp.concatenate(
            [q_ref[:, (n * GROUP + g) * LANES:(n * GROUP + g + 1) * LANES] for g in range(GROUP)], axis=0)
        m_sc[...] = jnp.full(m_sc.shape, NEG, F32)
        l_sc[...] = jnp.zeros(l_sc.shape, F32)
        acc_sc[...] = jnp.zeros(acc_sc.shape, F32)

        def att_chunk(c, carry, n=n, qs=qs):
            k0 = pl.multiple_of(c * kc, LANES)
            kk = keys_sc[:, pl.ds(k0, kc)]
            sel = (kk >= thr) & (k0 + lane <= qpos)
            bias = jnp.where(sel, 0.0, NEG)
            bias = jnp.concatenate([bias] * GROUP, axis=0)
            kt = k_ref[pl.ds(k0, kc), n * LANES:(n + 1) * LANES]
            vt = v_ref[pl.ds(k0, kc), n * LANES:(n + 1) * LANES]
            s = lax.dot_general(qs, kt, _NT, preferred_element_type=F32) * scale + bias
            m_old = m_sc[...]
            m_new = jnp.maximum(m_old, jnp.max(s, axis=1, keepdims=True))
            alpha = jnp.exp(m_old - m_new)
            p = jnp.exp(s - m_new)
            l_sc[...] = alpha * l_sc[...] + jnp.sum(p, axis=1, keepdims=True)
            acc_sc[...] = alpha * acc_sc[...] + jnp.dot(p.astype(BF16), vt, preferred_element_type=F32)
            m_sc[...] = m_new
            return carry

        lax.fori_loop(0, nch, att_chunk, 0)
        o = acc_sc[...] / l_sc[...]
        for g in range(GROUP):
            hh = n * GROUP + g
            o_sc[:, hh * LANES:(hh + 1) * LANES] = o[g * QB:(g + 1) * QB].astype(BF16)

    out_ref[...] = h_ref[...] + jnp.dot(o_sc[...], wout_ref[...], preferred_element_type=F32)


def _attn_prompt_bounded_kernel(shift_ref, q_ref, iq_ref, iw_ref, k_ref, v_ref, ika_ref, ikb_ref,
                                wout_ref, h_ref, out_ref, keys_sc, iqt_sc, o_sc, acc_sc, *, kc, topk):
    i = pl.program_id(1)
    q0 = i * QB
    tp = k_ref.shape[0]
    nch = (q0 + QB + kc - 1) // kc
    nlt = kc // LANES
    qcol = q0 + lax.broadcasted_iota(I32, (LANES, QB), 1)
    krow = lax.broadcasted_iota(I32, (LANES, QB), 0)
    for p in range(IDX_HEADS // 2):
        sl = slice(p * LANES, (p + 1) * LANES)
        iqt_sc[:, sl] = iq_ref[:, sl].astype(F32).T.astype(BF16)
    iwt = iw_ref[...].T
    if keys_sc.shape[0] > tp:
        keys_sc[tp:, :] = jnp.full((keys_sc.shape[0] - tp, QB), _INT_MIN, I32)

    def chunk_start(c):
        return pl.multiple_of(jnp.minimum(c * kc, tp - kc), LANES)

    def score_chunk(c, carry):
        k0 = chunk_start(c)
        ka = ika_ref[pl.ds(k0, kc), :]
        kb = ikb_ref[pl.ds(k0, kc), :]
        sc = [jnp.zeros((LANES, QB), F32) for _ in range(nlt)]
        for j in range(IQ_W // MXU_COLS):
            w = iqt_sc[:, j * MXU_COLS:(j + 1) * MXU_COLS]
            sa = jnp.dot(ka, w, preferred_element_type=F32)
            sb = jnp.dot(kb, w, preferred_element_type=F32)
            for half in range(MXU_COLS // LANES):
                p = j * (MXU_COLS // LANES) + half
                cols = slice(half * LANES, (half + 1) * LANES)
                wa = iwt[2 * p:2 * p + 1, :]
                wb = iwt[2 * p + 1:2 * p + 2, :]
                for lt in range(nlt):
                    rows = slice(lt * LANES, (lt + 1) * LANES)
                    sc[lt] = (sc[lt] + wa * jnp.maximum(sa[rows, cols], 0.0)
                              + wb * jnp.maximum(sb[rows, cols], 0.0))
        for lt in range(nlt):
            s = jnp.where(k0 + lt * LANES + krow <= qcol, sc[lt] * (IDX_DIM ** -0.5), NEG)
            keys_sc[pl.ds(k0 + lt * LANES, LANES), :] = _sort_key(s)
        return carry

    lax.fori_loop(0, nch, score_chunk, 0)

    def count_ge(cand):
        def body(c, acc):
            k0 = pl.multiple_of(c * kc, LANES)
            acc = list(acc)
            for j in range(kc // SUBLANES):
                kk = keys_sc[pl.ds(k0 + j * SUBLANES, SUBLANES), :]
                acc[j % len(acc)] = acc[j % len(acc)] + jnp.where(kk >= cand, 1.0, 0.0)
            return tuple(acc)
        acc = lax.fori_loop(0, nch, body, tuple(jnp.zeros((SUBLANES, QB), F32) for _ in range(4)))
        tot = (acc[0] + acc[1]) + (acc[2] + acc[3])
        for s in (4, 2, 1):
            tot = tot + pltpu.roll(tot, s, 0)
        return tot

    thr = _kth_largest_key(count_ge, (SUBLANES, QB), float(topk))
    thr = jnp.broadcast_to(thr[0:1, :], (LANES, QB))

    scale = HEAD_DIM ** -0.5
    shift = shift_ref[0, 0]
    acc_sc[...] = jnp.zeros(acc_sc.shape, F32)
    ones_col = jnp.where(lax.broadcasted_iota(I32, (kc, LANES), 1) == 0, 1.0, 0.0).astype(BF16)

    def att_chunk(c, carry):
        k0 = chunk_start(c)
        bias = []
        for lt in range(nlt):
            kk = keys_sc[pl.ds(k0 + lt * LANES, LANES), :]
            kidx = k0 + lt * LANES + krow
            sel = (kk >= thr) & (kidx <= qcol) & (kidx >= c * kc)
            bias.append(jnp.where(sel, -shift, NEG).T)
        bias = jnp.concatenate(bias, axis=1)
        bias = jnp.concatenate([bias] * GROUP, axis=0)
        for n in range(N_KV):
            qs = jnp.concatenate(
                [q_ref[:, (n * GROUP + g) * LANES:(n * GROUP + g + 1) * LANES] for g in range(GROUP)], axis=0)
            kt = k_ref[pl.ds(k0, kc), n * LANES:(n + 1) * LANES]
            vt = jnp.concatenate([v_ref[pl.ds(k0, kc), n * LANES:(n + 1) * LANES], ones_col], axis=1)
            s = lax.dot_general(qs, kt, _NT, preferred_element_type=F32)
            p = jnp.exp(s * scale + bias).astype(BF16)
            acc_sc[n] += jnp.dot(p, vt, preferred_element_type=F32)
        return carry

    lax.fori_loop(0, nch, att_chunk, 0)
    for n in range(N_KV):
        a = acc_sc[n]
        o = a[:, :LANES] / a[:, LANES:LANES + 1]
        for g in range(GROUP):
            hh = n * GROUP + g
            o_sc[:, hh * LANES:(hh + 1) * LANES] = o[g * QB:(g + 1) * QB].astype(BF16)

    out_ref[...] = h_ref[...] + jnp.dot(o_sc[...], wout_ref[...], preferred_element_type=F32)


MAX_LOGIT_BOUND = 40.0


def _attn_prompt(h, q, iq, iw, kb, vb, ika, ikb, wout, logit_bound, *, batch, tp, topk):
    args = (q, iq, iw, kb, vb, ika, ikb, wout, h)
    kc_online = 3 * LANES
    assert MXU_COLS >= topk and (tp - MXU_COLS) % LANES == 0 and kc_online >= topk and tp % kc_online == 0
    return lax.cond(
        logit_bound <= MAX_LOGIT_BOUND,
        lambda: _attn_prompt_bounded(logit_bound.reshape(1, 1), *args, batch=batch, tp=tp, kc=kc_online, topk=topk),
        lambda: _attn_prompt_online(*args, batch=batch, tp=tp, kc=kc_online, topk=topk))


def _attn_prompt_bounded(shift, q, iq, iw, kb, vb, ika, ikb, wout, h, *, batch, tp, kc, topk):
    nqb = tp // QB
    qrow = lambda b, i: (b * nqb + i, 0)
    seq = lambda b, i: (b, 0)
    const = lambda b, i: (0, 0)
    return pl.pallas_call(
        functools.partial(_attn_prompt_bounded_kernel, kc=kc, topk=topk),
        grid=(batch, nqb),
        in_specs=[
            pl.BlockSpec(memory_space=pltpu.SMEM),
            pl.BlockSpec((QB, Q_W), qrow),
            pl.BlockSpec((QB, IQ_W), qrow),
            pl.BlockSpec((QB, LANES), qrow),
            pl.BlockSpec((tp, KV_W), seq),
            pl.BlockSpec((tp, KV_W), seq),
            pl.BlockSpec((tp, LANES), seq),
            pl.BlockSpec((tp, LANES), seq),
            pl.BlockSpec((Q_W, D_MODEL), const),
            pl.BlockSpec((QB, D_MODEL), qrow),
        ],
        out_specs=pl.BlockSpec((QB, D_MODEL), qrow),
        out_shape=jax.ShapeDtypeStruct(h.shape, F32),
        scratch_shapes=[
            pltpu.VMEM((_round_up(tp, kc), QB), I32),
            pltpu.VMEM((LANES, IQ_W), BF16),
            pltpu.VMEM((QB, Q_W), BF16),
            pltpu.VMEM((N_KV, GROUP * QB, 2 * HEAD_DIM), F32),
        ],
        compiler_params=pltpu.CompilerParams(
            dimension_semantics=("arbitrary", "arbitrary"), vmem_limit_bytes=VMEM_LIMIT),
        name="attn_prompt_bounded",
    )(shift, q, iq, iw, kb, vb, ika, ikb, wout, h)


def _attn_prompt_online(q, iq, iw, kb, vb, ika, ikb, wout, h, *, batch, tp, kc, topk):
    nqb = tp // QB
    qrow = lambda b, i: (b * nqb + i, 0)
    seq = lambda b, i: (b, 0)
    const = lambda b, i: (0, 0)
    return pl.pallas_call(
        functools.partial(_attn_prompt_kernel, kc=kc, topk=topk),
        grid=(batch, nqb),
        in_specs=[
            pl.BlockSpec((QB, Q_W), qrow),
            pl.BlockSpec((QB, IQ_W), qrow),
            pl.BlockSpec((QB, LANES), qrow),
            pl.BlockSpec((tp, KV_W), seq),
            pl.BlockSpec((tp, KV_W), seq),
            pl.BlockSpec((tp, LANES), seq),
            pl.BlockSpec((tp, LANES), seq),
            pl.BlockSpec((Q_W, D_MODEL), const),
            pl.BlockSpec((QB, D_MODEL), qrow),
        ],
        out_specs=pl.BlockSpec((QB, D_MODEL), qrow),
        out_shape=jax.ShapeDtypeStruct(h.shape, F32),
        scratch_shapes=[
            pltpu.VMEM((QB, tp), I32),
            pltpu.VMEM((QB, Q_W), BF16),
            pltpu.VMEM((GROUP * QB, 1), F32),
            pltpu.VMEM((GROUP * QB, 1), F32),
            pltpu.VMEM((GROUP * QB, HEAD_DIM), F32),
        ],
        compiler_params=pltpu.CompilerParams(
            dimension_semantics=("arbitrary", "arbitrary"), vmem_limit_bytes=VMEM_LIMIT),
        name="attn_prompt",
    )(q, iq, iw, kb, vb, ika, ikb, wout, h)


S1_PAGES = 16
S2_PAGES = 8


def _idx_page_scores(iq, iww, page_bf):
    s = lax.dot_general(iq, page_bf, _NT, preferred_element_type=F32)
    sc = jnp.zeros((SUBLANES, PAGE), F32)
    for hh in range(IDX_HEADS):
        sl = slice(hh * SUBLANES, (hh + 1) * SUBLANES)
        sc = sc + iww[sl, :] * jnp.maximum(s[sl, :], 0.0)
    return sc * (IDX_DIM ** -0.5)


def _sample_index_kernel(pt_ref, *refs, past, n_new, topk):
    pages = refs[:S1_PAGES]
    iq_ref, iww_ref, iknew_ref, keys_ref, thr_ref = refs[S1_PAGES:]
    s = pl.program_id(1)
    iq = iq_ref[0]
    iww = iww_ref[0]
    for r in range(S1_PAGES):
        sc = _idx_page_scores(iq, iww, pages[r][...].astype(BF16))
        off = pl.multiple_of((s * S1_PAGES + r) * PAGE, PAGE)
        keys_ref[0, :, pl.ds(off, PAGE)] = _sort_key(sc)

    @pl.when(s == pl.num_programs(1) - 1)
    def _():
        sc = _idx_page_scores(iq, iww, iknew_ref[0])
        t = lax.broadcasted_iota(I32, sc.shape, 0)
        jn = lax.broadcasted_iota(I32, sc.shape, 1)
        sc = jnp.where((jn <= t) & (jn < n_new), sc, NEG)
        keys_ref[0, :, pl.ds(past, PAGE)] = _sort_key(sc)
        def count_ge(cand):
            acc = [jnp.zeros((SUBLANES, LANES), F32) for _ in range(4)]
            for j in range(keys_ref.shape[2] // LANES):
                kk = keys_ref[0, :, j * LANES:(j + 1) * LANES]
                acc[j % 4] = acc[j % 4] + jnp.where(kk >= cand, 1.0, 0.0)
            tot = (acc[0] + acc[1]) + (acc[2] + acc[3])
            return jnp.broadcast_to(jnp.sum(tot, axis=1, keepdims=True), (SUBLANES, LANES))

        thr_ref[0] = _kth_largest_key(count_ge, (SUBLANES, LANES), float(topk))


def _sample_index(page_table, cache_ik, layer, iq, iww, iknew, *, n_new, topk):
    nb, n_pages = page_table.shape
    steps = n_pages // S1_PAGES
    nk = (n_pages + 1) * PAGE
    page_spec = lambda r: pl.BlockSpec(
        (None, None, PAGE, IDX_DIM), lambda b, s, pt: (layer, pt[b, s * S1_PAGES + r], 0, 0))
    per_b = lambda b, s, pt: (b, 0, 0)
    return pl.pallas_call(
        functools.partial(_sample_index_kernel, past=n_pages * PAGE, n_new=n_new, topk=topk),
        grid_spec=pltpu.PrefetchScalarGridSpec(
            num_scalar_prefetch=1,
            grid=(nb, steps),
            in_specs=[page_spec(r) for r in range(S1_PAGES)] + [
                pl.BlockSpec((1, IDX_HEADS * SUBLANES, IDX_DIM), per_b),
                pl.BlockSpec((1, IDX_HEADS * SUBLANES, LANES), per_b),
                pl.BlockSpec((1, PAGE, IDX_DIM), per_b),
            ],
            out_specs=[
                pl.BlockSpec((1, SUBLANES, nk), per_b),
                pl.BlockSpec((1, SUBLANES, LANES), per_b),
            ],
        ),
        out_shape=[jax.ShapeDtypeStruct((nb, SUBLANES, nk), I32),
                   jax.ShapeDtypeStruct((nb, SUBLANES, LANES), I32)],
        compiler_params=pltpu.CompilerParams(
            dimension_semantics=("arbitrary", "arbitrary"), vmem_limit_bytes=VMEM_LIMIT),
        name="sample_index",
    )(page_table, *([cache_ik] * S1_PAGES), iq, iww, iknew)


def _sample_attend_kernel(pt_ref, *refs, n_new):
    kpages = refs[:S2_PAGES]
    vpages = refs[S2_PAGES:2 * S2_PAGES]
    (qbd_ref, keys_ref, keysnew_ref, thr_ref, knew_ref, vnew_ref, o_ref,
     m_sc, l_sc, acc_sc) = refs[2 * S2_PAGES:]
    s = pl.program_id(1)
    scale = HEAD_DIM ** -0.5
    qbd = qbd_ref[0]
    thr = thr_ref[0]

    @pl.when(s == 0)
    def _():
        m_sc[...] = jnp.full(m_sc.shape, NEG, F32)
        l_sc[...] = jnp.zeros(l_sc.shape, F32)
        acc_sc[...] = jnp.zeros(acc_sc.shape, F32)

    def page_update(kp, vp, bias):
        bias = jnp.concatenate([bias] * (N_HEADS), axis=0)
        sc = lax.dot_general(qbd, kp, _NT, preferred_element_type=F32) * scale + bias
        m_old = m_sc[...]
        m_new = jnp.maximum(m_old, jnp.max(sc, axis=1, keepdims=True))
        alpha = jnp.exp(m_old - m_new)
        p = jnp.exp(sc - m_new)
        l_sc[...] = alpha * l_sc[...] + jnp.sum(p, axis=1, keepdims=True)
        acc_sc[...] = alpha * acc_sc[...] + jnp.dot(p.astype(BF16), vp, preferred_element_type=F32)
        m_sc[...] = m_new

    def page_rows(ref):
        return jnp.concatenate(
            [ref[pl.ds(n, PAGE, stride=N_KV), :] for n in range(N_KV)], axis=1).astype(BF16)

    bias = jnp.concatenate(
        [jnp.where(keys_ref[0, :, r * PAGE:(r + 1) * PAGE] >= thr, 0.0, NEG) for r in range(S2_PAGES)], axis=1)
    page_update(jnp.concatenate([page_rows(kpages[r]) for r in range(S2_PAGES)], axis=0),
                jnp.concatenate([page_rows(vpages[r]) for r in range(S2_PAGES)], axis=0), bias)

    @pl.when(s == pl.num_programs(1) - 1)
    def _():
        kk = keysnew_ref[0]
        t = lax.broadcasted_iota(I32, kk.shape, 0)
        jn = lax.broadcasted_iota(I32, kk.shape, 1)
        page_update(knew_ref[0], vnew_ref[0],
                    jnp.where((kk >= thr) & (jn <= t) & (jn < n_new), 0.0, NEG))
        o_ref[0] = acc_sc[...] / l_sc[...]


def _sample_attend(page_table, cache_k, cache_v, layer, qbd, keys, thr, knew, vnew, *, n_new):
    nb, n_pages = page_table.shape
    steps = n_pages // S2_PAGES
    page_spec = lambda r: pl.BlockSpec(
        (None, None, PAGE * N_KV, HEAD_DIM), lambda b, s, pt: (layer, pt[b, s * S2_PAGES + r], 0, 0))
    per_b = lambda b, s, pt: (b, 0, 0)
    rows = N_HEADS * SUBLANES
    return pl.pallas_call(
        functools.partial(_sample_attend_kernel, n_new=n_new),
        grid_spec=pltpu.PrefetchScalarGridSpec(
            num_scalar_prefetch=1,
            grid=(nb, steps),
            in_specs=[page_spec(r) for r in range(S2_PAGES)] * 2 + [
                pl.BlockSpec((1, rows, KV_W), per_b),
                pl.BlockSpec((1, SUBLANES, S2_PAGES * PAGE), lambda b, s, pt: (b, 0, s)),
                pl.BlockSpec((1, SUBLANES, PAGE), lambda b, s, pt: (b, 0, n_pages)),
                pl.BlockSpec((1, SUBLANES, LANES), per_b),
                pl.BlockSpec((1, PAGE, KV_W), per_b),
                pl.BlockSpec((1, PAGE, KV_W), per_b),
            ],
            out_specs=pl.BlockSpec((1, rows, KV_W), per_b),
            scratch_shapes=[
                pltpu.VMEM((rows, 1), F32),
                pltpu.VMEM((rows, 1), F32),
                pltpu.VMEM((rows, KV_W), F32),
            ],
        ),
        out_shape=jax.ShapeDtypeStruct((nb, rows, KV_W), F32),
        compiler_params=pltpu.CompilerParams(
            dimension_semantics=("arbitrary", "arbitrary"), vmem_limit_bytes=VMEM_LIMIT),
        name="sample_attend",
    )(page_table, *([cache_k] * S2_PAGES), *([cache_v] * S2_PAGES), qbd, keys, keys, thr, knew, vnew)


def _matmul_res_kernel(h_ref, x_ref, w_ref, o_ref):
    o_ref[...] = h_ref[...] + jnp.dot(x_ref[...], w_ref[...], preferred_element_type=F32)


def _matmul_res(h, x, w, *, tn):
    m, n = h.shape
    kdim = x.shape[1]
    return pl.pallas_call(
        _matmul_res_kernel,
        grid=(n // tn,),
        in_specs=[pl.BlockSpec((m, tn), lambda j: (0, j)),
                  pl.BlockSpec((m, kdim), lambda j: (0, 0)),
                  pl.BlockSpec((kdim, tn), lambda j: (0, j))],
        out_specs=pl.BlockSpec((m, tn), lambda j: (0, j)),
        out_shape=jax.ShapeDtypeStruct((m, n), F32),
        compiler_params=pltpu.CompilerParams(dimension_semantics=("arbitrary",)),
        name="matmul_residual",
    )(h, x, w)


def _conv_rows(ext_sc, u, init_ref, carry_sc, c, first, cw_ref, cb_ref, tail_ref, *,
               width, shift, d0, tail_end):
    tm = u.shape[0]
    ext_sc[d0:d0 + tm, :] = u
    ext_sc[0:d0, :] = jnp.where(first, init_ref[0], carry_sc[c])

    y = cb_ref[...] + cw_ref[width - 1:width, :] * u
    for j in range(1, width):
        y = y + cw_ref[width - 1 - j:width - j, :] * ext_sc[d0 - j * shift:d0 - j * shift + tm, :]
    carry_sc[c] = ext_sc[tm:tm + d0, :]
    tail_ref[0] = ext_sc[tail_end:tail_end + d0, :]
    return y


def _ffn_kernel(h_ref, g_ref, wu_ref, wg_ref, cw_ref, cb_ref, wd_ref, st_ref, out_ref, tail_ref,
                xn_sc, acc_sc, ext_sc, carry_sc, *, shift, d0, tiles_per_seq, tail_end):
    i = pl.program_id(0)
    c = pl.program_id(1)

    @pl.when(c == 0)
    def _():
        xn_sc[...] = _rms_rows(h_ref[...], g_ref[...]).astype(BF16)
        acc_sc[...] = jnp.zeros(acc_sc.shape, F32)

    xn = xn_sc[...]
    u = jnp.dot(xn, wu_ref[...], preferred_element_type=F32)
    conv = _conv_rows(ext_sc, u, st_ref, carry_sc, c, (i % tiles_per_seq) == 0, cw_ref, cb_ref,
                      tail_ref, width=FFN_CONV, shift=shift, d0=d0, tail_end=tail_end)
    mid = _gelu(conv) * jnp.dot(xn, wg_ref[...], preferred_element_type=F32)
    acc_sc[...] += jnp.dot(mid.astype(BF16), wd_ref[...], preferred_element_type=F32)

    @pl.when(c == pl.num_programs(1) - 1)
    def _():
        out_ref[...] = h_ref[...] + acc_sc[...]


def _conv_ffn(h, g, wu, wg, cw, cb, wd, state, *, tm, tf, shift, tiles_per_seq, tail_end):
    m = h.shape[0]
    d0 = state.shape[1]
    n_tiles, n_chunks = m // tm, D_FF // tf
    row = lambda i, c: (i, 0)
    return pl.pallas_call(
        functools.partial(_ffn_kernel, shift=shift, d0=d0, tiles_per_seq=tiles_per_seq, tail_end=tail_end),
        grid=(n_tiles, n_chunks),
        in_specs=[
            pl.BlockSpec((tm, D_MODEL), row),
            pl.BlockSpec((1, D_MODEL), lambda i, c: (0, 0)),
            pl.BlockSpec((D_MODEL, tf), lambda i, c: (0, c)),
            pl.BlockSpec((D_MODEL, tf), lambda i, c: (0, c)),
            pl.BlockSpec((FFN_CONV, tf), lambda i, c: (0, c)),
            pl.BlockSpec((1, tf), lambda i, c: (0, c)),
            pl.BlockSpec((tf, D_MODEL), lambda i, c: (c, 0)),
            pl.BlockSpec((1, d0, tf), lambda i, c: (i // tiles_per_seq, 0, c)),
        ],
        out_specs=[
            pl.BlockSpec((tm, D_MODEL), row),
            pl.BlockSpec((1, d0, tf), lambda i, c: (i, 0, c)),
        ],
        out_shape=[jax.ShapeDtypeStruct((m, D_MODEL), F32),
                   jax.ShapeDtypeStruct((n_tiles, d0, D_FF), F32)],
        scratch_shapes=[
            pltpu.VMEM((tm, D_MODEL), BF16),
            pltpu.VMEM((tm, D_MODEL), F32),
            pltpu.VMEM((d0 + tm, tf), F32),
            pltpu.VMEM((n_chunks, d0, tf), F32),
        ],
        compiler_params=pltpu.CompilerParams(
            dimension_semantics=("arbitrary", "arbitrary"), vmem_limit_bytes=VMEM_LIMIT),
        name="conv_ffn",
    )(h, g, wu, wg, cw, cb, wd, state)


LRU_BLOCKS_PER_STEP = 2

def _scan_rows(a_sc, b_sc, hc, *, shift):
    tm = a_sc.shape[0]
    ngroups = tm // SUBLANES
    row = lax.broadcasted_iota(I32, (SUBLANES, a_sc.shape[1]), 0)

    def body(gi, hc):
        r0 = gi * SUBLANES
        a = a_sc[pl.ds(r0, SUBLANES), :]
        b = b_sc[pl.ds(r0, SUBLANES), :]
        if shift == 1:
            for s in (1, 2, 4):
                keep = row >= s
                a_sh = jnp.where(keep, pltpu.roll(a, s, 0), 1.0)
                b_sh = jnp.where(keep, pltpu.roll(b, s, 0), 0.0)
                b = a * b_sh + b
                a = a * a_sh
        hrows = a * hc + b
        b_sc[pl.ds(r0, SUBLANES), :] = hrows
        if shift == 1:
            return jnp.broadcast_to(hrows[SUBLANES - 1:SUBLANES, :], hrows.shape)
        return hrows

    for gi in range(ngroups):
        hc = body(gi, hc)
    return hc


def _lru_kernel(h_ref, g_ref, wx_ref, wy_ref, cw_ref, cb_ref, wa_ref, ba_ref, wi_ref, bi_ref,
                lam_ref, wo_ref, h0_ref, st_ref, out_ref, htail_ref, utail_ref,
                xn_sc, acc_sc, ext_sc, a_sc, b_sc, gate_sc, carry_u, carry_h, *,
                shift, d0, tiles_per_seq, tail_end):
    i = pl.program_id(0)
    c = pl.program_id(1)
    first = (i % tiles_per_seq) == 0

    @pl.when(c == 0)
    def _():
        xn_sc[...] = _rms_rows(h_ref[...], g_ref[...]).astype(BF16)
        acc_sc[...] = jnp.zeros(acc_sc.shape, F32)

    xn = xn_sc[...]
    u = jnp.dot(xn, wx_ref[...], preferred_element_type=F32)
    conv = _conv_rows(ext_sc, u, st_ref, carry_u, c, first, cw_ref, cb_ref, utail_ref,
                      width=LRU_CONV, shift=shift, d0=d0, tail_end=tail_end)
    lam = lam_ref[...]
    log_sig = jnp.minimum(lam, 0.0) - jnp.log(1.0 + jnp.exp(-jnp.abs(lam)))
    for s in range(wa_ref.shape[0]):
        sl = slice(s * RNN_BD, (s + 1) * RNN_BD)
        cs = conv[:, sl]
        cbf = cs.astype(BF16)
        r = _sigmoid(jnp.dot(cbf, wa_ref[s], preferred_element_type=F32) + ba_ref[:, sl])
        ig = _sigmoid(jnp.dot(cbf, wi_ref[s], preferred_element_type=F32) + bi_ref[:, sl])
        log_a = LRU_C * r * log_sig[:, sl]
        a = jnp.exp(log_a)
        a_sc[:, sl] = a
        b_sc[:, sl] = jnp.sqrt(-jnp.tanh(log_a) * (a * a + 1.0)) * (ig * cs)
    gate_sc[...] = _gelu(jnp.dot(xn, wy_ref[...], preferred_element_type=F32))

    carry_h[c] = _scan_rows(a_sc, b_sc, jnp.where(first, h0_ref[0], carry_h[c]), shift=shift)
    htail_ref[0] = b_sc[tail_end - SUBLANES:tail_end, :]
    tm = a_sc.shape[0]
    nsplit = 4 if tm % (4 * 2 * SUBLANES) == 0 else 1
    for rc in range(nsplit):
        rows = slice(rc * (tm // nsplit), (rc + 1) * (tm // nsplit))
        acc_sc[rows, :] += jnp.dot((b_sc[rows, :] * gate_sc[rows, :]).astype(BF16), wo_ref[...],
                                   preferred_element_type=F32)

    @pl.when(c == pl.num_programs(1) - 1)
    def _():
        out_ref[...] = h_ref[...] + acc_sc[...]


def _lru_block(h, g, wx, wy, cw, cb, wa, ba, wi, bi, lam, wo, h0, state, *,
               tm, shift, tiles_per_seq, tail_end):
    m = h.shape[0]
    d0 = state.shape[1]
    n_tiles = m // tm
    nsub = LRU_BLOCKS_PER_STEP
    bd = nsub * RNN_BD
    nsteps = RNN_BLOCKS // nsub
    row = lambda i, c: (i, 0)
    col = lambda i, c: (0, c)
    return pl.pallas_call(
        functools.partial(_lru_kernel, shift=shift, d0=d0, tiles_per_seq=tiles_per_seq, tail_end=tail_end),
        grid=(n_tiles, nsteps),
        in_specs=[
            pl.BlockSpec((tm, D_MODEL), row),
            pl.BlockSpec((1, D_MODEL), lambda i, c: (0, 0)),
            pl.BlockSpec((D_MODEL, bd), col),
            pl.BlockSpec((D_MODEL, bd), col),
            pl.BlockSpec((LRU_CONV, bd), col),
            pl.BlockSpec((1, bd), col),
            pl.BlockSpec((nsub, RNN_BD, RNN_BD), lambda i, c: (c, 0, 0)),
            pl.BlockSpec((1, bd), col),
            pl.BlockSpec((nsub, RNN_BD, RNN_BD), lambda i, c: (c, 0, 0)),
            pl.BlockSpec((1, bd), col),
            pl.BlockSpec((1, bd), col),
            pl.BlockSpec((bd, D_MODEL), lambda i, c: (c, 0)),
            pl.BlockSpec((1, SUBLANES, bd), lambda i, c: (i // tiles_per_seq, 0, c)),
            pl.BlockSpec((1, d0, bd), lambda i, c: (i // tiles_per_seq, 0, c)),
        ],
        out_specs=[
            pl.BlockSpec((tm, D_MODEL), row),
            pl.BlockSpec((1, SUBLANES, bd), lambda i, c: (i, 0, c)),
            pl.BlockSpec((1, d0, bd), lambda i, c: (i, 0, c)),
        ],
        out_shape=[jax.ShapeDtypeStruct((m, D_MODEL), F32),
                   jax.ShapeDtypeStruct((n_tiles, SUBLANES, D_MODEL), F32),
                   jax.ShapeDtypeStruct((n_tiles, d0, D_MODEL), F32)],
        scratch_shapes=[
            pltpu.VMEM((tm, D_MODEL), BF16),
            pltpu.VMEM((tm, D_MODEL), F32),
            pltpu.VMEM((d0 + tm, bd), F32),
            pltpu.VMEM((tm, bd), F32),
            pltpu.VMEM((tm, bd), F32),
            pltpu.VMEM((tm, bd), F32),
            pltpu.VMEM((nsteps, d0, bd), F32),
            pltpu.VMEM((nsteps, SUBLANES, bd), F32),
        ],
        compiler_params=pltpu.CompilerParams(
            dimension_semantics=("arbitrary", "arbitrary"), vmem_limit_bytes=VMEM_LIMIT),
        name="rg_lru",
    )(h, g, wx, wy, cw, cb, wa, ba, wi, bi, lam, wo, h0, state)


def _rope_tables(pos, rot, period):
    half = rot // 2
    inv = 1.0 / (ROPE_THETA ** (jnp.arange(half, dtype=F32) * (2.0 / rot)))
    ang = pos.astype(F32)[:, None] * inv[None, :]
    lane = jnp.arange(LANES) % period
    cos = jnp.cos(ang)[:, lane % half]
    sin = jnp.sin(ang)[:, lane % half]
    return jnp.stack([
        jnp.where(lane < rot, cos, 1.0),
        jnp.where(lane < half, -sin, 0.0),
        jnp.where((lane >= half) & (lane < rot), sin, 0.0),
    ])


def _time_major(x):
    return jnp.swapaxes(x, 0, 1)


def kernel(x_prompt, x_sample, cache_k, cache_v, cache_idx_k, state_lru_h, state_lru_conv, state_ffn_conv, page_table, meta_tokens, norm1_g, norm2_g, attn_w_in, attn_q_norm_g, attn_k_norm_g, attn_idx_k_norm_g, attn_w_out, lru_w_x, lru_w_y, lru_conv_w, lru_conv_b, lru_wa, lru_ba, lru_wi, lru_bi, lru_lambda, lru_w_out, ffn_w_up, ffn_w_gate, ffn_dw_w, ffn_dw_b, ffn_w_down):
    batch, seq, _ = x_prompt.shape
    nb, ns, _ = x_sample.shape
    depth = norm1_g.shape[0]
    t_real = seq + N_META
    tp = _round_up(t_real, QB)
    n_pages = page_table.shape[1]
    past = n_pages * PAGE
    topk_p = min(TOPK_MAX, t_real // 4)
    topk_s = min(TOPK_MAX, (past + ns) // 4)
    assert nb == SUBLANES and ns <= SUBLANES and n_pages % S1_PAGES == 0 and n_pages % S2_PAGES == 0

    tiles_per_seq = 6
    tm = tp // tiles_per_seq
    assert tm * tiles_per_seq == tp and tm % (2 * SUBLANES) == 0
    tail_end = t_real - (tiles_per_seq - 1) * tm
    assert tail_end % SUBLANES == 0 and tail_end >= SUBLANES
    tm_proj = 3 * LANES
    assert tp % tm_proj == 0
    ms = nb * ns

    hp = jnp.concatenate([
        jnp.broadcast_to(meta_tokens[None], (batch, N_META, D_MODEL)), x_prompt,
        jnp.zeros((batch, tp - t_real, D_MODEL), F32)], axis=1).reshape(batch * tp, D_MODEL)
    hs = _time_major(x_sample).reshape(ms, D_MODEL)

    rq_p = _rope_tables(jnp.arange(tp), ROT, HEAD_DIM)
    ri_p = _rope_tables(jnp.arange(tp), IDX_ROT, IDX_DIM)
    pos_s = jnp.repeat(past + jnp.arange(ns), nb)
    rq_s = _rope_tables(pos_s, ROT, HEAD_DIM)
    ri_s = _rope_tables(pos_s, IDX_ROT, IDX_DIM)

    cache_k4 = cache_k.reshape(cache_k.shape[:2] + (PAGE * N_KV, HEAD_DIM))
    cache_v4 = cache_v.reshape(cache_v.shape[:2] + (PAGE * N_KV, HEAD_DIM))

    kp, vp, ikp, ksl, vsl, iksl = [], [], [], [], [], []
    hpl, cpl, hsl, csl, fpl, fsl = [], [], [], [], [], []
    last_tiles = jnp.arange(batch) * tiles_per_seq + tiles_per_seq - 1

    for layer in range(depth):
        mi = layer // 2
        g1 = norm1_g[layer][None]
        if layer % 2 == 0:
            w_in = jnp.pad(attn_w_in[mi], ((0, 0), (0, PROJ_WP - PROJ_W))).astype(BF16)
            w_out = attn_w_out[mi].astype(BF16)
            qg = attn_q_norm_g[mi][None]
            kg = attn_k_norm_g[mi][None]
            ikg = jnp.pad(attn_idx_k_norm_g[mi], (0, LANES - IDX_DIM))[None]

            q, kf, kb, vf, vb, iq, ikf, ika, ikb, iw = _attn_project(
                hp, g1, w_in, qg, kg, ikg, rq_p, ri_p, tm=tm_proj, nseq=batch, rows_real=t_real)
            logit_bound = (1.02 * HEAD_DIM ** 0.5) * jnp.max(jnp.abs(qg)) * jnp.max(jnp.abs(kg))
            hp = _attn_prompt(hp, q, iq, iw, kb, vb, ika, ikb, w_out, logit_bound,
                              batch=batch, tp=tp, topk=topk_p)
            kp.append(kf.reshape(batch, t_real, N_KV, HEAD_DIM))
            vp.append(vf.reshape(batch, t_real, N_KV, HEAD_DIM))
            ikp.append(ikf)

            q, kf, kb, vf, vb, iq, ikf, ika, ikb, iw = _attn_project(
                hs, g1, w_in, qg, kg, ikg, rq_s, ri_s, tm=ms, nseq=1, rows_real=ms)
            ksl.append(_time_major(kf.reshape(ns, nb, N_KV, HEAD_DIM)))
            vsl.append(_time_major(vf.reshape(ns, nb, N_KV, HEAD_DIM)))
            iksl.append(_time_major(ikf.reshape(ns, nb, IDX_DIM)))

            pad_q = lambda x: jnp.pad(x, [(0, 0)] * (x.ndim - 2) + [(0, SUBLANES - ns), (0, 0)])
            iq_b = pad_q(iq.reshape(ns, nb, IDX_HEADS, IDX_DIM).transpose(1, 2, 0, 3))
            iq_b = iq_b.reshape(nb, IDX_HEADS * SUBLANES, IDX_DIM)
            iw_b = jnp.pad(iw[:, :IDX_HEADS].reshape(ns, nb, IDX_HEADS).transpose(1, 2, 0), ((0, 0), (0, 0), (0, SUBLANES - ns)))
            iww = jnp.broadcast_to(iw_b.reshape(nb, IDX_HEADS * SUBLANES, 1), (nb, IDX_HEADS * SUBLANES, LANES))
            new_page = lambda x: jnp.pad(_time_major(x.reshape(ns, nb, -1)), ((0, 0), (0, PAGE - ns), (0, 0)))
            iknew = new_page(ika[:, :IDX_DIM])
            keys, thr = _sample_index(page_table, cache_idx_k, mi, iq_b, iww, iknew, n_new=ns, topk=topk_s)

            q_b = pad_q(q.reshape(ns, nb, N_KV, GROUP, HEAD_DIM).transpose(1, 2, 3, 0, 4))
            qbd = q_b[:, :, :, :, None, :] * jnp.eye(N_KV, dtype=BF16)[None, :, None, None, :, None]
            qbd = qbd.reshape(nb, N_HEADS * SUBLANES, KV_W)
            o = _sample_attend(page_table, cache_k4, cache_v4, mi, qbd, keys, thr,
                               new_page(kb), new_page(vb), n_new=ns)
            o = o.reshape(nb, N_KV, GROUP, SUBLANES, N_KV, HEAD_DIM)
            o = jnp.stack([o[:, n, :, :ns, n, :] for n in range(N_KV)], axis=1)
            o = o.transpose(3, 0, 1, 2, 4).reshape(ms, Q_W).astype(BF16)
            hs = _matmul_res(hs, o, w_out, tn=512)
        else:
            lw = dict(
                g=g1, wx=lru_w_x[mi].astype(BF16), wy=lru_w_y[mi].astype(BF16),
                cw=lru_conv_w[mi], cb=lru_conv_b[mi][None],
                wa=lru_wa[mi].astype(BF16), ba=lru_ba[mi][None],
                wi=lru_wi[mi].astype(BF16), bi=lru_bi[mi][None],
                lam=lru_lambda[mi][None], wo=lru_w_out[mi].astype(BF16))
            hp, htail, utail = _lru_block(
                hp, h0=jnp.zeros((batch, SUBLANES, D_MODEL), F32),
                state=jnp.zeros((batch, SUBLANES, D_MODEL), F32),
                tm=tm, shift=1, tiles_per_seq=tiles_per_seq, tail_end=tail_end, **lw)
            hpl.append(htail[last_tiles, SUBLANES - 1])
            cpl.append(utail[last_tiles, SUBLANES - (LRU_CONV - 1):])
            d0s = (LRU_CONV - 1) * nb
            hs, htail, utail = _lru_block(
                hs, h0=state_lru_h[mi][None],
                state=_time_major(state_lru_conv[mi]).reshape(1, d0s, D_MODEL),
                tm=ms, shift=nb, tiles_per_seq=1, tail_end=ms, **lw)
            hsl.append(htail[0])
            csl.append(_time_major(utail.reshape(LRU_CONV - 1, nb, D_MODEL)))

        fw = dict(g=norm2_g[layer][None], wu=ffn_w_up[layer].astype(BF16), wg=ffn_w_gate[layer].astype(BF16),
                  cw=ffn_dw_w[layer], cb=ffn_dw_b[layer][None], wd=ffn_w_down[layer].astype(BF16))
        hp, tail = _conv_ffn(hp, state=jnp.zeros((batch, SUBLANES, D_FF), F32),
                             tm=tm, tf=512, shift=1, tiles_per_seq=tiles_per_seq, tail_end=tail_end, **fw)
        fpl.append(tail[last_tiles, SUBLANES - (FFN_CONV - 1):])
        d0s = (FFN_CONV - 1) * nb
        hs, tail = _conv_ffn(hs, state=_time_major(state_ffn_conv[layer]).reshape(1, d0s, D_FF),
                             tm=ms, tf=512, shift=nb, tiles_per_seq=1, tail_end=ms, **fw)
        fsl.append(_time_major(tail.reshape(FFN_CONV - 1, nb, D_FF)))

    y_prompt = hp.reshape(batch, tp, D_MODEL)[:, N_META:t_real]
    y_sample = _time_major(hs.reshape(ns, nb, D_MODEL))
    return (y_prompt, y_sample, jnp.stack(kp), jnp.stack(vp), jnp.stack(ikp),
            jnp.stack(ksl), jnp.stack(vsl), jnp.stack(iksl),
            jnp.stack(hpl), jnp.stack(cpl), jnp.stack(hsl), jnp.stack(csl),
            jnp.stack(fpl), jnp.stack(fsl))
```

```python
import functools

import jax
import jax.numpy as jnp
from jax import lax
from jax.experimental import pallas as pl
from jax.experimental.pallas import tpu as pltpu

F32 = jnp.float32
BF16 = jnp.bfloat16
I32 = jnp.int32

D_MODEL = 2048
N_META = 16
N_HEADS = 16
HEAD_DIM = 128
N_KV = 4
GROUP = N_HEADS // N_KV
ROT = HEAD_DIM // 4
IDX_HEADS = 16
IDX_DIM = 64
IDX_ROT = IDX_DIM // 4
ROPE_THETA = 500000.0
TOPK_MAX = 256
PAGE = 128
Q_W = N_HEADS * HEAD_DIM
KV_W = N_KV * HEAD_DIM
IQ_W = IDX_HEADS * IDX_DIM
PROJ_W = Q_W + 2 * KV_W + IQ_W + IDX_DIM + IDX_HEADS
D_FF = 3 * D_MODEL
FFN_CONV = 3
LRU_CONV = 4
LRU_C = 8.0
RNN_BLOCKS = 8
RNN_BD = D_MODEL // RNN_BLOCKS
EPS = 1e-6
NEG = -1e30

LANES = 128
SUBLANES = 8
QB = 128
MXU_COLS = 256
PROJ_WP = -(-PROJ_W // LANES) * LANES
VMEM_LIMIT = 56 * 1024 * 1024

_NT = (((1,), (1,)), ((), ()))
_INT_MIN = -2 ** 31


def _round_up(x, m):
    return -(-x // m) * m


def _rms_rows(x, g):
    ms = jnp.mean(x * x, axis=-1, keepdims=True)
    return x * lax.rsqrt(ms + EPS) * g


def _gelu(x):
    return 0.5 * x * (1.0 + jnp.tanh(0.7978845608028654 * (x + 0.044715 * (x * x * x))))


def _sigmoid(x):
    return 0.5 * jnp.tanh(0.5 * x) + 0.5


def _sort_key(x):
    b = lax.bitcast_convert_type(x + 0.0, I32)
    return b ^ ((b >> 31) & 0x7FFFFFFF)


def _kth_largest_key(count_ge, shape, k):
    def bit_body(bi, t):
        cand = t | jnp.left_shift(jnp.int32(1), 31 - bi)
        cnt = count_ge(cand ^ _INT_MIN)
        return jnp.where(cnt >= k, cand, t)
    t = lax.fori_loop(0, 32, bit_body, jnp.zeros(shape, I32))
    return t ^ _INT_MIN


def _rope(x, tab_ref, sh):
    return (x * tab_ref[0] + pltpu.roll(x, LANES - sh, 1) * tab_ref[1]
            + pltpu.roll(x, sh, 1) * tab_ref[2])


def _proj_kernel(h_ref, g_ref, w_ref, qg_ref, kg_ref, ikg_ref, rq_ref, ri_ref,
                 q_ref, kf_ref, kb_ref, vf_ref, vb_ref, iq_ref, ikf_ref, ika_ref, ikb_ref, iw_ref):
    xn = _rms_rows(h_ref[...], g_ref[...]).astype(BF16)

    def zcols(lo, width=MXU_COLS):
        return lax.dot_general(xn, w_ref[lo:lo + width, :], _NT, preferred_element_type=F32)

    tm = h_ref.shape[0]
    halves = [slice(s * LANES, (s + 1) * LANES) for s in range(MXU_COLS // LANES)]
    for j in range(Q_W // MXU_COLS):
        z = zcols(j * MXU_COLS)
        for sl in halves:
            x = _rope(_rms_rows(z[:, sl], qg_ref[...]), rq_ref, ROT // 2)
            q_ref[:, j * MXU_COLS + sl.start:j * MXU_COLS + sl.stop] = x.astype(BF16)
    for j in range(KV_W // MXU_COLS):
        z = zcols(Q_W + j * MXU_COLS)
        for s, sl in enumerate(halves):
            n = j * len(halves) + s
            x = _rope(_rms_rows(z[:, sl], kg_ref[...]), rq_ref, ROT // 2)
            kf_ref[0, pl.ds(n, tm, stride=N_KV), :] = x
            kb_ref[:, n * HEAD_DIM:(n + 1) * HEAD_DIM] = x.astype(BF16)
        z = zcols(Q_W + KV_W + j * MXU_COLS)
        for s, sl in enumerate(halves):
            n = j * len(halves) + s
            vf_ref[0, pl.ds(n, tm, stride=N_KV), :] = z[:, sl]
        vb_ref[:, j * MXU_COLS:(j + 1) * MXU_COLS] = z.astype(BF16)
    for j in range(IQ_W // MXU_COLS):
        z = zcols(Q_W + 2 * KV_W + j * MXU_COLS)
        for sl in halves:
            x = _rope(z[:, sl], ri_ref, IDX_ROT // 2)
            iq_ref[:, j * MXU_COLS + sl.start:j * MXU_COLS + sl.stop] = x.astype(BF16)

    x = zcols(Q_W + 2 * KV_W + IQ_W, LANES)
    lane = lax.broadcasted_iota(I32, x.shape, 1)
    ms = jnp.sum(jnp.where(lane < IDX_DIM, x * x, 0.0), axis=-1, keepdims=True) * (1.0 / IDX_DIM)
    y = _rope(x * lax.rsqrt(ms + EPS) * ikg_ref[...], ri_ref, IDX_ROT // 2)
    ikf_ref[0] = y[:, :IDX_DIM]
    ika_ref[...] = y.astype(BF16)
    ikb_ref[...] = pltpu.roll(y, IDX_DIM, 1).astype(BF16)
    iw_ref[...] = jnp.where(lane < IDX_HEADS, pltpu.roll(x, LANES - IDX_DIM, 1) * (IDX_HEADS ** -0.5), 0.0)


def _attn_project(h, g, w_t, layer, qg, kg, ikg, rq, ri, *, tm, nseq, rows_real):
    m = h.shape[0]
    tab_tiles = rq.shape[1] // tm
    assert m == nseq * tab_tiles * tm
    row = lambda i: (i, 0)
    const = lambda i: (0, 0)
    tab = lambda i: (0, i % tab_tiles, 0)
    seq_row = lambda i: (i // tab_tiles, i % tab_tiles, 0)
    sds = jax.ShapeDtypeStruct
    return pl.pallas_call(
        _proj_kernel,
        grid=(m // tm,),
        in_specs=[
            pl.BlockSpec((tm, D_MODEL), row),
            pl.BlockSpec((1, D_MODEL), const),
            pl.BlockSpec((None,) + w_t.shape[1:], lambda i: (layer, 0, 0), pipeline_mode=pl.Buffered(1)),
            pl.BlockSpec((1, LANES), const),
            pl.BlockSpec((1, LANES), const),
            pl.BlockSpec((1, LANES), const),
            pl.BlockSpec((3, tm, LANES), tab),
            pl.BlockSpec((3, tm, LANES), tab),
        ],
        out_specs=[
            pl.BlockSpec((tm, Q_W), row),
            pl.BlockSpec((1, tm * N_KV, HEAD_DIM), seq_row),
            pl.BlockSpec((tm, KV_W), row),
            pl.BlockSpec((1, tm * N_KV, HEAD_DIM), seq_row),
            pl.BlockSpec((tm, KV_W), row),
            pl.BlockSpec((tm, IQ_W), row),
            pl.BlockSpec((1, tm, IDX_DIM), seq_row),
            pl.BlockSpec((tm, LANES), row),
            pl.BlockSpec((tm, LANES), row),
            pl.BlockSpec((tm, LANES), row),
        ],
        out_shape=[
            sds((m, Q_W), BF16), sds((nseq, rows_real * N_KV, HEAD_DIM), F32), sds((m, KV_W), BF16),
            sds((nseq, rows_real * N_KV, HEAD_DIM), F32), sds((m, KV_W), BF16), sds((m, IQ_W), BF16),
            sds((nseq, rows_real, IDX_DIM), F32), sds((m, LANES), BF16), sds((m, LANES), BF16),
            sds((m, LANES), F32),
        ],
        compiler_params=pltpu.CompilerParams(
            dimension_semantics=("arbitrary",), vmem_limit_bytes=VMEM_LIMIT),
        name="attn_project",
    )(h, g, w_t, qg, kg, ikg, rq, ri)


def _attn_prompt_kernel(q_ref, iq_ref, iw_ref, k_ref, v_ref, ika_ref, ikb_ref, wout_ref, h_ref,
                        out_ref, keys_sc, o_sc, m_sc, l_sc, acc_sc, *, kc, topk):
    i = pl.program_id(1)
    q0 = i * QB
    nch = (q0 + QB + kc - 1) // kc
    qpos = q0 + lax.broadcasted_iota(I32, (QB, 1), 0)
    lane = lax.broadcasted_iota(I32, (QB, kc), 1)
    iw = iw_ref[...]

    def score_chunk(c, carry):
        k0 = pl.multiple_of(c * kc, LANES)
        ka = ika_ref[pl.ds(k0, kc), :]
        kb = ikb_ref[pl.ds(k0, kc), :]
        sc = jnp.zeros((QB, kc), F32)
        for p in range(IDX_HEADS // 2):
            a = iq_ref[:, p * LANES:(p + 1) * LANES]
            sa = lax.dot_general(a, ka, _NT, preferred_element_type=F32)
            sb = lax.dot_general(a, kb, _NT, preferred_element_type=F32)
            sc = sc + iw[:, 2 * p:2 * p + 1] * jnp.maximum(sa, 0.0)
            sc = sc + iw[:, 2 * p + 1:2 * p + 2] * jnp.maximum(sb, 0.0)
        sc = sc * (IDX_DIM ** -0.5)
        sc = jnp.where(k0 + lane <= qpos, sc, NEG)
        keys_sc[:, pl.ds(k0, kc)] = _sort_key(sc)
        return carry

    lax.fori_loop(0, nch, score_chunk, 0)

    def count_ge(cand):
        def body(c, acc):
            k0 = pl.multiple_of(c * kc, LANES)
            kk = keys_sc[:, pl.ds(k0, kc)]
            return acc + jnp.sum(jnp.where(kk >= cand, 1.0, 0.0), axis=1, keepdims=True)
        return lax.fori_loop(0, nch, body, jnp.zeros((QB, 1), F32))

    thr = _kth_largest_key(count_ge, (QB, 1), float(topk))

    scale = HEAD_DIM ** -0.5
    for n in range(N_KV):
        qs = jnp.concatenate(
            [q_ref[:, (n * GROUP + g) * LANES:(n * GROUP + g + 1) * LANES] for g in range(GROUP)], axis=0)
        m_sc[...] = jnp.full(m_sc.shape, NEG, F32)
        l_sc[...] = jnp.zeros(l_sc.shape, F32)
        acc_sc[...] = jnp.zeros(acc_sc.shape, F32)

        def att_chunk(c, carry, n=n, qs=qs):
            k0 = pl.multiple_of(c * kc, LANES)
            kk = keys_sc[:, pl.ds(k0, kc)]
            sel = (kk >= thr) & (k0 + lane <= qpos)
            bias = jnp.where(sel, 0.0, NEG)
            bias = jnp.concatenate([bias] * GROUP, axis=0)
            kt = k_ref[pl.ds(k0, kc), n * LANES:(n + 1) * LANES]
            vt = v_ref[pl.ds(k0, kc), n * LANES:(n + 1) * LANES]
            s = lax.dot_general(qs, kt, _NT, preferred_element_type=F32) * scale + bias
            m_old = m_sc[...]
            m_new = jnp.maximum(m_old, jnp.max(s, axis=1, keepdims=True))
            alpha = jnp.exp(m_old - m_new)
            p = jnp.exp(s - m_new)
            l_sc[...] = alpha * l_sc[...] + jnp.sum(p, axis=1, keepdims=True)
            acc_sc[...] = alpha * acc_sc[...] + jnp.dot(p.astype(BF16), vt, preferred_element_type=F32)
            m_sc[...] = m_new
            return carry

        lax.fori_loop(0, nch, att_chunk, 0)
        o = acc_sc[...] / l_sc[...]
        for g in range(GROUP):
            hh = n * GROUP + g
            o_sc[:, hh * LANES:(hh + 1) * LANES] = o[g * QB:(g + 1) * QB].astype(BF16)

    out_ref[...] = h_ref[...] + jnp.dot(o_sc[...], wout_ref[...], preferred_element_type=F32)


def _attn_prompt_bounded_kernel(shift_ref, q_ref, iq_ref, iw_ref, k_ref, v_ref, ika_ref, ikb_ref,
                                wout_ref, h_ref, out_ref, keys_sc, iqt_sc, o_sc, acc_sc, *, kc, topk):
    i = pl.program_id(1)
    q0 = i * QB
    tp = k_ref.shape[0]
    nch = (q0 + QB + kc - 1) // kc
    nlt = kc // LANES
    qcol = q0 + lax.broadcasted_iota(I32, (LANES, QB), 1)
    krow = lax.broadcasted_iota(I32, (LANES, QB), 0)
    for p in range(IDX_HEADS // 2):
        sl = slice(p * LANES, (p + 1) * LANES)
        iqt_sc[:, sl] = iq_ref[:, sl].astype(F32).T.astype(BF16)
    iwt = iw_ref[...].T
    if keys_sc.shape[0] > tp:
        keys_sc[tp:, :] = jnp.full((keys_sc.shape[0] - tp, QB), _INT_MIN, I32)

    def chunk_start(c):
        return pl.multiple_of(jnp.minimum(c * kc, tp - kc), LANES)

    def score_chunk(c, carry):
        k0 = chunk_start(c)
        ka = ika_ref[pl.ds(k0, kc), :]
        kb = ikb_ref[pl.ds(k0, kc), :]
        sc = [jnp.zeros((LANES, QB), F32) for _ in range(nlt)]
        for j in range(IQ_W // MXU_COLS):
            w = iqt_sc[:, j * MXU_COLS:(j + 1) * MXU_COLS]
            sa = jnp.dot(ka, w, preferred_element_type=F32)
            sb = jnp.dot(kb, w, preferred_element_type=F32)
            for half in range(MXU_COLS // LANES):
                p = j * (MXU_COLS // LANES) + half
                cols = slice(half * LANES, (half + 1) * LANES)
                wa = iwt[2 * p:2 * p + 1, :]
                wb = iwt[2 * p + 1:2 * p + 2, :]
                for lt in range(nlt):
                    rows = slice(lt * LANES, (lt + 1) * LANES)
                    sc[lt] = (sc[lt] + wa * jnp.maximum(sa[rows, cols], 0.0)
                              + wb * jnp.maximum(sb[rows, cols], 0.0))
        for lt in range(nlt):
            s = jnp.where(k0 + lt * LANES + krow <= qcol, sc[lt] * (IDX_DIM ** -0.5), NEG)
            keys_sc[pl.ds(k0 + lt * LANES, LANES), :] = _sort_key(s)
        return carry

    lax.fori_loop(0, nch, score_chunk, 0)

    def count_ge(cand):
        def body(c, acc):
            k0 = pl.multiple_of(c * kc, LANES)
            acc = list(acc)
            for j in range(kc // SUBLANES):
                kk = keys_sc[pl.ds(k0 + j * SUBLANES, SUBLANES), :]
                acc[j % len(acc)] = acc[j % len(acc)] + jnp.where(kk >= cand, 1.0, 0.0)
            return tuple(acc)
        acc = lax.fori_loop(0, nch, body, tuple(jnp.zeros((SUBLANES, QB), F32) for _ in range(4)))
        tot = (acc[0] + acc[1]) + (acc[2] + acc[3])
        for s in (4, 2, 1):
            tot = tot + pltpu.roll(tot, s, 0)
        return tot

    thr = _kth_largest_key(count_ge, (SUBLANES, QB), float(topk))
    thr = jnp.broadcast_to(thr[0:1, :], (LANES, QB))

    scale = HEAD_DIM ** -0.5
    shift = shift_ref[0, 0]
    acc_sc[...] = jnp.zeros(acc_sc.shape, F32)
    ones_col = jnp.where(lax.broadcasted_iota(I32, (kc, LANES), 1) == 0, 1.0, 0.0).astype(BF16)

    def att_chunk(c, carry):
        k0 = chunk_start(c)
        bias = []
        for lt in range(nlt):
            kk = keys_sc[pl.ds(k0 + lt * LANES, LANES), :]
            kidx = k0 + lt * LANES + krow
            sel = (kk >= thr) & (kidx <= qcol) & (kidx >= c * kc)
            bias.append(jnp.where(sel, -shift, NEG).T)
        bias = jnp.concatenate(bias, axis=1)
        bias = jnp.concatenate([bias] * GROUP, axis=0)
        for n in range(N_KV):
            qs = jnp.concatenate(
                [q_ref[:, (n * GROUP + g) * LANES:(n * GROUP + g + 1) * LANES] for g in range(GROUP)], axis=0)
            kt = k_ref[pl.ds(k0, kc), n * LANES:(n + 1) * LANES]
            vt = jnp.concatenate([v_ref[pl.ds(k0, kc), n * LANES:(n + 1) * LANES], ones_col], axis=1)
            s = lax.dot_general(qs, kt, _NT, preferred_element_type=F32)
            p = jnp.exp(s * scale + bias).astype(BF16)
            acc_sc[n] += jnp.dot(p, vt, preferred_element_type=F32)
        return carry

    lax.fori_loop(0, nch, att_chunk, 0)
    for n in range(N_KV):
        a = acc_sc[n]
        o = a[:, :LANES] / a[:, LANES:LANES + 1]
        for g in range(GROUP):
            hh = n * GROUP + g
            o_sc[:, hh * LANES:(hh + 1) * LANES] = o[g * QB:(g + 1) * QB].astype(BF16)

    out_ref[...] = h_ref[...] + jnp.dot(o_sc[...], wout_ref[...], preferred_element_type=F32)


MAX_LOGIT_BOUND = 40.0


def _attn_prompt(h, q, iq, iw, kb, vb, ika, ikb, wout, layer, logit_bound, *, batch, tp, topk):
    args = (q, iq, iw, kb, vb, ika, ikb, wout, h)
    kc_online = 3 * LANES
    assert MXU_COLS >= topk and (tp - MXU_COLS) % LANES == 0 and kc_online >= topk and tp % kc_online == 0
    return lax.cond(
        logit_bound <= MAX_LOGIT_BOUND,
        lambda: _attn_prompt_bounded(logit_bound.reshape(1, 1), *args, layer=layer, batch=batch, tp=tp,
                                     kc=kc_online, topk=topk),
        lambda: _attn_prompt_online(*args, layer=layer, batch=batch, tp=tp, kc=kc_online, topk=topk))


def _attn_prompt_bounded(shift, q, iq, iw, kb, vb, ika, ikb, wout, h, *, layer, batch, tp, kc, topk):
    nqb = tp // QB
    qrow = lambda b, i: (b * nqb + i, 0)
    seq = lambda b, i: (b, 0)
    const = lambda b, i: (0, 0)
    return pl.pallas_call(
        functools.partial(_attn_prompt_bounded_kernel, kc=kc, topk=topk),
        grid=(batch, nqb),
        in_specs=[
            pl.BlockSpec(memory_space=pltpu.SMEM),
            pl.BlockSpec((QB, Q_W), qrow),
            pl.BlockSpec((QB, IQ_W), qrow),
            pl.BlockSpec((QB, LANES), qrow),
            pl.BlockSpec((tp, KV_W), seq),
            pl.BlockSpec((tp, KV_W), seq),
            pl.BlockSpec((tp, LANES), seq),
            pl.BlockSpec((tp, LANES), seq),
            pl.BlockSpec((None, Q_W, D_MODEL), lambda b, i: (layer, 0, 0)),
            pl.BlockSpec((QB, D_MODEL), qrow),
        ],
        out_specs=pl.BlockSpec((QB, D_MODEL), qrow),
        out_shape=jax.ShapeDtypeStruct(h.shape, F32),
        scratch_shapes=[
            pltpu.VMEM((_round_up(tp, kc), QB), I32),
            pltpu.VMEM((LANES, IQ_W), BF16),
            pltpu.VMEM((QB, Q_W), BF16),
            pltpu.VMEM((N_KV, GROUP * QB, 2 * HEAD_DIM), F32),
        ],
        compiler_params=pltpu.CompilerParams(
            dimension_semantics=("arbitrary", "arbitrary"), vmem_limit_bytes=VMEM_LIMIT),
        name="attn_prompt_bounded",
    )(shift, q, iq, iw, kb, vb, ika, ikb, wout, h)


def _attn_prompt_online(q, iq, iw, kb, vb, ika, ikb, wout, h, *, layer, batch, tp, kc, topk):
    nqb = tp // QB
    qrow = lambda b, i: (b * nqb + i, 0)
    seq = lambda b, i: (b, 0)
    const = lambda b, i: (0, 0)
    return pl.pallas_call(
        functools.partial(_attn_prompt_kernel, kc=kc, topk=topk),
        grid=(batch, nqb),
        in_specs=[
            pl.BlockSpec((QB, Q_W), qrow),
            pl.BlockSpec((QB, IQ_W), qrow),
            pl.BlockSpec((QB, LANES), qrow),
            pl.BlockSpec((tp, KV_W), seq),
            pl.BlockSpec((tp, KV_W), seq),
            pl.BlockSpec((tp, LANES), seq),
            pl.BlockSpec((tp, LANES), seq),
            pl.BlockSpec((None, Q_W, D_MODEL), lambda b, i: (layer, 0, 0)),
            pl.BlockSpec((QB, D_MODEL), qrow),
        ],
        out_specs=pl.BlockSpec((QB, D_MODEL), qrow),
        out_shape=jax.ShapeDtypeStruct(h.shape, F32),
        scratch_shapes=[
            pltpu.VMEM((QB, tp), I32),
            pltpu.VMEM((QB, Q_W), BF16),
            pltpu.VMEM((GROUP * QB, 1), F32),
            pltpu.VMEM((GROUP * QB, 1), F32),
            pltpu.VMEM((GROUP * QB, HEAD_DIM), F32),
        ],
        compiler_params=pltpu.CompilerParams(
            dimension_semantics=("arbitrary", "arbitrary"), vmem_limit_bytes=VMEM_LIMIT),
        name="attn_prompt",
    )(q, iq, iw, kb, vb, ika, ikb, wout, h)


S1_PAGES = 16
S2_PAGES = 8


def _idx_page_scores(iq, iww, page_t):
    s = jnp.dot(iq, page_t, preferred_element_type=F32)
    sc = jnp.zeros((SUBLANES, PAGE), F32)
    for hh in range(IDX_HEADS):
        sl = slice(hh * SUBLANES, (hh + 1) * SUBLANES)
        sc = sc + iww[sl, :] * jnp.maximum(s[sl, :], 0.0)
    return sc * (IDX_DIM ** -0.5)


def _sample_index_kernel(pt_ref, *refs, past, n_new, topk):
    pages = refs[:S1_PAGES]
    iq_ref, iww_ref, iknew_ref, keys_ref, thr_ref = refs[S1_PAGES:]
    s = pl.program_id(1)
    iq = iq_ref[0]
    iww = iww_ref[0]
    for r in range(S1_PAGES):
        sc = _idx_page_scores(iq, iww, pages[r][...].astype(BF16))
        off = pl.multiple_of((s * S1_PAGES + r) * PAGE, PAGE)
        keys_ref[0, :, pl.ds(off, PAGE)] = _sort_key(sc)

    @pl.when(s == pl.num_programs(1) - 1)
    def _():
        sc = _idx_page_scores(iq, iww, iknew_ref[0])
        t = lax.broadcasted_iota(I32, sc.shape, 0)
        jn = lax.broadcasted_iota(I32, sc.shape, 1)
        sc = jnp.where((jn <= t) & (jn < n_new), sc, NEG)
        keys_ref[0, :, pl.ds(past, PAGE)] = _sort_key(sc)
        def count_ge(cand):
            acc = [jnp.zeros((SUBLANES, LANES), F32) for _ in range(4)]
            for j in range(keys_ref.shape[2] // LANES):
                kk = keys_ref[0, :, j * LANES:(j + 1) * LANES]
                acc[j % 4] = acc[j % 4] + jnp.where(kk >= cand, 1.0, 0.0)
            tot = (acc[0] + acc[1]) + (acc[2] + acc[3])
            return jnp.broadcast_to(jnp.sum(tot, axis=1, keepdims=True), (SUBLANES, LANES))

        thr_ref[0] = _kth_largest_key(count_ge, (SUBLANES, LANES), float(topk))


def _sample_index(page_table, cache_ik, layer, iq, iww, iknew, *, n_new, topk):
    nb, n_pages = page_table.shape
    steps = n_pages // S1_PAGES
    nk = (n_pages + 1) * PAGE
    page_spec = lambda r: pl.BlockSpec(
        (None, None, IDX_DIM, PAGE), lambda b, s, pt: (layer, pt[b, s * S1_PAGES + r], 0, 0))
    per_b = lambda b, s, pt: (b, 0, 0)
    return pl.pallas_call(
        functools.partial(_sample_index_kernel, past=n_pages * PAGE, n_new=n_new, topk=topk),
        grid_spec=pltpu.PrefetchScalarGridSpec(
            num_scalar_prefetch=1,
            grid=(nb, steps),
            in_specs=[page_spec(r) for r in range(S1_PAGES)] + [
                pl.BlockSpec((1, IDX_HEADS * SUBLANES, IDX_DIM), per_b),
                pl.BlockSpec((1, IDX_HEADS * SUBLANES, LANES), per_b),
                pl.BlockSpec((1, IDX_DIM, PAGE), per_b),
            ],
            out_specs=[
                pl.BlockSpec((1, SUBLANES, nk), per_b),
                pl.BlockSpec((1, SUBLANES, LANES), per_b),
            ],
        ),
        out_shape=[jax.ShapeDtypeStruct((nb, SUBLANES, nk), I32),
                   jax.ShapeDtypeStruct((nb, SUBLANES, LANES), I32)],
        compiler_params=pltpu.CompilerParams(
            dimension_semantics=("arbitrary", "arbitrary"), vmem_limit_bytes=VMEM_LIMIT),
        name="sample_index",
    )(page_table, *([cache_ik] * S1_PAGES), iq, iww, iknew)


def _sample_attend_kernel(pt_ref, *refs, n_new):
    kpages = refs[:S2_PAGES]
    vpages = refs[S2_PAGES:2 * S2_PAGES]
    (qbd_ref, keys_ref, keysnew_ref, thr_ref, knew_ref, vnew_ref, o_ref,
     m_sc, l_sc, acc_sc) = refs[2 * S2_PAGES:]
    s = pl.program_id(1)
    scale = HEAD_DIM ** -0.5
    qbd = qbd_ref[0]
    thr = thr_ref[0]

    @pl.when(s == 0)
    def _():
        m_sc[...] = jnp.full(m_sc.shape, NEG, F32)
        l_sc[...] = jnp.zeros(l_sc.shape, F32)
        acc_sc[...] = jnp.zeros(acc_sc.shape, F32)

    def page_update(kp, vp, bias):
        bias = jnp.concatenate([bias] * (N_HEADS), axis=0)
        sc = lax.dot_general(qbd, kp, _NT, preferred_element_type=F32) * scale + bias
        m_old = m_sc[...]
        m_new = jnp.maximum(m_old, jnp.max(sc, axis=1, keepdims=True))
        alpha = jnp.exp(m_old - m_new)
        p = jnp.exp(sc - m_new)
        l_sc[...] = alpha * l_sc[...] + jnp.sum(p, axis=1, keepdims=True)
        acc_sc[...] = alpha * acc_sc[...] + jnp.dot(p.astype(BF16), vp, preferred_element_type=F32)
        m_sc[...] = m_new

    def page_rows(ref):
        return jnp.concatenate(
            [ref[pl.ds(n, PAGE, stride=N_KV), :] for n in range(N_KV)], axis=1).astype(BF16)

    bias = jnp.concatenate(
        [jnp.where(keys_ref[0, :, r * PAGE:(r + 1) * PAGE] >= thr, 0.0, NEG) for r in range(S2_PAGES)], axis=1)
    page_update(jnp.concatenate([page_rows(kpages[r]) for r in range(S2_PAGES)], axis=0),
                jnp.concatenate([page_rows(vpages[r]) for r in range(S2_PAGES)], axis=0), bias)

    @pl.when(s == pl.num_programs(1) - 1)
    def _():
        kk = keysnew_ref[0]
        t = lax.broadcasted_iota(I32, kk.shape, 0)
        jn = lax.broadcasted_iota(I32, kk.shape, 1)
        page_update(knew_ref[0], vnew_ref[0],
                    jnp.where((kk >= thr) & (jn <= t) & (jn < n_new), 0.0, NEG))
        o_ref[0] = acc_sc[...] / l_sc[...]


def _sample_attend(page_table, cache_k, cache_v, layer, qbd, keys, thr, knew, vnew, *, n_new):
    nb, n_pages = page_table.shape
    steps = n_pages // S2_PAGES
    page_spec = lambda r: pl.BlockSpec(
        (None, None, PAGE * N_KV, HEAD_DIM), lambda b, s, pt: (layer, pt[b, s * S2_PAGES + r], 0, 0))
    per_b = lambda b, s, pt: (b, 0, 0)
    rows = N_HEADS * SUBLANES
    return pl.pallas_call(
        functools.partial(_sample_attend_kernel, n_new=n_new),
        grid_spec=pltpu.PrefetchScalarGridSpec(
            num_scalar_prefetch=1,
            grid=(nb, steps),
            in_specs=[page_spec(r) for r in range(S2_PAGES)] * 2 + [
                pl.BlockSpec((1, rows, KV_W), per_b),
                pl.BlockSpec((1, SUBLANES, S2_PAGES * PAGE), lambda b, s, pt: (b, 0, s)),
                pl.BlockSpec((1, SUBLANES, PAGE), lambda b, s, pt: (b, 0, n_pages)),
                pl.BlockSpec((1, SUBLANES, LANES), per_b),
                pl.BlockSpec((1, PAGE, KV_W), per_b),
                pl.BlockSpec((1, PAGE, KV_W), per_b),
            ],
            out_specs=pl.BlockSpec((1, rows, KV_W), per_b),
            scratch_shapes=[
                pltpu.VMEM((rows, 1), F32),
                pltpu.VMEM((rows, 1), F32),
                pltpu.VMEM((rows, KV_W), F32),
            ],
        ),
        out_shape=jax.ShapeDtypeStruct((nb, rows, KV_W), F32),
        compiler_params=pltpu.CompilerParams(
            dimension_semantics=("arbitrary", "arbitrary"), vmem_limit_bytes=VMEM_LIMIT),
        name="sample_attend",
    )(page_table, *([cache_k] * S2_PAGES), *([cache_v] * S2_PAGES), qbd, keys, keys, thr, knew, vnew)


def _matmul_res_kernel(h_ref, x_ref, w_ref, o_ref):
    o_ref[...] = h_ref[...] + jnp.dot(x_ref[...], w_ref[...], preferred_element_type=F32)


def _matmul_res(h, x, w, layer, *, tn):
    m, n = h.shape
    kdim = x.shape[1]
    return pl.pallas_call(
        _matmul_res_kernel,
        grid=(n // tn,),
        in_specs=[pl.BlockSpec((m, tn), lambda j: (0, j)),
                  pl.BlockSpec((m, kdim), lambda j: (0, 0)),
                  pl.BlockSpec((None, kdim, tn), lambda j: (layer, 0, j))],
        out_specs=pl.BlockSpec((m, tn), lambda j: (0, j)),
        out_shape=jax.ShapeDtypeStruct((m, n), F32),
        compiler_params=pltpu.CompilerParams(dimension_semantics=("arbitrary",)),
        name="matmul_residual",
    )(h, x, w)


def _conv_rows(ext_sc, u, init_ref, carry_sc, c, first, cw_ref, cb_ref, tail_ref, *,
               width, shift, d0, tail_end):
    tm = u.shape[0]
    ext_sc[d0:d0 + tm, :] = u
    ext_sc[0:d0, :] = jnp.where(first, init_ref[0], carry_sc[c])

    y = cb_ref[...] + cw_ref[width - 1:width, :] * u
    for j in range(1, width):
        y = y + cw_ref[width - 1 - j:width - j, :] * ext_sc[d0 - j * shift:d0 - j * shift + tm, :]
    carry_sc[c] = ext_sc[tm:tm + d0, :]
    tail_ref[0] = ext_sc[tail_end:tail_end + d0, :]
    return y


def _ffn_kernel(h_ref, g_ref, wu_ref, wg_ref, cw_ref, cb_ref, wd_ref, st_ref, out_ref, tail_ref,
                xn_sc, acc_sc, ext_sc, carry_sc, *, shift, d0, tiles_per_seq, tail_end):
    i = pl.program_id(0)
    c = pl.program_id(1)

    @pl.when(c == 0)
    def _():
        xn_sc[...] = _rms_rows(h_ref[...], g_ref[...]).astype(BF16)
        acc_sc[...] = jnp.zeros(acc_sc.shape, F32)

    xn = xn_sc[...]
    u = jnp.dot(xn, wu_ref[...], preferred_element_type=F32)
    conv = _conv_rows(ext_sc, u, st_ref, carry_sc, c, (i % tiles_per_seq) == 0, cw_ref, cb_ref,
                      tail_ref, width=FFN_CONV, shift=shift, d0=d0, tail_end=tail_end)
    mid = _gelu(conv) * jnp.dot(xn, wg_ref[...], preferred_element_type=F32)
    acc_sc[...] += jnp.dot(mid.astype(BF16), wd_ref[...], preferred_element_type=F32)

    @pl.when(c == pl.num_programs(1) - 1)
    def _():
        out_ref[...] = h_ref[...] + acc_sc[...]


def _conv_ffn(h, g, wu, wg, cw, cb, wd, state, *, layer, tm, tf, shift, tiles_per_seq, tail_end):
    m = h.shape[0]
    d0 = state.shape[1]
    n_tiles, n_chunks = m // tm, D_FF // tf
    row = lambda i, c: (i, 0)
    return pl.pallas_call(
        functools.partial(_ffn_kernel, shift=shift, d0=d0, tiles_per_seq=tiles_per_seq, tail_end=tail_end),
        grid=(n_tiles, n_chunks),
        in_specs=[
            pl.BlockSpec((tm, D_MODEL), row),
            pl.BlockSpec((1, D_MODEL), lambda i, c: (0, 0)),
            pl.BlockSpec((None, D_MODEL, tf), lambda i, c: (layer, 0, c)),
            pl.BlockSpec((None, D_MODEL, tf), lambda i, c: (layer, 0, c)),
            pl.BlockSpec((FFN_CONV, tf), lambda i, c: (0, c)),
            pl.BlockSpec((1, tf), lambda i, c: (0, c)),
            pl.BlockSpec((None, tf, D_MODEL), lambda i, c: (layer, c, 0)),
            pl.BlockSpec((1, d0, tf), lambda i, c: (i // tiles_per_seq, 0, c)),
        ],
        out_specs=[
            pl.BlockSpec((tm, D_MODEL), row),
            pl.BlockSpec((1, d0, tf), lambda i, c: (i, 0, c)),
        ],
        out_shape=[jax.ShapeDtypeStruct((m, D_MODEL), F32),
                   jax.ShapeDtypeStruct((n_tiles, d0, D_FF), F32)],
        scratch_shapes=[
            pltpu.VMEM((tm, D_MODEL), BF16),
            pltpu.VMEM((tm, D_MODEL), F32),
            pltpu.VMEM((d0 + tm, tf), F32),
            pltpu.VMEM((n_chunks, d0, tf), F32),
        ],
        compiler_params=pltpu.CompilerParams(
            dimension_semantics=("arbitrary", "arbitrary"), vmem_limit_bytes=VMEM_LIMIT),
        name="conv_ffn",
    )(h, g, wu, wg, cw, cb, wd, state)


LRU_BLOCKS_PER_STEP = 2

def _scan_rows(a_sc, b_sc, hc, *, shift):
    tm = a_sc.shape[0]
    ngroups = tm // SUBLANES
    row = lax.broadcasted_iota(I32, (SUBLANES, a_sc.shape[1]), 0)

    def body(gi, hc):
        r0 = gi * SUBLANES
        a = a_sc[pl.ds(r0, SUBLANES), :]
        b = b_sc[pl.ds(r0, SUBLANES), :]
        if shift == 1:
            for s in (1, 2, 4):
                keep = row >= s
                a_sh = jnp.where(keep, pltpu.roll(a, s, 0), 1.0)
                b_sh = jnp.where(keep, pltpu.roll(b, s, 0), 0.0)
                b = a * b_sh + b
                a = a * a_sh
        hrows = a * hc + b
        b_sc[pl.ds(r0, SUBLANES), :] = hrows
        if shift == 1:
            return jnp.broadcast_to(hrows[SUBLANES - 1:SUBLANES, :], hrows.shape)
        return hrows

    for gi in range(ngroups):
        hc = body(gi, hc)
    return hc


def _lru_kernel(h_ref, g_ref, wx_ref, wy_ref, cw_ref, cb_ref, wa_ref, ba_ref, wi_ref, bi_ref,
                lam_ref, wo_ref, h0_ref, st_ref, out_ref, htail_ref, utail_ref,
                xn_sc, acc_sc, ext_sc, a_sc, b_sc, gate_sc, carry_u, carry_h, *,
                shift, d0, tiles_per_seq, tail_end):
    i = pl.program_id(0)
    c = pl.program_id(1)
    first = (i % tiles_per_seq) == 0

    @pl.when(c == 0)
    def _():
        xn_sc[...] = _rms_rows(h_ref[...], g_ref[...]).astype(BF16)
        acc_sc[...] = jnp.zeros(acc_sc.shape, F32)

    xn = xn_sc[...]
    u = jnp.dot(xn, wx_ref[...], preferred_element_type=F32)
    conv = _conv_rows(ext_sc, u, st_ref, carry_u, c, first, cw_ref, cb_ref, utail_ref,
                      width=LRU_CONV, shift=shift, d0=d0, tail_end=tail_end)
    lam = lam_ref[...]
    log_sig = jnp.minimum(lam, 0.0) - jnp.log(1.0 + jnp.exp(-jnp.abs(lam)))
    for s in range(wa_ref.shape[0]):
        sl = slice(s * RNN_BD, (s + 1) * RNN_BD)
        cs = conv[:, sl]
        cbf = cs.astype(BF16)
        r = _sigmoid(jnp.dot(cbf, wa_ref[s], preferred_element_type=F32) + ba_ref[:, sl])
        ig = _sigmoid(jnp.dot(cbf, wi_ref[s], preferred_element_type=F32) + bi_ref[:, sl])
        log_a = LRU_C * r * log_sig[:, sl]
        a = jnp.exp(log_a)
        a_sc[:, sl] = a
        b_sc[:, sl] = jnp.sqrt(-jnp.tanh(log_a) * (a * a + 1.0)) * (ig * cs)
    gate_sc[...] = _gelu(jnp.dot(xn, wy_ref[...], preferred_element_type=F32))

    carry_h[c] = _scan_rows(a_sc, b_sc, jnp.where(first, h0_ref[0], carry_h[c]), shift=shift)
    htail_ref[0] = b_sc[tail_end - SUBLANES:tail_end, :]
    tm = a_sc.shape[0]
    nsplit = 4 if tm % (4 * 2 * SUBLANES) == 0 else 1
    for rc in range(nsplit):
        rows = slice(rc * (tm // nsplit), (rc + 1) * (tm // nsplit))
        acc_sc[rows, :] += jnp.dot((b_sc[rows, :] * gate_sc[rows, :]).astype(BF16), wo_ref[...],
                                   preferred_element_type=F32)

    @pl.when(c == pl.num_programs(1) - 1)
    def _():
        out_ref[...] = h_ref[...] + acc_sc[...]


def _lru_block(h, g, wx, wy, cw, cb, wa, ba, wi, bi, lam, wo, h0, state, *,
               layer, tm, shift, tiles_per_seq, tail_end):
    m = h.shape[0]
    d0 = state.shape[1]
    n_tiles = m // tm
    nsub = LRU_BLOCKS_PER_STEP
    bd = nsub * RNN_BD
    nsteps = RNN_BLOCKS // nsub
    row = lambda i, c: (i, 0)
    col = lambda i, c: (0, c)
    return pl.pallas_call(
        functools.partial(_lru_kernel, shift=shift, d0=d0, tiles_per_seq=tiles_per_seq, tail_end=tail_end),
        grid=(n_tiles, nsteps),
        in_specs=[
            pl.BlockSpec((tm, D_MODEL), row),
            pl.BlockSpec((1, D_MODEL), lambda i, c: (0, 0)),
            pl.BlockSpec((None, D_MODEL, bd), lambda i, c: (layer, 0, c)),
            pl.BlockSpec((None, D_MODEL, bd), lambda i, c: (layer, 0, c)),
            pl.BlockSpec((LRU_CONV, bd), col),
            pl.BlockSpec((1, bd), col),
            pl.BlockSpec((None, nsub, RNN_BD, RNN_BD), lambda i, c: (layer, c, 0, 0)),
            pl.BlockSpec((1, bd), col),
            pl.BlockSpec((None, nsub, RNN_BD, RNN_BD), lambda i, c: (layer, c, 0, 0)),
            pl.BlockSpec((1, bd), col),
            pl.BlockSpec((1, bd), col),
            pl.BlockSpec((None, bd, D_MODEL), lambda i, c: (layer, c, 0)),
            pl.BlockSpec((1, SUBLANES, bd), lambda i, c: (i // tiles_per_seq, 0, c)),
            pl.BlockSpec((1, d0, bd), lambda i, c: (i // tiles_per_seq, 0, c)),
        ],
        out_specs=[
            pl.BlockSpec((tm, D_MODEL), row),
            pl.BlockSpec((1, SUBLANES, bd), lambda i, c: (i, 0, c)),
            pl.BlockSpec((1, d0, bd), lambda i, c: (i, 0, c)),
        ],
        out_shape=[jax.ShapeDtypeStruct((m, D_MODEL), F32),
                   jax.ShapeDtypeStruct((n_tiles, SUBLANES, D_MODEL), F32),
                   jax.ShapeDtypeStruct((n_tiles, d0, D_MODEL), F32)],
        scratch_shapes=[
            pltpu.VMEM((tm, D_MODEL), BF16),
            pltpu.VMEM((tm, D_MODEL), F32),
            pltpu.VMEM((d0 + tm, bd), F32),
            pltpu.VMEM((tm, bd), F32),
            pltpu.VMEM((tm, bd), F32),
            pltpu.VMEM((tm, bd), F32),
            pltpu.VMEM((nsteps, d0, bd), F32),
            pltpu.VMEM((nsteps, SUBLANES, bd), F32),
        ],
        compiler_params=pltpu.CompilerParams(
            dimension_semantics=("arbitrary", "arbitrary"), vmem_limit_bytes=VMEM_LIMIT),
        name="rg_lru",
    )(h, g, wx, wy, cw, cb, wa, ba, wi, bi, lam, wo, h0, state)


def _rope_tables(pos, rot, period):
    half = rot // 2
    inv = 1.0 / (ROPE_THETA ** (jnp.arange(half, dtype=F32) * (2.0 / rot)))
    ang = pos.astype(F32)[:, None] * inv[None, :]
    lane = jnp.arange(LANES) % period
    cos = jnp.cos(ang)[:, lane % half]
    sin = jnp.sin(ang)[:, lane % half]
    return jnp.stack([
        jnp.where(lane < rot, cos, 1.0),
        jnp.where(lane < half, -sin, 0.0),
        jnp.where((lane >= half) & (lane < rot), sin, 0.0),
    ])


def _time_major(x):
    return jnp.swapaxes(x, 0, 1)


def kernel(x_prompt, x_sample, cache_k, cache_v, cache_idx_k, state_lru_h, state_lru_conv, state_ffn_conv, page_table, meta_tokens, norm1_g, norm2_g, attn_w_in, attn_q_norm_g, attn_k_norm_g, attn_idx_k_norm_g, attn_w_out, lru_w_x, lru_w_y, lru_conv_w, lru_conv_b, lru_wa, lru_ba, lru_wi, lru_bi, lru_lambda, lru_w_out, ffn_w_up, ffn_w_gate, ffn_dw_w, ffn_dw_b, ffn_w_down):
    batch, seq, _ = x_prompt.shape
    nb, ns, _ = x_sample.shape
    depth = norm1_g.shape[0]
    t_real = seq + N_META
    tp = _round_up(t_real, QB)
    n_pages = page_table.shape[1]
    past = n_pages * PAGE
    topk_p = min(TOPK_MAX, t_real // 4)
    topk_s = min(TOPK_MAX, (past + ns) // 4)
    assert nb == SUBLANES and ns <= SUBLANES and n_pages % S1_PAGES == 0 and n_pages % S2_PAGES == 0

    tiles_per_seq = 6
    tm = tp // tiles_per_seq
    assert tm * tiles_per_seq == tp and tm % (2 * SUBLANES) == 0
    tail_end = t_real - (tiles_per_seq - 1) * tm
    assert tail_end % SUBLANES == 0 and tail_end >= SUBLANES
    tm_proj = 3 * LANES
    assert tp % tm_proj == 0
    ms = nb * ns

    hp = jnp.concatenate([
        jnp.broadcast_to(meta_tokens[None], (batch, N_META, D_MODEL)), x_prompt,
        jnp.zeros((batch, tp - t_real, D_MODEL), F32)], axis=1).reshape(batch * tp, D_MODEL)
    hs = _time_major(x_sample).reshape(ms, D_MODEL)

    rq_p = _rope_tables(jnp.arange(tp), ROT, HEAD_DIM)
    ri_p = _rope_tables(jnp.arange(tp), IDX_ROT, IDX_DIM)
    pos_s = jnp.repeat(past + jnp.arange(ns), nb)
    rq_s = _rope_tables(pos_s, ROT, HEAD_DIM)
    ri_s = _rope_tables(pos_s, IDX_ROT, IDX_DIM)

    cache_k4 = cache_k.reshape(cache_k.shape[:2] + (PAGE * N_KV, HEAD_DIM))
    cache_v4 = cache_v.reshape(cache_v.shape[:2] + (PAGE * N_KV, HEAD_DIM))
    cache_ik_t = jnp.swapaxes(cache_idx_k, 2, 3)

    w_in_t = jnp.pad(jnp.swapaxes(attn_w_in, 1, 2).astype(BF16), ((0, 0), (0, PROJ_WP - PROJ_W), (0, 0)))
    w_out_all = attn_w_out.astype(BF16)
    lru_mats = dict(wx=lru_w_x.astype(BF16), wy=lru_w_y.astype(BF16), wa=lru_wa.astype(BF16),
                    wi=lru_wi.astype(BF16), wo=lru_w_out.astype(BF16))
    ffn_mats = dict(wu=ffn_w_up.astype(BF16), wg=ffn_w_gate.astype(BF16), wd=ffn_w_down.astype(BF16))

    kp, vp, ikp, ksl, vsl, iksl = [], [], [], [], [], []
    hpl, cpl, hsl, csl, fpl, fsl = [], [], [], [], [], []
    last_tiles = jnp.arange(batch) * tiles_per_seq + tiles_per_seq - 1

    for layer in range(depth):
        mi = layer // 2
        g1 = norm1_g[layer][None]
        if layer % 2 == 0:
            qg = attn_q_norm_g[mi][None]
            kg = attn_k_norm_g[mi][None]
            ikg = jnp.pad(attn_idx_k_norm_g[mi], (0, LANES - IDX_DIM))[None]

            q, kf, kb, vf, vb, iq, ikf, ika, ikb, iw = _attn_project(
                hp, g1, w_in_t, mi, qg, kg, ikg, rq_p, ri_p, tm=tm_proj, nseq=batch, rows_real=t_real)
            logit_bound = (1.02 * HEAD_DIM ** 0.5) * jnp.max(jnp.abs(qg)) * jnp.max(jnp.abs(kg))
            hp = _attn_prompt(hp, q, iq, iw, kb, vb, ika, ikb, w_out_all, mi, logit_bound,
                              batch=batch, tp=tp, topk=topk_p)
            kp.append(kf.reshape(batch, t_real, N_KV, HEAD_DIM))
            vp.append(vf.reshape(batch, t_real, N_KV, HEAD_DIM))
            ikp.append(ikf)

            q, kf, kb, vf, vb, iq, ikf, ika, ikb, iw = _attn_project(
                hs, g1, w_in_t, mi, qg, kg, ikg, rq_s, ri_s, tm=ms, nseq=1, rows_real=ms)
            ksl.append(_time_major(kf.reshape(ns, nb, N_KV, HEAD_DIM)))
            vsl.append(_time_major(vf.reshape(ns, nb, N_KV, HEAD_DIM)))
            iksl.append(_time_major(ikf.reshape(ns, nb, IDX_DIM)))

            pad_q = lambda x: jnp.pad(x, [(0, 0)] * (x.ndim - 2) + [(0, SUBLANES - ns), (0, 0)])
            iq_b = pad_q(iq.reshape(ns, nb, IDX_HEADS, IDX_DIM).transpose(1, 2, 0, 3))
            iq_b = iq_b.reshape(nb, IDX_HEADS * SUBLANES, IDX_DIM)
            iw_b = jnp.pad(iw[:, :IDX_HEADS].reshape(ns, nb, IDX_HEADS).transpose(1, 2, 0), ((0, 0), (0, 0), (0, SUBLANES - ns)))
            iww = jnp.broadcast_to(iw_b.reshape(nb, IDX_HEADS * SUBLANES, 1), (nb, IDX_HEADS * SUBLANES, LANES))
            new_page = lambda x: jnp.pad(_time_major(x.reshape(ns, nb, -1)), ((0, 0), (0, PAGE - ns), (0, 0)))
            iknew = jnp.swapaxes(new_page(ika[:, :IDX_DIM]), 1, 2)
            keys, thr = _sample_index(page_table, cache_ik_t, mi, iq_b, iww, iknew, n_new=ns, topk=topk_s)

            q_b = pad_q(q.reshape(ns, nb, N_KV, GROUP, HEAD_DIM).transpose(1, 2, 3, 0, 4))
            qbd = q_b[:, :, :, :, None, :] * jnp.eye(N_KV, dtype=BF16)[None, :, None, None, :, None]
            qbd = qbd.reshape(nb, N_HEADS * SUBLANES, KV_W)
            o = _sample_attend(page_table, cache_k4, cache_v4, mi, qbd, keys, thr,
                               new_page(kb), new_page(vb), n_new=ns)
            o = o.reshape(nb, N_KV, GROUP, SUBLANES, N_KV, HEAD_DIM)
            o = jnp.stack([o[:, n, :, :ns, n, :] for n in range(N_KV)], axis=1)
            o = o.transpose(3, 0, 1, 2, 4).reshape(ms, Q_W).astype(BF16)
            hs = _matmul_res(hs, o, w_out_all, mi, tn=512)
        else:
            lw = dict(
                g=g1, cw=lru_conv_w[mi], cb=lru_conv_b[mi][None], ba=lru_ba[mi][None], bi=lru_bi[mi][None],
                lam=lru_lambda[mi][None], layer=mi, **lru_mats)
            hp, htail, utail = _lru_block(
                hp, h0=jnp.zeros((batch, SUBLANES, D_MODEL), F32),
                state=jnp.zeros((batch, SUBLANES, D_MODEL), F32),
                tm=tm, shift=1, tiles_per_seq=tiles_per_seq, tail_end=tail_end, **lw)
            hpl.append(htail[last_tiles, SUBLANES - 1])
            cpl.append(utail[last_tiles, SUBLANES - (LRU_CONV - 1):])
            d0s = (LRU_CONV - 1) * nb
            hs, htail, utail = _lru_block(
                hs, h0=state_lru_h[mi][None],
                state=_time_major(state_lru_conv[mi]).reshape(1, d0s, D_MODEL),
                tm=ms, shift=nb, tiles_per_seq=1, tail_end=ms, **lw)
            hsl.append(htail[0])
            csl.append(_time_major(utail.reshape(LRU_CONV - 1, nb, D_MODEL)))

        fw = dict(g=norm2_g[layer][None], cw=ffn_dw_w[layer], cb=ffn_dw_b[layer][None], layer=layer, **ffn_mats)
        hp, tail = _conv_ffn(hp, state=jnp.zeros((batch, SUBLANES, D_FF), F32),
                             tm=tm, tf=512, shift=1, tiles_per_seq=tiles_per_seq, tail_end=tail_end, **fw)
        fpl.append(tail[last_tiles, SUBLANES - (FFN_CONV - 1):])
        d0s = (FFN_CONV - 1) * nb
        hs, tail = _conv_ffn(hs, state=_time_major(state_ffn_conv[layer]).reshape(1, d0s, D_FF),
                             tm=ms, tf=512, shift=nb, tiles_per_seq=1, tail_end=ms, **fw)
        fsl.append(_time_major(tail.reshape(FFN_CONV - 1, nb, D_FF)))

    y_prompt = hp.reshape(batch, tp, D_MODEL)[:, N_META:t_real]
    y_sample = _time_major(hs.reshape(ns, nb, D_MODEL))
    return (y_prompt, y_sample, jnp.stack(kp), jnp.stack(vp), jnp.stack(ikp),
            jnp.stack(ksl), jnp.stack(vsl), jnp.stack(iksl),
            jnp.stack(hpl), jnp.stack(cpl), jnp.stack(hsl), jnp.stack(csl),
            jnp.stack(fpl), jnp.stack(fsl))
```

```python
import functools
import struct

import jax
import jax.numpy as jnp
from jax import lax
from jax.experimental import pallas as pl
from jax.experimental.pallas import tpu as pltpu

F32 = jnp.float32
BF16 = jnp.bfloat16
I32 = jnp.int32

D_MODEL = 2048
N_META = 16
N_HEADS = 16
HEAD_DIM = 128
N_KV = 4
GROUP = N_HEADS // N_KV
ROT = HEAD_DIM // 4
IDX_HEADS = 16
IDX_DIM = 64
IDX_ROT = IDX_DIM // 4
ROPE_THETA = 500000.0
TOPK_MAX = 256
PAGE = 128
Q_W = N_HEADS * HEAD_DIM
KV_W = N_KV * HEAD_DIM
IQ_W = IDX_HEADS * IDX_DIM
PROJ_W = Q_W + 2 * KV_W + IQ_W + IDX_DIM + IDX_HEADS
D_FF = 3 * D_MODEL
FFN_CONV = 3
LRU_CONV = 4
LRU_C = 8.0
RNN_BLOCKS = 8
RNN_BD = D_MODEL // RNN_BLOCKS
EPS = 1e-6
NEG = -1e30

LANES = 128
SUBLANES = 8
QB = 128
MXU_COLS = 256
PROJ_WP = -(-PROJ_W // LANES) * LANES
VMEM_LIMIT = 56 * 1024 * 1024

_NT = (((1,), (1,)), ((), ()))
_INT_MIN = -2 ** 31


def _round_up(x, m):
    return -(-x // m) * m


def _rms_rows(x, g):
    ms = jnp.mean(x * x, axis=-1, keepdims=True)
    return x * lax.rsqrt(ms + EPS) * g


def _gelu(x):
    return 0.5 * x * (1.0 + jnp.tanh(0.7978845608028654 * (x + 0.044715 * (x * x * x))))


def _sigmoid(x):
    return 0.5 * jnp.tanh(0.5 * x) + 0.5


def _sort_key(x):
    b = lax.bitcast_convert_type(x + 0.0, I32)
    return b ^ ((b >> 31) & 0x7FFFFFFF)


def _sort_key_of_constant(x):
    b = struct.unpack("<i", struct.pack("<f", x))[0]
    return b ^ ((b >> 31) & 0x7FFFFFFF)


_NEG_KEY = _sort_key_of_constant(NEG)


def _kth_largest_key(count_ge, shape, k):
    def bit_body(bi, t):
        cand = t | jnp.left_shift(jnp.int32(1), 31 - bi)
        cnt = count_ge(cand ^ _INT_MIN)
        return jnp.where(cnt >= k, cand, t)
    t = lax.fori_loop(0, 32, bit_body, jnp.zeros(shape, I32))
    return t ^ _INT_MIN


def _rope(x, tab_ref, sh):
    return (x * tab_ref[0] + pltpu.roll(x, LANES - sh, 1) * tab_ref[1]
            + pltpu.roll(x, sh, 1) * tab_ref[2])


def _proj_kernel(h_ref, g_ref, w_ref, qg_ref, kg_ref, ikg_ref, rq_ref, ri_ref,
                 q_ref, kf_ref, kb_ref, vf_ref, vb_ref, iq_ref, ikf_ref, ika_ref, ikb_ref, iw_ref):
    xn = _rms_rows(h_ref[...], g_ref[...]).astype(BF16)

    def zcols(lo, width=MXU_COLS):
        return lax.dot_general(xn, w_ref[lo:lo + width, :], _NT, preferred_element_type=F32)

    tm = h_ref.shape[0]
    halves = [slice(s * LANES, (s + 1) * LANES) for s in range(MXU_COLS // LANES)]
    for j in range(Q_W // MXU_COLS):
        z = zcols(j * MXU_COLS)
        for sl in halves:
            x = _rope(_rms_rows(z[:, sl], qg_ref[...]), rq_ref, ROT // 2)
            q_ref[:, j * MXU_COLS + sl.start:j * MXU_COLS + sl.stop] = x.astype(BF16)
    for j in range(KV_W // MXU_COLS):
        z = zcols(Q_W + j * MXU_COLS)
        for s, sl in enumerate(halves):
            n = j * len(halves) + s
            x = _rope(_rms_rows(z[:, sl], kg_ref[...]), rq_ref, ROT // 2)
            kf_ref[0, pl.ds(n, tm, stride=N_KV), :] = x
            kb_ref[:, n * HEAD_DIM:(n + 1) * HEAD_DIM] = x.astype(BF16)
        z = zcols(Q_W + KV_W + j * MXU_COLS)
        for s, sl in enumerate(halves):
            n = j * len(halves) + s
            vf_ref[0, pl.ds(n, tm, stride=N_KV), :] = z[:, sl]
        vb_ref[:, j * MXU_COLS:(j + 1) * MXU_COLS] = z.astype(BF16)
    for j in range(IQ_W // MXU_COLS):
        z = zcols(Q_W + 2 * KV_W + j * MXU_COLS)
        for sl in halves:
            x = _rope(z[:, sl], ri_ref, IDX_ROT // 2)
            iq_ref[:, j * MXU_COLS + sl.start:j * MXU_COLS + sl.stop] = x.astype(BF16)

    x = zcols(Q_W + 2 * KV_W + IQ_W, LANES)
    lane = lax.broadcasted_iota(I32, x.shape, 1)
    ms = jnp.sum(jnp.where(lane < IDX_DIM, x * x, 0.0), axis=-1, keepdims=True) * (1.0 / IDX_DIM)
    y = _rope(x * lax.rsqrt(ms + EPS) * ikg_ref[...], ri_ref, IDX_ROT // 2)
    ikf_ref[0] = y[:, :IDX_DIM]
    ika_ref[...] = y.astype(BF16)
    ikb_ref[...] = pltpu.roll(y, IDX_DIM, 1).astype(BF16)
    iw_ref[...] = jnp.where(lane < IDX_HEADS, pltpu.roll(x, LANES - IDX_DIM, 1) * (IDX_HEADS ** -0.5), 0.0)


def _attn_project(h, g, w_t, layer, qg, kg, ikg, rq, ri, *, tm, nseq, rows_real):
    m = h.shape[0]
    tab_tiles = rq.shape[1] // tm
    assert m == nseq * tab_tiles * tm
    row = lambda i: (i, 0)
    const = lambda i: (0, 0)
    tab = lambda i: (0, i % tab_tiles, 0)
    seq_row = lambda i: (i // tab_tiles, i % tab_tiles, 0)
    sds = jax.ShapeDtypeStruct
    return pl.pallas_call(
        _proj_kernel,
        grid=(m // tm,),
        in_specs=[
            pl.BlockSpec((tm, D_MODEL), row),
            pl.BlockSpec((1, D_MODEL), const),
            pl.BlockSpec((None,) + w_t.shape[1:], lambda i: (layer, 0, 0), pipeline_mode=pl.Buffered(1)),
            pl.BlockSpec((1, LANES), const),
            pl.BlockSpec((1, LANES), const),
            pl.BlockSpec((1, LANES), const),
            pl.BlockSpec((3, tm, LANES), tab),
            pl.BlockSpec((3, tm, LANES), tab),
        ],
        out_specs=[
            pl.BlockSpec((tm, Q_W), row),
            pl.BlockSpec((1, tm * N_KV, HEAD_DIM), seq_row),
            pl.BlockSpec((tm, KV_W), row),
            pl.BlockSpec((1, tm * N_KV, HEAD_DIM), seq_row),
            pl.BlockSpec((tm, KV_W), row),
            pl.BlockSpec((tm, IQ_W), row),
            pl.BlockSpec((1, tm, IDX_DIM), seq_row),
            pl.BlockSpec((tm, LANES), row),
            pl.BlockSpec((tm, LANES), row),
            pl.BlockSpec((tm, LANES), row),
        ],
        out_shape=[
            sds((m, Q_W), BF16), sds((nseq, rows_real * N_KV, HEAD_DIM), F32), sds((m, KV_W), BF16),
            sds((nseq, rows_real * N_KV, HEAD_DIM), F32), sds((m, KV_W), BF16), sds((m, IQ_W), BF16),
            sds((nseq, rows_real, IDX_DIM), F32), sds((m, LANES), BF16), sds((m, LANES), BF16),
            sds((m, LANES), F32),
        ],
        compiler_params=pltpu.CompilerParams(
            dimension_semantics=("arbitrary",), vmem_limit_bytes=VMEM_LIMIT),
        name="attn_project",
    )(h, g, w_t, qg, kg, ikg, rq, ri)


def _attn_prompt_kernel(shift_ref, q_ref, iq_ref, iw_ref, k_ref, v_ref, ika_ref, ikb_ref,
                        wout_ref, h_ref, out_ref, keys_sc, iqt_sc, jt_sc, o_sc, m_sc, acc_sc, *,
                        kc, topk, online):
    i = pl.program_id(1)
    q0 = i * QB
    tp = k_ref.shape[0]
    nch = (q0 + QB + kc - 1) // kc
    nlt = kc // LANES
    qcol = q0 + lax.broadcasted_iota(I32, (LANES, QB), 1)
    krow = lax.broadcasted_iota(I32, (LANES, QB), 0)
    for p in range(IDX_HEADS // 2):
        sl = slice(p * LANES, (p + 1) * LANES)
        iqt_sc[:, sl] = iq_ref[:, sl].astype(F32).T.astype(BF16)
    iwt = iw_ref[...].T
    if keys_sc.shape[0] > tp:
        keys_sc[tp:, :] = jnp.full((keys_sc.shape[0] - tp, QB), _INT_MIN, I32)

    def chunk_start(c):
        return pl.multiple_of(jnp.minimum(c * kc, tp - kc), LANES)

    def score_chunk(c, carry):
        k0 = chunk_start(c)
        ka = ika_ref[pl.ds(k0, kc), :]
        kb = ikb_ref[pl.ds(k0, kc), :]
        sc = [jnp.zeros((LANES, QB), F32) for _ in range(nlt)]
        for j in range(IQ_W // MXU_COLS):
            w = iqt_sc[:, j * MXU_COLS:(j + 1) * MXU_COLS]
            sa = jnp.dot(ka, w, preferred_element_type=F32)
            sb = jnp.dot(kb, w, preferred_element_type=F32)
            for half in range(MXU_COLS // LANES):
                p = j * (MXU_COLS // LANES) + half
                cols = slice(half * LANES, (half + 1) * LANES)
                wa = iwt[2 * p:2 * p + 1, :]
                wb = iwt[2 * p + 1:2 * p + 2, :]
                for lt in range(nlt):
                    rows = slice(lt * LANES, (lt + 1) * LANES)
                    sc[lt] = (sc[lt] + wa * jnp.maximum(sa[rows, cols], 0.0)
                              + wb * jnp.maximum(sb[rows, cols], 0.0))
        for lt in range(nlt):
            s = jnp.where(k0 + lt * LANES + krow <= qcol, sc[lt] * (IDX_DIM ** -0.5), NEG)
            keys_sc[pl.ds(k0 + lt * LANES, LANES), :] = _sort_key(s)
        return carry

    lax.fori_loop(0, nch, score_chunk, 0)

    sub8 = lax.broadcasted_iota(I32, (SUBLANES, QB), 0)

    def count_keys(pred):
        def body(c, acc):
            k0 = pl.multiple_of(c * kc, LANES)
            acc = list(acc)
            for j in range(kc // SUBLANES):
                kk = keys_sc[pl.ds(k0 + j * SUBLANES, SUBLANES), :]
                hit = pred(kk, k0 + j * SUBLANES + sub8)
                acc[j % len(acc)] = acc[j % len(acc)] + jnp.where(hit, 1.0, 0.0)
            return tuple(acc)
        acc = lax.fori_loop(0, nch, body, tuple(jnp.zeros((SUBLANES, QB), F32) for _ in range(4)))
        tot = (acc[0] + acc[1]) + (acc[2] + acc[3])
        for s in (4, 2, 1):
            tot = tot + pltpu.roll(tot, s, 0)
        return tot

    def count_ge(cand):
        return count_keys(lambda kk, kidx: kk >= cand)

    k = float(topk)
    thr8 = _kth_largest_key(count_ge, (SUBLANES, QB), k)
    tied = (count_ge(thr8) > k) & (thr8 > _NEG_KEY)
    jt_sc[...] = jnp.full((SUBLANES, QB), tp, I32)

    @pl.when(jnp.max(jnp.where(tied, 1.0, 0.0)) > 0.0)
    def _():
        need = k - count_keys(lambda kk, kidx: kk > thr8)
        nbits = (keys_sc.shape[0] - 1).bit_length()

        def bit_body(bi, x):
            cand = x | jnp.left_shift(jnp.int32(1), nbits - 1 - bi)
            below = count_keys(lambda kk, kidx: (kk == thr8) & (kidx < cand))
            return jnp.where(below < need, cand, x)

        x = lax.fori_loop(0, nbits, bit_body, jnp.zeros((SUBLANES, QB), I32))
        jt_sc[...] = jnp.where(tied, x, tp)

    thr = jnp.broadcast_to(thr8[0:1, :], (LANES, QB))
    jt = jnp.broadcast_to(jt_sc[0:1, :], (LANES, QB))

    scale = HEAD_DIM ** -0.5
    unmasked = 0.0 if online else -shift_ref[0, 0]
    acc_sc[...] = jnp.zeros(acc_sc.shape, F32)
    if online:
        m_sc[...] = jnp.full(m_sc.shape, NEG, F32)
    ones_col = jnp.where(lax.broadcasted_iota(I32, (kc, LANES), 1) == 0, 1.0, 0.0).astype(BF16)

    def att_chunk(c, carry):
        k0 = chunk_start(c)
        bias = []
        for lt in range(nlt):
            kk = keys_sc[pl.ds(k0 + lt * LANES, LANES), :]
            kidx = k0 + lt * LANES + krow
            sel = ((kk > thr) | ((kk == thr) & (kidx <= jt))) & (kidx <= qcol) & (kidx >= c * kc)
            bias.append(jnp.where(sel, unmasked, NEG).T)
        bias = jnp.concatenate(bias, axis=1)
        bias = jnp.concatenate([bias] * GROUP, axis=0)
        for n in range(N_KV):
            qs = jnp.concatenate(
                [q_ref[:, (n * GROUP + g) * LANES:(n * GROUP + g + 1) * LANES] for g in range(GROUP)], axis=0)
            kt = k_ref[pl.ds(k0, kc), n * LANES:(n + 1) * LANES]
            vt = jnp.concatenate([v_ref[pl.ds(k0, kc), n * LANES:(n + 1) * LANES], ones_col], axis=1)
            s = lax.dot_general(qs, kt, _NT, preferred_element_type=F32) * scale + bias
            if online:
                m_old = m_sc[n]
                m_new = jnp.maximum(m_old, jnp.max(s, axis=1, keepdims=True))
                p = jnp.exp(s - m_new).astype(BF16)
                acc_sc[n] = jnp.exp(m_old - m_new) * acc_sc[n] + jnp.dot(p, vt, preferred_element_type=F32)
                m_sc[n] = m_new
            else:
                acc_sc[n] += jnp.dot(jnp.exp(s).astype(BF16), vt, preferred_element_type=F32)
        return carry

    lax.fori_loop(0, nch, att_chunk, 0)
    for n in range(N_KV):
        a = acc_sc[n]
        o = a[:, :LANES] / a[:, LANES:LANES + 1]
        for g in range(GROUP):
            hh = n * GROUP + g
            o_sc[:, hh * LANES:(hh + 1) * LANES] = o[g * QB:(g + 1) * QB].astype(BF16)

    out_ref[...] = h_ref[...] + jnp.dot(o_sc[...], wout_ref[...], preferred_element_type=F32)


MAX_LOGIT_BOUND = 40.0


def _attn_prompt(h, q, iq, iw, kb, vb, ika, ikb, wout, layer, logit_bound, *, batch, tp, topk):
    args = (logit_bound.reshape(1, 1), q, iq, iw, kb, vb, ika, ikb, wout, h)
    kc = 3 * LANES
    assert kc >= topk and (tp - kc) % LANES == 0
    kw = dict(layer=layer, batch=batch, tp=tp, kc=kc, topk=topk)
    return lax.cond(logit_bound <= MAX_LOGIT_BOUND,
                    lambda: _attn_prompt_call(*args, online=False, **kw),
                    lambda: _attn_prompt_call(*args, online=True, **kw))


def _attn_prompt_call(shift, q, iq, iw, kb, vb, ika, ikb, wout, h, *, layer, batch, tp, kc, topk, online):
    nqb = tp // QB
    qrow = lambda b, i: (b * nqb + i, 0)
    seq = lambda b, i: (b, 0)
    return pl.pallas_call(
        functools.partial(_attn_prompt_kernel, kc=kc, topk=topk, online=online),
        grid=(batch, nqb),
        in_specs=[
            pl.BlockSpec(memory_space=pltpu.SMEM),
            pl.BlockSpec((QB, Q_W), qrow),
            pl.BlockSpec((QB, IQ_W), qrow),
            pl.BlockSpec((QB, LANES), qrow),
            pl.BlockSpec((tp, KV_W), seq),
            pl.BlockSpec((tp, KV_W), seq),
            pl.BlockSpec((tp, LANES), seq),
            pl.BlockSpec((tp, LANES), seq),
            pl.BlockSpec((None, Q_W, D_MODEL), lambda b, i: (layer, 0, 0)),
            pl.BlockSpec((QB, D_MODEL), qrow),
        ],
        out_specs=pl.BlockSpec((QB, D_MODEL), qrow),
        out_shape=jax.ShapeDtypeStruct(h.shape, F32),
        scratch_shapes=[
            pltpu.VMEM((_round_up(tp, kc), QB), I32),
            pltpu.VMEM((LANES, IQ_W), BF16),
            pltpu.VMEM((SUBLANES, QB), I32),
            pltpu.VMEM((QB, Q_W), BF16),
            pltpu.VMEM((N_KV, GROUP * QB, 1), F32),
            pltpu.VMEM((N_KV, GROUP * QB, 2 * HEAD_DIM), F32),
        ],
        compiler_params=pltpu.CompilerParams(
            dimension_semantics=("arbitrary", "arbitrary"), vmem_limit_bytes=VMEM_LIMIT),
        name="attn_prompt_online" if online else "attn_prompt_bounded",
    )(shift, q, iq, iw, kb, vb, ika, ikb, wout, h)


S1_PAGES = 16
S2_PAGES = 8


def _idx_page_scores(iq, iww, page_t):
    s = jnp.dot(iq, page_t, preferred_element_type=F32)
    sc = jnp.zeros((SUBLANES, PAGE), F32)
    for hh in range(IDX_HEADS):
        sl = slice(hh * SUBLANES, (hh + 1) * SUBLANES)
        sc = sc + iww[sl, :] * jnp.maximum(s[sl, :], 0.0)
    return sc * (IDX_DIM ** -0.5)


def _sample_index_kernel(pt_ref, *refs, past, n_new, topk):
    pages = refs[:S1_PAGES]
    iq_ref, iww_ref, iknew_ref, keys_ref, thr_ref = refs[S1_PAGES:]
    s = pl.program_id(1)
    iq = iq_ref[0]
    iww = iww_ref[0]
    for r in range(S1_PAGES):
        sc = _idx_page_scores(iq, iww, pages[r][...].astype(BF16))
        off = pl.multiple_of((s * S1_PAGES + r) * PAGE, PAGE)
        keys_ref[0, :, pl.ds(off, PAGE)] = _sort_key(sc)

    @pl.when(s == pl.num_programs(1) - 1)
    def _():
        sc = _idx_page_scores(iq, iww, iknew_ref[0])
        t = lax.broadcasted_iota(I32, sc.shape, 0)
        jn = lax.broadcasted_iota(I32, sc.shape, 1)
        sc = jnp.where((jn <= t) & (jn < n_new), sc, NEG)
        keys_ref[0, :, pl.ds(past, PAGE)] = _sort_key(sc)
        nk = keys_ref.shape[2]
        lane = lax.broadcasted_iota(I32, (SUBLANES, LANES), 1)

        def count_keys(pred):
            acc = [jnp.zeros((SUBLANES, LANES), F32) for _ in range(4)]
            for j in range(nk // LANES):
                kk = keys_ref[0, :, j * LANES:(j + 1) * LANES]
                acc[j % 4] = acc[j % 4] + jnp.where(pred(kk, j * LANES + lane), 1.0, 0.0)
            tot = (acc[0] + acc[1]) + (acc[2] + acc[3])
            return jnp.broadcast_to(jnp.sum(tot, axis=1, keepdims=True), (SUBLANES, LANES))

        def count_ge(cand):
            return count_keys(lambda kk, kidx: kk >= cand)

        k = float(topk)
        thr = _kth_largest_key(count_ge, (SUBLANES, LANES), k)
        thr_ref[0, :, :LANES] = thr
        tied = (count_ge(thr) > k) & (thr > _NEG_KEY)
        thr_ref[0, :, LANES:] = jnp.full((SUBLANES, LANES), nk, I32)

        @pl.when(jnp.max(jnp.where(tied, 1.0, 0.0)) > 0.0)
        def _():
            need = k - count_keys(lambda kk, kidx: kk > thr)
            nbits = (nk - 1).bit_length()

            def bit_body(bi, x):
                cand = x | jnp.left_shift(jnp.int32(1), nbits - 1 - bi)
                below = count_keys(lambda kk, kidx: (kk == thr) & (kidx < cand))
                return jnp.where(below < need, cand, x)

            x = lax.fori_loop(0, nbits, bit_body, jnp.zeros((SUBLANES, LANES), I32))
            thr_ref[0, :, LANES:] = jnp.where(tied, x, nk)


def _sample_index(page_table, cache_ik, layer, iq, iww, iknew, *, n_new, topk):
    nb, n_pages = page_table.shape
    steps = n_pages // S1_PAGES
    nk = (n_pages + 1) * PAGE
    page_spec = lambda r: pl.BlockSpec(
        (None, None, IDX_DIM, PAGE), lambda b, s, pt: (layer, pt[b, s * S1_PAGES + r], 0, 0))
    per_b = lambda b, s, pt: (b, 0, 0)
    return pl.pallas_call(
        functools.partial(_sample_index_kernel, past=n_pages * PAGE, n_new=n_new, topk=topk),
        grid_spec=pltpu.PrefetchScalarGridSpec(
            num_scalar_prefetch=1,
            grid=(nb, steps),
            in_specs=[page_spec(r) for r in range(S1_PAGES)] + [
                pl.BlockSpec((1, IDX_HEADS * SUBLANES, IDX_DIM), per_b),
                pl.BlockSpec((1, IDX_HEADS * SUBLANES, LANES), per_b),
                pl.BlockSpec((1, IDX_DIM, PAGE), per_b),
            ],
            out_specs=[
                pl.BlockSpec((1, SUBLANES, nk), per_b),
                pl.BlockSpec((1, SUBLANES, 2 * LANES), per_b),
            ],
        ),
        out_shape=[jax.ShapeDtypeStruct((nb, SUBLANES, nk), I32),
                   jax.ShapeDtypeStruct((nb, SUBLANES, 2 * LANES), I32)],
        compiler_params=pltpu.CompilerParams(
            dimension_semantics=("arbitrary", "arbitrary"), vmem_limit_bytes=VMEM_LIMIT),
        name="sample_index",
    )(page_table, *([cache_ik] * S1_PAGES), iq, iww, iknew)


def _sample_attend_kernel(pt_ref, *refs, n_new):
    kpages = refs[:S2_PAGES]
    vpages = refs[S2_PAGES:2 * S2_PAGES]
    (qbd_ref, keys_ref, keysnew_ref, thr_ref, knew_ref, vnew_ref, o_ref,
     m_sc, l_sc, acc_sc) = refs[2 * S2_PAGES:]
    s = pl.program_id(1)
    scale = HEAD_DIM ** -0.5
    qbd = qbd_ref[0]
    thr = thr_ref[0, :, :LANES]
    jt = thr_ref[0, :, LANES:]
    lane = lax.broadcasted_iota(I32, (SUBLANES, LANES), 1)

    def selected(kk, kidx):
        return (kk > thr) | ((kk == thr) & (kidx <= jt))

    @pl.when(s == 0)
    def _():
        m_sc[...] = jnp.full(m_sc.shape, NEG, F32)
        l_sc[...] = jnp.zeros(l_sc.shape, F32)
        acc_sc[...] = jnp.zeros(acc_sc.shape, F32)

    def page_update(kp, vp, bias):
        bias = jnp.concatenate([bias] * (N_HEADS), axis=0)
        sc = lax.dot_general(qbd, kp, _NT, preferred_element_type=F32) * scale + bias
        m_old = m_sc[...]
        m_new = jnp.maximum(m_old, jnp.max(sc, axis=1, keepdims=True))
        alpha = jnp.exp(m_old - m_new)
        p = jnp.exp(sc - m_new)
        l_sc[...] = alpha * l_sc[...] + jnp.sum(p, axis=1, keepdims=True)
        acc_sc[...] = alpha * acc_sc[...] + jnp.dot(p.astype(BF16), vp, preferred_element_type=F32)
        m_sc[...] = m_new

    def page_rows(ref):
        return jnp.concatenate(
            [ref[pl.ds(n, PAGE, stride=N_KV), :] for n in range(N_KV)], axis=1).astype(BF16)

    bias = jnp.concatenate(
        [jnp.where(selected(keys_ref[0, :, r * PAGE:(r + 1) * PAGE], (s * S2_PAGES + r) * PAGE + lane), 0.0, NEG)
         for r in range(S2_PAGES)], axis=1)
    page_update(jnp.concatenate([page_rows(kpages[r]) for r in range(S2_PAGES)], axis=0),
                jnp.concatenate([page_rows(vpages[r]) for r in range(S2_PAGES)], axis=0), bias)

    @pl.when(s == pl.num_programs(1) - 1)
    def _():
        kk = keysnew_ref[0]
        t = lax.broadcasted_iota(I32, kk.shape, 0)
        jn = lax.broadcasted_iota(I32, kk.shape, 1)
        past = pl.num_programs(1) * S2_PAGES * PAGE
        page_update(knew_ref[0], vnew_ref[0],
                    jnp.where(selected(kk, past + jn) & (jn <= t) & (jn < n_new), 0.0, NEG))
        o_ref[0] = acc_sc[...] / l_sc[...]


def _sample_attend(page_table, cache_k, cache_v, layer, qbd, keys, thr, knew, vnew, *, n_new):
    nb, n_pages = page_table.shape
    steps = n_pages // S2_PAGES
    page_spec = lambda r: pl.BlockSpec(
        (None, None, PAGE * N_KV, HEAD_DIM), lambda b, s, pt: (layer, pt[b, s * S2_PAGES + r], 0, 0))
    per_b = lambda b, s, pt: (b, 0, 0)
    rows = N_HEADS * SUBLANES
    return pl.pallas_call(
        functools.partial(_sample_attend_kernel, n_new=n_new),
        grid_spec=pltpu.PrefetchScalarGridSpec(
            num_scalar_prefetch=1,
            grid=(nb, steps),
            in_specs=[page_spec(r) for r in range(S2_PAGES)] * 2 + [
                pl.BlockSpec((1, rows, KV_W), per_b),
                pl.BlockSpec((1, SUBLANES, S2_PAGES * PAGE), lambda b, s, pt: (b, 0, s)),
                pl.BlockSpec((1, SUBLANES, PAGE), lambda b, s, pt: (b, 0, n_pages)),
                pl.BlockSpec((1, SUBLANES, 2 * LANES), per_b),
                pl.BlockSpec((1, PAGE, KV_W), per_b),
                pl.BlockSpec((1, PAGE, KV_W), per_b),
            ],
            out_specs=pl.BlockSpec((1, rows, KV_W), per_b),
            scratch_shapes=[
                pltpu.VMEM((rows, 1), F32),
                pltpu.VMEM((rows, 1), F32),
                pltpu.VMEM((rows, KV_W), F32),
            ],
        ),
        out_shape=jax.ShapeDtypeStruct((nb, rows, KV_W), F32),
        compiler_params=pltpu.CompilerParams(
            dimension_semantics=("arbitrary", "arbitrary"), vmem_limit_bytes=VMEM_LIMIT),
        name="sample_attend",
    )(page_table, *([cache_k] * S2_PAGES), *([cache_v] * S2_PAGES), qbd, keys, keys, thr, knew, vnew)


def _matmul_res_kernel(h_ref, x_ref, w_ref, o_ref):
    o_ref[...] = h_ref[...] + jnp.dot(x_ref[...], w_ref[...], preferred_element_type=F32)


def _matmul_res(h, x, w, layer, *, tn):
    m, n = h.shape
    kdim = x.shape[1]
    return pl.pallas_call(
        _matmul_res_kernel,
        grid=(n // tn,),
        in_specs=[pl.BlockSpec((m, tn), lambda j: (0, j)),
                  pl.BlockSpec((m, kdim), lambda j: (0, 0)),
                  pl.BlockSpec((None, kdim, tn), lambda j: (layer, 0, j))],
        out_specs=pl.BlockSpec((m, tn), lambda j: (0, j)),
        out_shape=jax.ShapeDtypeStruct((m, n), F32),
        compiler_params=pltpu.CompilerParams(dimension_semantics=("arbitrary",)),
        name="matmul_residual",
    )(h, x, w)


def _conv_rows(ext_sc, u, init_ref, carry_sc, c, first, cw_ref, cb_ref, tail_ref, *,
               width, shift, d0, tail_end):
    tm = u.shape[0]
    ext_sc[d0:d0 + tm, :] = u
    ext_sc[0:d0, :] = jnp.where(first, init_ref[0], carry_sc[c])

    y = cb_ref[...] + cw_ref[width - 1:width, :] * u
    for j in range(1, width):
        y = y + cw_ref[width - 1 - j:width - j, :] * ext_sc[d0 - j * shift:d0 - j * shift + tm, :]
    carry_sc[c] = ext_sc[tm:tm + d0, :]
    tail_ref[0] = ext_sc[tail_end:tail_end + d0, :]
    return y


def _ffn_kernel(h_ref, g_ref, wu_ref, wg_ref, cw_ref, cb_ref, wd_ref, st_ref, out_ref, tail_ref,
                xn_sc, acc_sc, ext_sc, carry_sc, *, shift, d0, tiles_per_seq, tail_end):
    i = pl.program_id(0)
    c = pl.program_id(1)

    @pl.when(c == 0)
    def _():
        xn_sc[...] = _rms_rows(h_ref[...], g_ref[...]).astype(BF16)
        acc_sc[...] = jnp.zeros(acc_sc.shape, F32)

    xn = xn_sc[...]
    u = jnp.dot(xn, wu_ref[...], preferred_element_type=F32)
    conv = _conv_rows(ext_sc, u, st_ref, carry_sc, c, (i % tiles_per_seq) == 0, cw_ref, cb_ref,
                      tail_ref, width=FFN_CONV, shift=shift, d0=d0, tail_end=tail_end)
    mid = _gelu(conv) * jnp.dot(xn, wg_ref[...], preferred_element_type=F32)
    acc_sc[...] += jnp.dot(mid.astype(BF16), wd_ref[...], preferred_element_type=F32)

    @pl.when(c == pl.num_programs(1) - 1)
    def _():
        out_ref[...] = h_ref[...] + acc_sc[...]


def _conv_ffn(h, g, wu, wg, cw, cb, wd, state, *, layer, tm, tf, shift, tiles_per_seq, tail_end):
    m = h.shape[0]
    d0 = state.shape[1]
    n_tiles, n_chunks = m // tm, D_FF // tf
    row = lambda i, c: (i, 0)
    return pl.pallas_call(
        functools.partial(_ffn_kernel, shift=shift, d0=d0, tiles_per_seq=tiles_per_seq, tail_end=tail_end),
        grid=(n_tiles, n_chunks),
        in_specs=[
            pl.BlockSpec((tm, D_MODEL), row),
            pl.BlockSpec((1, D_MODEL), lambda i, c: (0, 0)),
            pl.BlockSpec((None, D_MODEL, tf), lambda i, c: (layer, 0, c)),
            pl.BlockSpec((None, D_MODEL, tf), lambda i, c: (layer, 0, c)),
            pl.BlockSpec((FFN_CONV, tf), lambda i, c: (0, c)),
            pl.BlockSpec((1, tf), lambda i, c: (0, c)),
            pl.BlockSpec((None, tf, D_MODEL), lambda i, c: (layer, c, 0)),
            pl.BlockSpec((1, d0, tf), lambda i, c: (i // tiles_per_seq, 0, c)),
        ],
        out_specs=[
            pl.BlockSpec((tm, D_MODEL), row),
            pl.BlockSpec((1, d0, tf), lambda i, c: (i, 0, c)),
        ],
        out_shape=[jax.ShapeDtypeStruct((m, D_MODEL), F32),
                   jax.ShapeDtypeStruct((n_tiles, d0, D_FF), F32)],
        scratch_shapes=[
            pltpu.VMEM((tm, D_MODEL), BF16),
            pltpu.VMEM((tm, D_MODEL), F32),
            pltpu.VMEM((d0 + tm, tf), F32),
            pltpu.VMEM((n_chunks, d0, tf), F32),
        ],
        compiler_params=pltpu.CompilerParams(
            dimension_semantics=("arbitrary", "arbitrary"), vmem_limit_bytes=VMEM_LIMIT),
        name="conv_ffn",
    )(h, g, wu, wg, cw, cb, wd, state)


LRU_BLOCKS_PER_STEP = 2

def _scan_rows(a_sc, b_sc, hc, *, shift):
    tm = a_sc.shape[0]
    ngroups = tm // SUBLANES
    row = lax.broadcasted_iota(I32, (SUBLANES, a_sc.shape[1]), 0)

    def body(gi, hc):
        r0 = gi * SUBLANES
        a = a_sc[pl.ds(r0, SUBLANES), :]
        b = b_sc[pl.ds(r0, SUBLANES), :]
        if shift == 1:
            for s in (1, 2, 4):
                keep = row >= s
                a_sh = jnp.where(keep, pltpu.roll(a, s, 0), 1.0)
                b_sh = jnp.where(keep, pltpu.roll(b, s, 0), 0.0)
                b = a * b_sh + b
                a = a * a_sh
        hrows = a * hc + b
        b_sc[pl.ds(r0, SUBLANES), :] = hrows
        if shift == 1:
            return jnp.broadcast_to(hrows[SUBLANES - 1:SUBLANES, :], hrows.shape)
        return hrows

    for gi in range(ngroups):
        hc = body(gi, hc)
    return hc


def _lru_kernel(h_ref, g_ref, wx_ref, wy_ref, cw_ref, cb_ref, wa_ref, ba_ref, wi_ref, bi_ref,
                lam_ref, wo_ref, h0_ref, st_ref, out_ref, htail_ref, utail_ref,
                xn_sc, acc_sc, ext_sc, a_sc, b_sc, gate_sc, carry_u, carry_h, *,
                shift, d0, tiles_per_seq, tail_end):
    i = pl.program_id(0)
    c = pl.program_id(1)
    first = (i % tiles_per_seq) == 0

    @pl.when(c == 0)
    def _():
        xn_sc[...] = _rms_rows(h_ref[...], g_ref[...]).astype(BF16)
        acc_sc[...] = jnp.zeros(acc_sc.shape, F32)

    xn = xn_sc[...]
    u = jnp.dot(xn, wx_ref[...], preferred_element_type=F32)
    gate_sc[...] = _gelu(jnp.dot(xn, wy_ref[...], preferred_element_type=F32))
    conv = _conv_rows(ext_sc, u, st_ref, carry_u, c, first, cw_ref, cb_ref, utail_ref,
                      width=LRU_CONV, shift=shift, d0=d0, tail_end=tail_end)
    lam = lam_ref[...]
    log_sig = jnp.minimum(lam, 0.0) - jnp.log(1.0 + jnp.exp(-jnp.abs(lam)))
    for s in range(wa_ref.shape[0]):
        sl = slice(s * RNN_BD, (s + 1) * RNN_BD)
        cs = conv[:, sl]
        cbf = cs.astype(BF16)
        r = _sigmoid(jnp.dot(cbf, wa_ref[s], preferred_element_type=F32) + ba_ref[:, sl])
        ig = _sigmoid(jnp.dot(cbf, wi_ref[s], preferred_element_type=F32) + bi_ref[:, sl])
        log_a = LRU_C * r * log_sig[:, sl]
        a = jnp.exp(log_a)
        a_sc[:, sl] = a
        b_sc[:, sl] = jnp.sqrt(-jnp.tanh(log_a) * (a * a + 1.0)) * (ig * cs)

    carry_h[c] = _scan_rows(a_sc, b_sc, jnp.where(first, h0_ref[0], carry_h[c]), shift=shift)
    htail_ref[0] = b_sc[tail_end - SUBLANES:tail_end, :]
    tm = a_sc.shape[0]
    nsplit = 4 if tm % (4 * 2 * SUBLANES) == 0 else 1
    for rc in range(nsplit):
        rows = slice(rc * (tm // nsplit), (rc + 1) * (tm // nsplit))
        acc_sc[rows, :] += jnp.dot((b_sc[rows, :] * gate_sc[rows, :]).astype(BF16), wo_ref[...],
                                   preferred_element_type=F32)

    @pl.when(c == pl.num_programs(1) - 1)
    def _():
        out_ref[...] = h_ref[...] + acc_sc[...]


def _lru_block(h, g, wx, wy, cw, cb, wa, ba, wi, bi, lam, wo, h0, state, *,
               layer, tm, shift, tiles_per_seq, tail_end):
    m = h.shape[0]
    d0 = state.shape[1]
    n_tiles = m // tm
    nsub = LRU_BLOCKS_PER_STEP
    bd = nsub * RNN_BD
    nsteps = RNN_BLOCKS // nsub
    row = lambda i, c: (i, 0)
    col = lambda i, c: (0, c)
    return pl.pallas_call(
        functools.partial(_lru_kernel, shift=shift, d0=d0, tiles_per_seq=tiles_per_seq, tail_end=tail_end),
        grid=(n_tiles, nsteps),
        in_specs=[
            pl.BlockSpec((tm, D_MODEL), row),
            pl.BlockSpec((1, D_MODEL), lambda i, c: (0, 0)),
            pl.BlockSpec((None, D_MODEL, bd), lambda i, c: (layer, 0, c)),
            pl.BlockSpec((None, D_MODEL, bd), lambda i, c: (layer, 0, c)),
            pl.BlockSpec((LRU_CONV, bd), col),
            pl.BlockSpec((1, bd), col),
            pl.BlockSpec((None, nsub, RNN_BD, RNN_BD), lambda i, c: (layer, c, 0, 0)),
            pl.BlockSpec((1, bd), col),
            pl.BlockSpec((None, nsub, RNN_BD, RNN_BD), lambda i, c: (layer, c, 0, 0)),
            pl.BlockSpec((1, bd), col),
            pl.BlockSpec((1, bd), col),
            pl.BlockSpec((None, bd, D_MODEL), lambda i, c: (layer, c, 0)),
            pl.BlockSpec((1, SUBLANES, bd), lambda i, c: (i // tiles_per_seq, 0, c)),
            pl.BlockSpec((1, d0, bd), lambda i, c: (i // tiles_per_seq, 0, c)),
        ],
        out_specs=[
            pl.BlockSpec((tm, D_MODEL), row),
            pl.BlockSpec((1, SUBLANES, bd), lambda i, c: (i, 0, c)),
            pl.BlockSpec((1, d0, bd), lambda i, c: (i, 0, c)),
        ],
        out_shape=[jax.ShapeDtypeStruct((m, D_MODEL), F32),
                   jax.ShapeDtypeStruct((n_tiles, SUBLANES, D_MODEL), F32),
                   jax.ShapeDtypeStruct((n_tiles, d0, D_MODEL), F32)],
        scratch_shapes=[
            pltpu.VMEM((tm, D_MODEL), BF16),
            pltpu.VMEM((tm, D_MODEL), F32),
            pltpu.VMEM((d0 + tm, bd), F32),
            pltpu.VMEM((tm, bd), F32),
            pltpu.VMEM((tm, bd), F32),
            pltpu.VMEM((tm, bd), F32),
            pltpu.VMEM((nsteps, d0, bd), F32),
            pltpu.VMEM((nsteps, SUBLANES, bd), F32),
        ],
        compiler_params=pltpu.CompilerParams(
            dimension_semantics=("arbitrary", "arbitrary"), vmem_limit_bytes=VMEM_LIMIT),
        name="rg_lru",
    )(h, g, wx, wy, cw, cb, wa, ba, wi, bi, lam, wo, h0, state)


def _rope_tables(pos, rot, period):
    half = rot // 2
    inv = 1.0 / (ROPE_THETA ** (jnp.arange(half, dtype=F32) * (2.0 / rot)))
    ang = pos.astype(F32)[:, None] * inv[None, :]
    lane = jnp.arange(LANES) % period
    cos = jnp.cos(ang)[:, lane % half]
    sin = jnp.sin(ang)[:, lane % half]
    return jnp.stack([
        jnp.where(lane < rot, cos, 1.0),
        jnp.where(lane < half, -sin, 0.0),
        jnp.where((lane >= half) & (lane < rot), sin, 0.0),
    ])


def _time_major(x):
    return jnp.swapaxes(x, 0, 1)


def kernel(x_prompt, x_sample, cache_k, cache_v, cache_idx_k, state_lru_h, state_lru_conv, state_ffn_conv, page_table, meta_tokens, norm1_g, norm2_g, attn_w_in, attn_q_norm_g, attn_k_norm_g, attn_idx_k_norm_g, attn_w_out, lru_w_x, lru_w_y, lru_conv_w, lru_conv_b, lru_wa, lru_ba, lru_wi, lru_bi, lru_lambda, lru_w_out, ffn_w_up, ffn_w_gate, ffn_dw_w, ffn_dw_b, ffn_w_down):
    batch, seq, _ = x_prompt.shape
    nb, ns, _ = x_sample.shape
    depth = norm1_g.shape[0]
    t_real = seq + N_META
    tp = _round_up(t_real, QB)
    n_pages = page_table.shape[1]
    past = n_pages * PAGE
    topk_p = min(TOPK_MAX, t_real // 4)
    topk_s = min(TOPK_MAX, (past + ns) // 4)
    assert nb == SUBLANES and ns <= SUBLANES and n_pages % S1_PAGES == 0 and n_pages % S2_PAGES == 0

    tiles_per_seq = 6
    tm = tp // tiles_per_seq
    assert tm * tiles_per_seq == tp and tm % (2 * SUBLANES) == 0
    tail_end = t_real - (tiles_per_seq - 1) * tm
    assert tail_end % SUBLANES == 0 and tail_end >= SUBLANES
    tm_proj = 3 * LANES
    assert tp % tm_proj == 0
    ms = nb * ns

    hp = jnp.concatenate([
        jnp.broadcast_to(meta_tokens[None], (batch, N_META, D_MODEL)), x_prompt,
        jnp.zeros((batch, tp - t_real, D_MODEL), F32)], axis=1).reshape(batch * tp, D_MODEL)
    hs = _time_major(x_sample).reshape(ms, D_MODEL)

    rq_p = _rope_tables(jnp.arange(tp), ROT, HEAD_DIM)
    ri_p = _rope_tables(jnp.arange(tp), IDX_ROT, IDX_DIM)
    pos_s = jnp.repeat(past + jnp.arange(ns), nb)
    rq_s = _rope_tables(pos_s, ROT, HEAD_DIM)
    ri_s = _rope_tables(pos_s, IDX_ROT, IDX_DIM)

    cache_k4 = cache_k.reshape(cache_k.shape[:2] + (PAGE * N_KV, HEAD_DIM))
    cache_v4 = cache_v.reshape(cache_v.shape[:2] + (PAGE * N_KV, HEAD_DIM))
    cache_ik_t = jnp.swapaxes(cache_idx_k, 2, 3)

    w_in_t = jnp.pad(jnp.swapaxes(attn_w_in, 1, 2).astype(BF16), ((0, 0), (0, PROJ_WP - PROJ_W), (0, 0)))
    w_out_all = attn_w_out.astype(BF16)
    lru_mats = dict(wx=lru_w_x.astype(BF16), wy=lru_w_y.astype(BF16), wa=lru_wa.astype(BF16),
                    wi=lru_wi.astype(BF16), wo=lru_w_out.astype(BF16))
    ffn_mats = dict(wu=ffn_w_up.astype(BF16), wg=ffn_w_gate.astype(BF16), wd=ffn_w_down.astype(BF16))

    kp, vp, ikp, ksl, vsl, iksl = [], [], [], [], [], []
    hpl, cpl, hsl, csl, fpl, fsl = [], [], [], [], [], []
    last_tiles = jnp.arange(batch) * tiles_per_seq + tiles_per_seq - 1

    for layer in range(depth):
        mi = layer // 2
        g1 = norm1_g[layer][None]
        if layer % 2 == 0:
            qg = attn_q_norm_g[mi][None]
            kg = attn_k_norm_g[mi][None]
            ikg = jnp.pad(attn_idx_k_norm_g[mi], (0, LANES - IDX_DIM))[None]

            q, kf, kb, vf, vb, iq, ikf, ika, ikb, iw = _attn_project(
                hp, g1, w_in_t, mi, qg, kg, ikg, rq_p, ri_p, tm=tm_proj, nseq=batch, rows_real=t_real)
            logit_bound = (1.02 * HEAD_DIM ** 0.5) * jnp.max(jnp.abs(qg)) * jnp.max(jnp.abs(kg))
            hp = _attn_prompt(hp, q, iq, iw, kb, vb, ika, ikb, w_out_all, mi, logit_bound,
                              batch=batch, tp=tp, topk=topk_p)
            kp.append(kf.reshape(batch, t_real, N_KV, HEAD_DIM))
            vp.append(vf.reshape(batch, t_real, N_KV, HEAD_DIM))
            ikp.append(ikf)

            q, kf, kb, vf, vb, iq, ikf, ika, ikb, iw = _attn_project(
                hs, g1, w_in_t, mi, qg, kg, ikg, rq_s, ri_s, tm=ms, nseq=1, rows_real=ms)
            ksl.append(_time_major(kf.reshape(ns, nb, N_KV, HEAD_DIM)))
            vsl.append(_time_major(vf.reshape(ns, nb, N_KV, HEAD_DIM)))
            iksl.append(_time_major(ikf.reshape(ns, nb, IDX_DIM)))

            pad_q = lambda x: jnp.pad(x, [(0, 0)] * (x.ndim - 2) + [(0, SUBLANES - ns), (0, 0)])
            iq_b = pad_q(iq.reshape(ns, nb, IDX_HEADS, IDX_DIM).transpose(1, 2, 0, 3))
            iq_b = iq_b.reshape(nb, IDX_HEADS * SUBLANES, IDX_DIM)
            iw_b = jnp.pad(iw[:, :IDX_HEADS].reshape(ns, nb, IDX_HEADS).transpose(1, 2, 0), ((0, 0), (0, 0), (0, SUBLANES - ns)))
            iww = jnp.broadcast_to(iw_b.reshape(nb, IDX_HEADS * SUBLANES, 1), (nb, IDX_HEADS * SUBLANES, LANES))
            new_page = lambda x: jnp.pad(_time_major(x.reshape(ns, nb, -1)), ((0, 0), (0, PAGE - ns), (0, 0)))
            iknew = jnp.swapaxes(new_page(ika[:, :IDX_DIM]), 1, 2)
            keys, thr = _sample_index(page_table, cache_ik_t, mi, iq_b, iww, iknew, n_new=ns, topk=topk_s)

            q_b = pad_q(q.reshape(ns, nb, N_KV, GROUP, HEAD_DIM).transpose(1, 2, 3, 0, 4))
            qbd = q_b[:, :, :, :, None, :] * jnp.eye(N_KV, dtype=BF16)[None, :, None, None, :, None]
            qbd = qbd.reshape(nb, N_HEADS * SUBLANES, KV_W)
            o = _sample_attend(page_table, cache_k4, cache_v4, mi, qbd, keys, thr,
                               new_page(kb), new_page(vb), n_new=ns)
            o = o.reshape(nb, N_KV, GROUP, SUBLANES, N_KV, HEAD_DIM)
            o = jnp.stack([o[:, n, :, :ns, n, :] for n in range(N_KV)], axis=1)
            o = o.transpose(3, 0, 1, 2, 4).reshape(ms, Q_W).astype(BF16)
            hs = _matmul_res(hs, o, w_out_all, mi, tn=512)
        else:
            lw = dict(
                g=g1, cw=lru_conv_w[mi], cb=lru_conv_b[mi][None], ba=lru_ba[mi][None], bi=lru_bi[mi][None],
                lam=lru_lambda[mi][None], layer=mi, **lru_mats)
            hp, htail, utail = _lru_block(
                hp, h0=jnp.zeros((batch, SUBLANES, D_MODEL), F32),
                state=jnp.zeros((batch, SUBLANES, D_MODEL), F32),
                tm=tm, shift=1, tiles_per_seq=tiles_per_seq, tail_end=tail_end, **lw)
            hpl.append(htail[last_tiles, SUBLANES - 1])
            cpl.append(utail[last_tiles, SUBLANES - (LRU_CONV - 1):])
            d0s = (LRU_CONV - 1) * nb
            hs, htail, utail = _lru_block(
                hs, h0=state_lru_h[mi][None],
                state=_time_major(state_lru_conv[mi]).reshape(1, d0s, D_MODEL),
                tm=ms, shift=nb, tiles_per_seq=1, tail_end=ms, **lw)
            hsl.append(htail[0])
            csl.append(_time_major(utail.reshape(LRU_CONV - 1, nb, D_MODEL)))

        fw = dict(g=norm2_g[layer][None], cw=ffn_dw_w[layer], cb=ffn_dw_b[layer][None], layer=layer, **ffn_mats)
        hp, tail = _conv_ffn(hp, state=jnp.zeros((batch, SUBLANES, D_FF), F32),
                             tm=tm, tf=512, shift=1, tiles_per_seq=tiles_per_seq, tail_end=tail_end, **fw)
        fpl.append(tail[last_tiles, SUBLANES - (FFN_CONV - 1):])
        d0s = (FFN_CONV - 1) * nb
        hs, tail = _conv_ffn(hs, state=_time_major(state_ffn_conv[layer]).reshape(1, d0s, D_FF),
                             tm=ms, tf=512, shift=nb, tiles_per_seq=1, tail_end=ms, **fw)
        fsl.append(_time_major(tail.reshape(FFN_CONV - 1, nb, D_FF)))

    y_prompt = hp.reshape(batch, tp, D_MODEL)[:, N_META:t_real]
    y_sample = _time_major(hs.reshape(ns, nb, D_MODEL))
    return (y_prompt, y_sample, jnp.stack(kp), jnp.stack(vp), jnp.stack(ikp),
            jnp.stack(ksl), jnp.stack(vsl), jnp.stack(iksl),
            jnp.stack(hpl), jnp.stack(cpl), jnp.stack(hsl), jnp.stack(csl),
            jnp.stack(fpl), jnp.stack(fsl))
```

```python
import functools
import struct

import jax
import jax.numpy as jnp
from jax import lax
from jax.experimental import pallas as pl
from jax.experimental.pallas import tpu as pltpu

F32 = jnp.float32
BF16 = jnp.bfloat16
I32 = jnp.int32

D_MODEL = 2048
N_META = 16
N_HEADS = 16
HEAD_DIM = 128
N_KV = 4
GROUP = N_HEADS // N_KV
ROT = HEAD_DIM // 4
IDX_HEADS = 16
IDX_DIM = 64
IDX_ROT = IDX_DIM // 4
ROPE_THETA = 500000.0
TOPK_MAX = 256
PAGE = 128
Q_W = N_HEADS * HEAD_DIM
KV_W = N_KV * HEAD_DIM
IQ_W = IDX_HEADS * IDX_DIM
PROJ_W = Q_W + 2 * KV_W + IQ_W + IDX_DIM + IDX_HEADS
D_FF = 3 * D_MODEL
FFN_CONV = 3
LRU_CONV = 4
LRU_C = 8.0
RNN_BLOCKS = 8
RNN_BD = D_MODEL // RNN_BLOCKS
EPS = 1e-6
NEG = -1e30

LANES = 128
SUBLANES = 8
QB = 128
MXU_COLS = 256
PROJ_WP = -(-PROJ_W // LANES) * LANES
VMEM_LIMIT = 56 * 1024 * 1024

_NT = (((1,), (1,)), ((), ()))
_INT_MIN = -2 ** 31


def _round_up(x, m):
    return -(-x // m) * m


def _rms_rows(x, g):
    ms = jnp.mean(x * x, axis=-1, keepdims=True)
    return x * lax.rsqrt(ms + EPS) * g


def _gelu(x):
    return 0.5 * x * (1.0 + jnp.tanh(0.7978845608028654 * (x + 0.044715 * (x * x * x))))


def _sigmoid(x):
    return 0.5 * jnp.tanh(0.5 * x) + 0.5


def _sort_key(x):
    b = lax.bitcast_convert_type(x + 0.0, I32)
    return b ^ ((b >> 31) & 0x7FFFFFFF)


def _sort_key_of_constant(x):
    b = struct.unpack("<i", struct.pack("<f", x))[0]
    return b ^ ((b >> 31) & 0x7FFFFFFF)


_NEG_KEY = _sort_key_of_constant(NEG)


def _kth_largest_key(count_ge, shape, k, total):
    def bit_body(bi, carry):
        t, cnt_t = carry
        cand = t | jnp.left_shift(jnp.int32(1), 31 - bi)
        cnt = count_ge(cand ^ _INT_MIN)
        keep = cnt >= k
        return jnp.where(keep, cand, t), jnp.where(keep, cnt, cnt_t)
    t, cnt_t = lax.fori_loop(0, 32, bit_body, (jnp.zeros(shape, I32), jnp.full(shape, total, F32)))
    return t ^ _INT_MIN, cnt_t


def _rope(x, tab_ref, sh):
    return (x * tab_ref[0] + pltpu.roll(x, LANES - sh, 1) * tab_ref[1]
            + pltpu.roll(x, sh, 1) * tab_ref[2])


def _proj_kernel(h_ref, g_ref, w_ref, qg_ref, kg_ref, ikg_ref, rq_ref, ri_ref,
                 q_ref, kf_ref, kb_ref, vf_ref, vb_ref, iq_ref, ikf_ref, ika_ref, ikb_ref, iw_ref):
    xn = _rms_rows(h_ref[...], g_ref[...]).astype(BF16)

    def zcols(lo, width=MXU_COLS):
        return lax.dot_general(xn, w_ref[lo:lo + width, :], _NT, preferred_element_type=F32)

    tm = h_ref.shape[0]
    halves = [slice(s * LANES, (s + 1) * LANES) for s in range(MXU_COLS // LANES)]
    for j in range(Q_W // MXU_COLS):
        z = zcols(j * MXU_COLS)
        for sl in halves:
            x = _rope(_rms_rows(z[:, sl], qg_ref[...]), rq_ref, ROT // 2)
            q_ref[:, j * MXU_COLS + sl.start:j * MXU_COLS + sl.stop] = x.astype(BF16)
    for j in range(KV_W // MXU_COLS):
        z = zcols(Q_W + j * MXU_COLS)
        for s, sl in enumerate(halves):
            n = j * len(halves) + s
            x = _rope(_rms_rows(z[:, sl], kg_ref[...]), rq_ref, ROT // 2)
            kf_ref[0, pl.ds(n, tm, stride=N_KV), :] = x
            kb_ref[:, n * HEAD_DIM:(n + 1) * HEAD_DIM] = x.astype(BF16)
        z = zcols(Q_W + KV_W + j * MXU_COLS)
        for s, sl in enumerate(halves):
            n = j * len(halves) + s
            vf_ref[0, pl.ds(n, tm, stride=N_KV), :] = z[:, sl]
        vb_ref[:, j * MXU_COLS:(j + 1) * MXU_COLS] = z.astype(BF16)
    for j in range(IQ_W // MXU_COLS):
        z = zcols(Q_W + 2 * KV_W + j * MXU_COLS)
        for sl in halves:
            x = _rope(z[:, sl], ri_ref, IDX_ROT // 2)
            iq_ref[:, j * MXU_COLS + sl.start:j * MXU_COLS + sl.stop] = x.astype(BF16)

    x = zcols(Q_W + 2 * KV_W + IQ_W, LANES)
    lane = lax.broadcasted_iota(I32, x.shape, 1)
    ms = jnp.sum(jnp.where(lane < IDX_DIM, x * x, 0.0), axis=-1, keepdims=True) * (1.0 / IDX_DIM)
    y = _rope(x * lax.rsqrt(ms + EPS) * ikg_ref[...], ri_ref, IDX_ROT // 2)
    ikf_ref[0] = y[:, :IDX_DIM]
    ika_ref[...] = y.astype(BF16)
    ikb_ref[...] = pltpu.roll(y, IDX_DIM, 1).astype(BF16)
    iw_ref[...] = jnp.where(lane < IDX_HEADS, pltpu.roll(x, LANES - IDX_DIM, 1) * (IDX_HEADS ** -0.5), 0.0)


def _attn_project(h, g, w_t, layer, qg, kg, ikg, rq, ri, *, tm, nseq, rows_real):
    m = h.shape[0]
    tab_tiles = rq.shape[1] // tm
    assert m == nseq * tab_tiles * tm
    row = lambda i: (i, 0)
    const = lambda i: (0, 0)
    tab = lambda i: (0, i % tab_tiles, 0)
    seq_row = lambda i: (i // tab_tiles, i % tab_tiles, 0)
    sds = jax.ShapeDtypeStruct
    return pl.pallas_call(
        _proj_kernel,
        grid=(m // tm,),
        in_specs=[
            pl.BlockSpec((tm, D_MODEL), row),
            pl.BlockSpec((1, D_MODEL), const),
            pl.BlockSpec((None,) + w_t.shape[1:], lambda i: (layer, 0, 0), pipeline_mode=pl.Buffered(1)),
            pl.BlockSpec((1, LANES), const),
            pl.BlockSpec((1, LANES), const),
            pl.BlockSpec((1, LANES), const),
            pl.BlockSpec((3, tm, LANES), tab),
            pl.BlockSpec((3, tm, LANES), tab),
        ],
        out_specs=[
            pl.BlockSpec((tm, Q_W), row),
            pl.BlockSpec((1, tm * N_KV, HEAD_DIM), seq_row),
            pl.BlockSpec((tm, KV_W), row),
            pl.BlockSpec((1, tm * N_KV, HEAD_DIM), seq_row),
            pl.BlockSpec((tm, KV_W), row),
            pl.BlockSpec((tm, IQ_W), row),
            pl.BlockSpec((1, tm, IDX_DIM), seq_row),
            pl.BlockSpec((tm, LANES), row),
            pl.BlockSpec((tm, LANES), row),
            pl.BlockSpec((tm, LANES), row),
        ],
        out_shape=[
            sds((m, Q_W), BF16), sds((nseq, rows_real * N_KV, HEAD_DIM), F32), sds((m, KV_W), BF16),
            sds((nseq, rows_real * N_KV, HEAD_DIM), F32), sds((m, KV_W), BF16), sds((m, IQ_W), BF16),
            sds((nseq, rows_real, IDX_DIM), F32), sds((m, LANES), BF16), sds((m, LANES), BF16),
            sds((m, LANES), F32),
        ],
        compiler_params=pltpu.CompilerParams(
            dimension_semantics=("arbitrary",), vmem_limit_bytes=VMEM_LIMIT),
        name="attn_project",
    )(h, g, w_t, qg, kg, ikg, rq, ri)


def _attn_prompt_kernel(shift_ref, q_ref, iq_ref, iw_ref, k_ref, v_ref, ika_ref, ikb_ref,
                        wout_ref, h_ref, out_ref, keys_sc, iqt_sc, jt_sc, o_sc, m_sc, acc_sc, *,
                        kc, topk, online):
    i = pl.program_id(1)
    q0 = i * QB
    tp = k_ref.shape[0]
    nch = (q0 + QB + kc - 1) // kc
    nlt = kc // LANES
    qcol = q0 + lax.broadcasted_iota(I32, (LANES, QB), 1)
    krow = lax.broadcasted_iota(I32, (LANES, QB), 0)
    for p in range(IDX_HEADS // 2):
        sl = slice(p * LANES, (p + 1) * LANES)
        iqt_sc[:, sl] = iq_ref[:, sl].astype(F32).T.astype(BF16)
    iwt = iw_ref[...].T
    if keys_sc.shape[0] > tp:
        keys_sc[tp:, :] = jnp.full((keys_sc.shape[0] - tp, QB), _INT_MIN, I32)

    def chunk_start(c):
        return pl.multiple_of(jnp.minimum(c * kc, tp - kc), LANES)

    def score_chunk(c, carry):
        k0 = chunk_start(c)
        ka = ika_ref[pl.ds(k0, kc), :]
        kb = ikb_ref[pl.ds(k0, kc), :]
        sc = [jnp.zeros((LANES, QB), F32) for _ in range(nlt)]
        for j in range(IQ_W // MXU_COLS):
            w = iqt_sc[:, j * MXU_COLS:(j + 1) * MXU_COLS]
            sa = jnp.dot(ka, w, preferred_element_type=F32)
            sb = jnp.dot(kb, w, preferred_element_type=F32)
            for half in range(MXU_COLS // LANES):
                p = j * (MXU_COLS // LANES) + half
                cols = slice(half * LANES, (half + 1) * LANES)
                wa = iwt[2 * p:2 * p + 1, :]
                wb = iwt[2 * p + 1:2 * p + 2, :]
                for lt in range(nlt):
                    rows = slice(lt * LANES, (lt + 1) * LANES)
                    sc[lt] = (sc[lt] + wa * jnp.maximum(sa[rows, cols], 0.0)
                              + wb * jnp.maximum(sb[rows, cols], 0.0))
        for lt in range(nlt):
            s = jnp.where(k0 + lt * LANES + krow <= qcol, sc[lt] * (IDX_DIM ** -0.5), NEG)
            keys_sc[pl.ds(k0 + lt * LANES, LANES), :] = _sort_key(s)
        return carry

    lax.fori_loop(0, nch, score_chunk, 0)

    sub8 = lax.broadcasted_iota(I32, (SUBLANES, QB), 0)

    def count_keys(pred):
        def body(c, acc):
            k0 = pl.multiple_of(c * kc, LANES)
            acc = list(acc)
            for j in range(kc // SUBLANES):
                kk = keys_sc[pl.ds(k0 + j * SUBLANES, SUBLANES), :]
                hit = pred(kk, k0 + j * SUBLANES + sub8)
                acc[j % len(acc)] = acc[j % len(acc)] + jnp.where(hit, 1.0, 0.0)
            return tuple(acc)
        acc = lax.fori_loop(0, nch, body, tuple(jnp.zeros((SUBLANES, QB), F32) for _ in range(4)))
        tot = (acc[0] + acc[1]) + (acc[2] + acc[3])
        for s in (4, 2, 1):
            tot = tot + pltpu.roll(tot, s, 0)
        return tot

    def count_ge(cand):
        return count_keys(lambda kk, kidx: kk >= cand)

    k = float(topk)
    thr8, cnt8 = _kth_largest_key(count_ge, (SUBLANES, QB), k, (nch * kc).astype(F32))
    tied = (cnt8 > k) & (thr8 > _NEG_KEY)
    jt_sc[...] = jnp.full((SUBLANES, QB), tp, I32)

    @pl.when(jnp.max(jnp.where(tied, 1.0, 0.0)) > 0.0)
    def _():
        need = k - count_keys(lambda kk, kidx: kk > thr8)
        nbits = (keys_sc.shape[0] - 1).bit_length()

        def bit_body(bi, x):
            cand = x | jnp.left_shift(jnp.int32(1), nbits - 1 - bi)
            below = count_keys(lambda kk, kidx: (kk == thr8) & (kidx < cand))
            return jnp.where(below < need, cand, x)

        x = lax.fori_loop(0, nbits, bit_body, jnp.zeros((SUBLANES, QB), I32))
        jt_sc[...] = jnp.where(tied, x, tp)

    thr = jnp.broadcast_to(thr8[0:1, :], (LANES, QB))
    jt = jnp.broadcast_to(jt_sc[0:1, :], (LANES, QB))

    scale = HEAD_DIM ** -0.5
    unmasked = 0.0 if online else -shift_ref[0, 0]
    acc_sc[...] = jnp.zeros(acc_sc.shape, F32)
    if online:
        m_sc[...] = jnp.full(m_sc.shape, NEG, F32)
    ones_col = jnp.where(lax.broadcasted_iota(I32, (kc, LANES), 1) == 0, 1.0, 0.0).astype(BF16)

    def att_chunk(c, carry):
        k0 = chunk_start(c)
        bias = []
        for lt in range(nlt):
            kk = keys_sc[pl.ds(k0 + lt * LANES, LANES), :]
            kidx = k0 + lt * LANES + krow
            sel = ((kk > thr) | ((kk == thr) & (kidx <= jt))) & (kidx <= qcol) & (kidx >= c * kc)
            bias.append(jnp.where(sel, unmasked, NEG).T)
        bias = jnp.concatenate(bias, axis=1)
        bias = jnp.concatenate([bias] * GROUP, axis=0)
        for n in range(N_KV):
            qs = jnp.concatenate(
                [q_ref[:, (n * GROUP + g) * LANES:(n * GROUP + g + 1) * LANES] for g in range(GROUP)], axis=0)
            kt = k_ref[pl.ds(k0, kc), n * LANES:(n + 1) * LANES]
            vt = jnp.concatenate([v_ref[pl.ds(k0, kc), n * LANES:(n + 1) * LANES], ones_col], axis=1)
            s = lax.dot_general(qs, kt, _NT, preferred_element_type=F32) * scale + bias
            if online:
                m_old = m_sc[n]
                m_new = jnp.maximum(m_old, jnp.max(s, axis=1, keepdims=True))
                p = jnp.exp(s - m_new).astype(BF16)
                acc_sc[n] = jnp.exp(m_old - m_new) * acc_sc[n] + jnp.dot(p, vt, preferred_element_type=F32)
                m_sc[n] = m_new
            else:
                acc_sc[n] += jnp.dot(jnp.exp(s).astype(BF16), vt, preferred_element_type=F32)
        return carry

    lax.fori_loop(0, nch, att_chunk, 0)
    for n in range(N_KV):
        a = acc_sc[n]
        o = a[:, :LANES] / a[:, LANES:LANES + 1]
        for g in range(GROUP):
            hh = n * GROUP + g
            o_sc[:, hh * LANES:(hh + 1) * LANES] = o[g * QB:(g + 1) * QB].astype(BF16)

    out_ref[...] = h_ref[...] + jnp.dot(o_sc[...], wout_ref[...], preferred_element_type=F32)


MAX_LOGIT_BOUND = 40.0


def _attn_prompt(h, q, iq, iw, kb, vb, ika, ikb, wout, layer, logit_bound, *, batch, tp, topk):
    args = (logit_bound.reshape(1, 1), q, iq, iw, kb, vb, ika, ikb, wout, h)
    kc = 3 * LANES
    assert kc >= topk and (tp - kc) % LANES == 0
    kw = dict(layer=layer, batch=batch, tp=tp, kc=kc, topk=topk)
    return lax.cond(logit_bound <= MAX_LOGIT_BOUND,
                    lambda: _attn_prompt_call(*args, online=False, **kw),
                    lambda: _attn_prompt_call(*args, online=True, **kw))


def _attn_prompt_call(shift, q, iq, iw, kb, vb, ika, ikb, wout, h, *, layer, batch, tp, kc, topk, online):
    nqb = tp // QB
    qrow = lambda b, i: (b * nqb + i, 0)
    seq = lambda b, i: (b, 0)
    return pl.pallas_call(
        functools.partial(_attn_prompt_kernel, kc=kc, topk=topk, online=online),
        grid=(batch, nqb),
        in_specs=[
            pl.BlockSpec(memory_space=pltpu.SMEM),
            pl.BlockSpec((QB, Q_W), qrow),
            pl.BlockSpec((QB, IQ_W), qrow),
            pl.BlockSpec((QB, LANES), qrow),
            pl.BlockSpec((tp, KV_W), seq),
            pl.BlockSpec((tp, KV_W), seq),
            pl.BlockSpec((tp, LANES), seq),
            pl.BlockSpec((tp, LANES), seq),
            pl.BlockSpec((None, Q_W, D_MODEL), lambda b, i: (layer, 0, 0)),
            pl.BlockSpec((QB, D_MODEL), qrow),
        ],
        out_specs=pl.BlockSpec((QB, D_MODEL), qrow),
        out_shape=jax.ShapeDtypeStruct(h.shape, F32),
        scratch_shapes=[
            pltpu.VMEM((_round_up(tp, kc), QB), I32),
            pltpu.VMEM((LANES, IQ_W), BF16),
            pltpu.VMEM((SUBLANES, QB), I32),
            pltpu.VMEM((QB, Q_W), BF16),
            pltpu.VMEM((N_KV, GROUP * QB, 1), F32),
            pltpu.VMEM((N_KV, GROUP * QB, 2 * HEAD_DIM), F32),
        ],
        compiler_params=pltpu.CompilerParams(
            dimension_semantics=("arbitrary", "arbitrary"), vmem_limit_bytes=VMEM_LIMIT),
        name="attn_prompt_online" if online else "attn_prompt_bounded",
    )(shift, q, iq, iw, kb, vb, ika, ikb, wout, h)


S1_PAGES = 32
S2_PAGES = 16


def _idx_page_scores(iq, iww, page_t):
    s = jnp.dot(iq, page_t, preferred_element_type=F32)
    sc = jnp.zeros((SUBLANES, PAGE), F32)
    for hh in range(IDX_HEADS):
        sl = slice(hh * SUBLANES, (hh + 1) * SUBLANES)
        sc = sc + iww[sl, :] * jnp.maximum(s[sl, :], 0.0)
    return sc * (IDX_DIM ** -0.5)


def _sample_index_kernel(pt_ref, *refs, past, n_new, topk):
    pages = refs[:S1_PAGES]
    iq_ref, iww_ref, iknew_ref, keys_ref, thr_ref = refs[S1_PAGES:]
    s = pl.program_id(1)
    iq = iq_ref[0]
    iww = iww_ref[0]
    for r in range(S1_PAGES):
        sc = _idx_page_scores(iq, iww, pages[r][...].astype(BF16))
        off = pl.multiple_of((s * S1_PAGES + r) * PAGE, PAGE)
        keys_ref[0, :, pl.ds(off, PAGE)] = _sort_key(sc)

    @pl.when(s == pl.num_programs(1) - 1)
    def _():
        sc = _idx_page_scores(iq, iww, iknew_ref[0])
        t = lax.broadcasted_iota(I32, sc.shape, 0)
        jn = lax.broadcasted_iota(I32, sc.shape, 1)
        sc = jnp.where((jn <= t) & (jn < n_new), sc, NEG)
        keys_ref[0, :, pl.ds(past, PAGE)] = _sort_key(sc)
        nk = keys_ref.shape[2]
        lane = lax.broadcasted_iota(I32, (SUBLANES, LANES), 1)

        def count_keys(pred):
            acc = [jnp.zeros((SUBLANES, LANES), F32) for _ in range(4)]
            for j in range(nk // LANES):
                kk = keys_ref[0, :, j * LANES:(j + 1) * LANES]
                acc[j % 4] = acc[j % 4] + jnp.where(pred(kk, j * LANES + lane), 1.0, 0.0)
            tot = (acc[0] + acc[1]) + (acc[2] + acc[3])
            return jnp.broadcast_to(jnp.sum(tot, axis=1, keepdims=True), (SUBLANES, LANES))

        def count_ge(cand):
            return count_keys(lambda kk, kidx: kk >= cand)

        k = float(topk)
        thr, cnt = _kth_largest_key(count_ge, (SUBLANES, LANES), k, float(nk))
        thr_ref[0, :, :LANES] = thr
        tied = (cnt > k) & (thr > _NEG_KEY)
        thr_ref[0, :, LANES:] = jnp.full((SUBLANES, LANES), nk, I32)

        @pl.when(jnp.max(jnp.where(tied, 1.0, 0.0)) > 0.0)
        def _():
            need = k - count_keys(lambda kk, kidx: kk > thr)
            nbits = (nk - 1).bit_length()

            def bit_body(bi, x):
                cand = x | jnp.left_shift(jnp.int32(1), nbits - 1 - bi)
                below = count_keys(lambda kk, kidx: (kk == thr) & (kidx < cand))
                return jnp.where(below < need, cand, x)

            x = lax.fori_loop(0, nbits, bit_body, jnp.zeros((SUBLANES, LANES), I32))
            thr_ref[0, :, LANES:] = jnp.where(tied, x, nk)


def _sample_index(page_table, cache_ik, layer, iq, iww, iknew, *, n_new, topk):
    nb, n_pages = page_table.shape
    steps = n_pages // S1_PAGES
    nk = (n_pages + 1) * PAGE
    page_spec = lambda r: pl.BlockSpec(
        (None, None, IDX_DIM, PAGE), lambda b, s, pt: (layer, pt[b, s * S1_PAGES + r], 0, 0))
    per_b = lambda b, s, pt: (b, 0, 0)
    return pl.pallas_call(
        functools.partial(_sample_index_kernel, past=n_pages * PAGE, n_new=n_new, topk=topk),
        grid_spec=pltpu.PrefetchScalarGridSpec(
            num_scalar_prefetch=1,
            grid=(nb, steps),
            in_specs=[page_spec(r) for r in range(S1_PAGES)] + [
                pl.BlockSpec((1, IDX_HEADS * SUBLANES, IDX_DIM), per_b),
                pl.BlockSpec((1, IDX_HEADS * SUBLANES, LANES), per_b),
                pl.BlockSpec((1, IDX_DIM, PAGE), per_b),
            ],
            out_specs=[
                pl.BlockSpec((1, SUBLANES, nk), per_b),
                pl.BlockSpec((1, SUBLANES, 2 * LANES), per_b),
            ],
        ),
        out_shape=[jax.ShapeDtypeStruct((nb, SUBLANES, nk), I32),
                   jax.ShapeDtypeStruct((nb, SUBLANES, 2 * LANES), I32)],
        compiler_params=pltpu.CompilerParams(
            dimension_semantics=("arbitrary", "arbitrary"), vmem_limit_bytes=VMEM_LIMIT),
        name="sample_index",
    )(page_table, *([cache_ik] * S1_PAGES), iq, iww, iknew)


def _sample_attend_kernel(pt_ref, *refs, n_new):
    kpages = refs[:S2_PAGES]
    vpages = refs[S2_PAGES:2 * S2_PAGES]
    (qbd_ref, keys_ref, keysnew_ref, thr_ref, knew_ref, vnew_ref, o_ref,
     m_sc, l_sc, acc_sc) = refs[2 * S2_PAGES:]
    s = pl.program_id(1)
    scale = HEAD_DIM ** -0.5
    qbd = qbd_ref[0]
    thr = thr_ref[0, :, :LANES]
    jt = thr_ref[0, :, LANES:]
    lane = lax.broadcasted_iota(I32, (SUBLANES, LANES), 1)

    def selected(kk, kidx):
        return (kk > thr) | ((kk == thr) & (kidx <= jt))

    @pl.when(s == 0)
    def _():
        m_sc[...] = jnp.full(m_sc.shape, NEG, F32)
        l_sc[...] = jnp.zeros(l_sc.shape, F32)
        acc_sc[...] = jnp.zeros(acc_sc.shape, F32)

    def page_update(kp, vp, bias):
        bias = jnp.concatenate([bias] * (N_HEADS), axis=0)
        sc = lax.dot_general(qbd, kp, _NT, preferred_element_type=F32) * scale + bias
        m_old = m_sc[...]
        m_new = jnp.maximum(m_old, jnp.max(sc, axis=1, keepdims=True))
        alpha = jnp.exp(m_old - m_new)
        p = jnp.exp(sc - m_new)
        l_sc[...] = alpha * l_sc[...] + jnp.sum(p, axis=1, keepdims=True)
        acc_sc[...] = alpha * acc_sc[...] + jnp.dot(p.astype(BF16), vp, preferred_element_type=F32)
        m_sc[...] = m_new

    def page_rows(ref):
        return jnp.concatenate(
            [ref[pl.ds(n, PAGE, stride=N_KV), :] for n in range(N_KV)], axis=1).astype(BF16)

    bias = jnp.concatenate(
        [jnp.where(selected(keys_ref[0, :, r * PAGE:(r + 1) * PAGE], (s * S2_PAGES + r) * PAGE + lane), 0.0, NEG)
         for r in range(S2_PAGES)], axis=1)
    page_update(jnp.concatenate([page_rows(kpages[r]) for r in range(S2_PAGES)], axis=0),
                jnp.concatenate([page_rows(vpages[r]) for r in range(S2_PAGES)], axis=0), bias)

    @pl.when(s == pl.num_programs(1) - 1)
    def _():
        kk = keysnew_ref[0]
        t = lax.broadcasted_iota(I32, kk.shape, 0)
        jn = lax.broadcasted_iota(I32, kk.shape, 1)
        past = pl.num_programs(1) * S2_PAGES * PAGE
        page_update(knew_ref[0], vnew_ref[0],
                    jnp.where(selected(kk, past + jn) & (jn <= t) & (jn < n_new), 0.0, NEG))
        o_ref[0] = acc_sc[...] / l_sc[...]


def _sample_attend(page_table, cache_k, cache_v, layer, qbd, keys, thr, knew, vnew, *, n_new):
    nb, n_pages = page_table.shape
    steps = n_pages // S2_PAGES
    page_spec = lambda r: pl.BlockSpec(
        (None, None, PAGE * N_KV, HEAD_DIM), lambda b, s, pt: (layer, pt[b, s * S2_PAGES + r], 0, 0))
    per_b = lambda b, s, pt: (b, 0, 0)
    rows = N_HEADS * SUBLANES
    return pl.pallas_call(
        functools.partial(_sample_attend_kernel, n_new=n_new),
        grid_spec=pltpu.PrefetchScalarGridSpec(
            num_scalar_prefetch=1,
            grid=(nb, steps),
            in_specs=[page_spec(r) for r in range(S2_PAGES)] * 2 + [
                pl.BlockSpec((1, rows, KV_W), per_b),
                pl.BlockSpec((1, SUBLANES, S2_PAGES * PAGE), lambda b, s, pt: (b, 0, s)),
                pl.BlockSpec((1, SUBLANES, PAGE), lambda b, s, pt: (b, 0, n_pages)),
                pl.BlockSpec((1, SUBLANES, 2 * LANES), per_b),
                pl.BlockSpec((1, PAGE, KV_W), per_b),
                pl.BlockSpec((1, PAGE, KV_W), per_b),
            ],
            out_specs=pl.BlockSpec((1, rows, KV_W), per_b),
            scratch_shapes=[
                pltpu.VMEM((rows, 1), F32),
                pltpu.VMEM((rows, 1), F32),
                pltpu.VMEM((rows, KV_W), F32),
            ],
        ),
        out_shape=jax.ShapeDtypeStruct((nb, rows, KV_W), F32),
        compiler_params=pltpu.CompilerParams(
            dimension_semantics=("arbitrary", "arbitrary"), vmem_limit_bytes=VMEM_LIMIT),
        name="sample_attend",
    )(page_table, *([cache_k] * S2_PAGES), *([cache_v] * S2_PAGES), qbd, keys, keys, thr, knew, vnew)


def _matmul_res_kernel(h_ref, x_ref, w_ref, o_ref):
    o_ref[...] = h_ref[...] + jnp.dot(x_ref[...], w_ref[...], preferred_element_type=F32)


def _matmul_res(h, x, w, layer, *, tn):
    m, n = h.shape
    kdim = x.shape[1]
    return pl.pallas_call(
        _matmul_res_kernel,
        grid=(n // tn,),
        in_specs=[pl.BlockSpec((m, tn), lambda j: (0, j)),
                  pl.BlockSpec((m, kdim), lambda j: (0, 0)),
                  pl.BlockSpec((None, kdim, tn), lambda j: (layer, 0, j))],
        out_specs=pl.BlockSpec((m, tn), lambda j: (0, j)),
        out_shape=jax.ShapeDtypeStruct((m, n), F32),
        compiler_params=pltpu.CompilerParams(dimension_semantics=("arbitrary",)),
        name="matmul_residual",
    )(h, x, w)


def _conv_rows(ext_sc, u, init_ref, carry_sc, c, first, cw_ref, cb_ref, tail_ref, *,
               width, shift, d0, tail_end):
    tm = u.shape[0]
    ext_sc[d0:d0 + tm, :] = u
    ext_sc[0:d0, :] = jnp.where(first, init_ref[0], carry_sc[c])

    y = cb_ref[...] + cw_ref[width - 1:width, :] * u
    for j in range(1, width):
        y = y + cw_ref[width - 1 - j:width - j, :] * ext_sc[d0 - j * shift:d0 - j * shift + tm, :]
    carry_sc[c] = ext_sc[tm:tm + d0, :]
    tail_ref[0] = ext_sc[tail_end:tail_end + d0, :]
    return y


def _ffn_kernel(h_ref, g_ref, wu_ref, wg_ref, cw_ref, cb_ref, wd_ref, st_ref, out_ref, tail_ref,
                xn_sc, acc_sc, ext_sc, carry_sc, *, shift, d0, tiles_per_seq, tail_end):
    i = pl.program_id(0)
    c = pl.program_id(1)

    @pl.when(c == 0)
    def _():
        xn_sc[...] = _rms_rows(h_ref[...], g_ref[...]).astype(BF16)
        acc_sc[...] = jnp.zeros(acc_sc.shape, F32)

    xn = xn_sc[...]
    u = jnp.dot(xn, wu_ref[...], preferred_element_type=F32)
    conv = _conv_rows(ext_sc, u, st_ref, carry_sc, c, (i % tiles_per_seq) == 0, cw_ref, cb_ref,
                      tail_ref, width=FFN_CONV, shift=shift, d0=d0, tail_end=tail_end)
    mid = _gelu(conv) * jnp.dot(xn, wg_ref[...], preferred_element_type=F32)
    acc_sc[...] += jnp.dot(mid.astype(BF16), wd_ref[...], preferred_element_type=F32)

    @pl.when(c == pl.num_programs(1) - 1)
    def _():
        out_ref[...] = h_ref[...] + acc_sc[...]


def _conv_ffn(h, g, wu, wg, cw, cb, wd, state, *, layer, tm, tf, shift, tiles_per_seq, tail_end):
    m = h.shape[0]
    d0 = state.shape[1]
    n_tiles, n_chunks = m // tm, D_FF // tf
    row = lambda i, c: (i, 0)
    return pl.pallas_call(
        functools.partial(_ffn_kernel, shift=shift, d0=d0, tiles_per_seq=tiles_per_seq, tail_end=tail_end),
        grid=(n_tiles, n_chunks),
        in_specs=[
            pl.BlockSpec((tm, D_MODEL), row),
            pl.BlockSpec((1, D_MODEL), lambda i, c: (0, 0)),
            pl.BlockSpec((None, D_MODEL, tf), lambda i, c: (layer, 0, c)),
            pl.BlockSpec((None, D_MODEL, tf), lambda i, c: (layer, 0, c)),
            pl.BlockSpec((FFN_CONV, tf), lambda i, c: (0, c)),
            pl.BlockSpec((1, tf), lambda i, c: (0, c)),
            pl.BlockSpec((None, tf, D_MODEL), lambda i, c: (layer, c, 0)),
            pl.BlockSpec((1, d0, tf), lambda i, c: (i // tiles_per_seq, 0, c)),
        ],
        out_specs=[
            pl.BlockSpec((tm, D_MODEL), row),
            pl.BlockSpec((1, d0, tf), lambda i, c: (i, 0, c)),
        ],
        out_shape=[jax.ShapeDtypeStruct((m, D_MODEL), F32),
                   jax.ShapeDtypeStruct((n_tiles, d0, D_FF), F32)],
        scratch_shapes=[
            pltpu.VMEM((tm, D_MODEL), BF16),
            pltpu.VMEM((tm, D_MODEL), F32),
            pltpu.VMEM((d0 + tm, tf), F32),
            pltpu.VMEM((n_chunks, d0, tf), F32),
        ],
        compiler_params=pltpu.CompilerParams(
            dimension_semantics=("arbitrary", "arbitrary"), vmem_limit_bytes=VMEM_LIMIT),
        name="conv_ffn",
    )(h, g, wu, wg, cw, cb, wd, state)


LRU_BLOCKS_PER_STEP = 2

def _scan_rows(a_sc, b_sc, hc, *, shift):
    tm = a_sc.shape[0]
    ngroups = tm // SUBLANES
    row = lax.broadcasted_iota(I32, (SUBLANES, a_sc.shape[1]), 0)

    def body(gi, hc):
        r0 = gi * SUBLANES
        a = a_sc[pl.ds(r0, SUBLANES), :]
        b = b_sc[pl.ds(r0, SUBLANES), :]
        if shift == 1:
            for s in (1, 2, 4):
                keep = row >= s
                a_sh = jnp.where(keep, pltpu.roll(a, s, 0), 1.0)
                b_sh = jnp.where(keep, pltpu.roll(b, s, 0), 0.0)
                b = a * b_sh + b
                a = a * a_sh
        hrows = a * hc + b
        b_sc[pl.ds(r0, SUBLANES), :] = hrows
        if shift == 1:
            return jnp.broadcast_to(hrows[SUBLANES - 1:SUBLANES, :], hrows.shape)
        return hrows

    for gi in range(ngroups):
        hc = body(gi, hc)
    return hc


def _lru_kernel(h_ref, g_ref, wx_ref, wy_ref, cw_ref, cb_ref, wa_ref, ba_ref, wi_ref, bi_ref,
                lam_ref, wo_ref, h0_ref, st_ref, out_ref, htail_ref, utail_ref,
                xn_sc, acc_sc, ext_sc, a_sc, b_sc, gate_sc, carry_u, carry_h, *,
                shift, d0, tiles_per_seq, tail_end):
    i = pl.program_id(0)
    c = pl.program_id(1)
    first = (i % tiles_per_seq) == 0

    @pl.when(c == 0)
    def _():
        xn_sc[...] = _rms_rows(h_ref[...], g_ref[...]).astype(BF16)
        acc_sc[...] = jnp.zeros(acc_sc.shape, F32)

    xn = xn_sc[...]
    u = jnp.dot(xn, wx_ref[...], preferred_element_type=F32)
    gate_sc[...] = _gelu(jnp.dot(xn, wy_ref[...], preferred_element_type=F32))
    conv = _conv_rows(ext_sc, u, st_ref, carry_u, c, first, cw_ref, cb_ref, utail_ref,
                      width=LRU_CONV, shift=shift, d0=d0, tail_end=tail_end)
    lam = lam_ref[...]
    log_sig = jnp.minimum(lam, 0.0) - jnp.log(1.0 + jnp.exp(-jnp.abs(lam)))
    for s in range(wa_ref.shape[0]):
        sl = slice(s * RNN_BD, (s + 1) * RNN_BD)
        cs = conv[:, sl]
        cbf = cs.astype(BF16)
        r = _sigmoid(jnp.dot(cbf, wa_ref[s], preferred_element_type=F32) + ba_ref[:, sl])
        ig = _sigmoid(jnp.dot(cbf, wi_ref[s], preferred_element_type=F32) + bi_ref[:, sl])
        log_a = LRU_C * r * log_sig[:, sl]
        a = jnp.exp(log_a)
        a_sc[:, sl] = a
        b_sc[:, sl] = jnp.sqrt(-jnp.tanh(log_a) * (a * a + 1.0)) * (ig * cs)

    carry_h[c] = _scan_rows(a_sc, b_sc, jnp.where(first, h0_ref[0], carry_h[c]), shift=shift)
    htail_ref[0] = b_sc[tail_end - SUBLANES:tail_end, :]
    tm = a_sc.shape[0]
    nsplit = 4 if tm % (4 * 2 * SUBLANES) == 0 else 1
    for rc in range(nsplit):
        rows = slice(rc * (tm // nsplit), (rc + 1) * (tm // nsplit))
        acc_sc[rows, :] += jnp.dot((b_sc[rows, :] * gate_sc[rows, :]).astype(BF16), wo_ref[...],
                                   preferred_element_type=F32)

    @pl.when(c == pl.num_programs(1) - 1)
    def _():
        out_ref[...] = h_ref[...] + acc_sc[...]


def _lru_block(h, g, wx, wy, cw, cb, wa, ba, wi, bi, lam, wo, h0, state, *,
               layer, tm, shift, tiles_per_seq, tail_end):
    m = h.shape[0]
    d0 = state.shape[1]
    n_tiles = m // tm
    nsub = LRU_BLOCKS_PER_STEP
    bd = nsub * RNN_BD
    nsteps = RNN_BLOCKS // nsub
    row = lambda i, c: (i, 0)
    col = lambda i, c: (0, c)
    return pl.pallas_call(
        functools.partial(_lru_kernel, shift=shift, d0=d0, tiles_per_seq=tiles_per_seq, tail_end=tail_end),
        grid=(n_tiles, nsteps),
        in_specs=[
            pl.BlockSpec((tm, D_MODEL), row),
            pl.BlockSpec((1, D_MODEL), lambda i, c: (0, 0)),
            pl.BlockSpec((None, D_MODEL, bd), lambda i, c: (layer, 0, c)),
            pl.BlockSpec((None, D_MODEL, bd), lambda i, c: (layer, 0, c)),
            pl.BlockSpec((LRU_CONV, bd), col),
            pl.BlockSpec((1, bd), col),
            pl.BlockSpec((None, nsub, RNN_BD, RNN_BD), lambda i, c: (layer, c, 0, 0)),
            pl.BlockSpec((1, bd), col),
            pl.BlockSpec((None, nsub, RNN_BD, RNN_BD), lambda i, c: (layer, c, 0, 0)),
            pl.BlockSpec((1, bd), col),
            pl.BlockSpec((1, bd), col),
            pl.BlockSpec((None, bd, D_MODEL), lambda i, c: (layer, c, 0)),
            pl.BlockSpec((1, SUBLANES, bd), lambda i, c: (i // tiles_per_seq, 0, c)),
            pl.BlockSpec((1, d0, bd), lambda i, c: (i // tiles_per_seq, 0, c)),
        ],
        out_specs=[
            pl.BlockSpec((tm, D_MODEL), row),
            pl.BlockSpec((1, SUBLANES, bd), lambda i, c: (i, 0, c)),
            pl.BlockSpec((1, d0, bd), lambda i, c: (i, 0, c)),
        ],
        out_shape=[jax.ShapeDtypeStruct((m, D_MODEL), F32),
                   jax.ShapeDtypeStruct((n_tiles, SUBLANES, D_MODEL), F32),
                   jax.ShapeDtypeStruct((n_tiles, d0, D_MODEL), F32)],
        scratch_shapes=[
            pltpu.VMEM((tm, D_MODEL), BF16),
            pltpu.VMEM((tm, D_MODEL), F32),
            pltpu.VMEM((d0 + tm, bd), F32),
            pltpu.VMEM((tm, bd), F32),
            pltpu.VMEM((tm, bd), F32),
            pltpu.VMEM((tm, bd), F32),
            pltpu.VMEM((nsteps, d0, bd), F32),
            pltpu.VMEM((nsteps, SUBLANES, bd), F32),
        ],
        compiler_params=pltpu.CompilerParams(
            dimension_semantics=("arbitrary", "arbitrary"), vmem_limit_bytes=VMEM_LIMIT),
        name="rg_lru",
    )(h, g, wx, wy, cw, cb, wa, ba, wi, bi, lam, wo, h0, state)


def _rope_tables(pos, rot, period):
    half = rot // 2
    inv = 1.0 / (ROPE_THETA ** (jnp.arange(half, dtype=F32) * (2.0 / rot)))
    ang = pos.astype(F32)[:, None] * inv[None, :]
    lane = jnp.arange(LANES) % period
    cos = jnp.cos(ang)[:, lane % half]
    sin = jnp.sin(ang)[:, lane % half]
    return jnp.stack([
        jnp.where(lane < rot, cos, 1.0),
        jnp.where(lane < half, -sin, 0.0),
        jnp.where((lane >= half) & (lane < rot), sin, 0.0),
    ])


def _time_major(x):
    return jnp.swapaxes(x, 0, 1)


def kernel(x_prompt, x_sample, cache_k, cache_v, cache_idx_k, state_lru_h, state_lru_conv, state_ffn_conv, page_table, meta_tokens, norm1_g, norm2_g, attn_w_in, attn_q_norm_g, attn_k_norm_g, attn_idx_k_norm_g, attn_w_out, lru_w_x, lru_w_y, lru_conv_w, lru_conv_b, lru_wa, lru_ba, lru_wi, lru_bi, lru_lambda, lru_w_out, ffn_w_up, ffn_w_gate, ffn_dw_w, ffn_dw_b, ffn_w_down):
    batch, seq, _ = x_prompt.shape
    nb, ns, _ = x_sample.shape
    depth = norm1_g.shape[0]
    t_real = seq + N_META
    tp = _round_up(t_real, QB)
    n_pages = page_table.shape[1]
    past = n_pages * PAGE
    topk_p = min(TOPK_MAX, t_real // 4)
    topk_s = min(TOPK_MAX, (past + ns) // 4)
    assert nb == SUBLANES and ns <= SUBLANES and n_pages % S1_PAGES == 0 and n_pages % S2_PAGES == 0

    tiles_per_seq = 6
    tm = tp // tiles_per_seq
    assert tm * tiles_per_seq == tp and tm % (2 * SUBLANES) == 0
    tail_end = t_real - (tiles_per_seq - 1) * tm
    assert tail_end % SUBLANES == 0 and tail_end >= SUBLANES
    tm_proj = 3 * LANES
    assert tp % tm_proj == 0
    ms = nb * ns

    hp = jnp.concatenate([
        jnp.broadcast_to(meta_tokens[None], (batch, N_META, D_MODEL)), x_prompt,
        jnp.zeros((batch, tp - t_real, D_MODEL), F32)], axis=1).reshape(batch * tp, D_MODEL)
    hs = _time_major(x_sample).reshape(ms, D_MODEL)

    rq_p = _rope_tables(jnp.arange(tp), ROT, HEAD_DIM)
    ri_p = _rope_tables(jnp.arange(tp), IDX_ROT, IDX_DIM)
    pos_s = jnp.repeat(past + jnp.arange(ns), nb)
    rq_s = _rope_tables(pos_s, ROT, HEAD_DIM)
    ri_s = _rope_tables(pos_s, IDX_ROT, IDX_DIM)

    cache_k4 = cache_k.reshape(cache_k.shape[:2] + (PAGE * N_KV, HEAD_DIM))
    cache_v4 = cache_v.reshape(cache_v.shape[:2] + (PAGE * N_KV, HEAD_DIM))
    cache_ik_t = jnp.swapaxes(cache_idx_k, 2, 3)

    w_in_t = jnp.pad(jnp.swapaxes(attn_w_in, 1, 2).astype(BF16), ((0, 0), (0, PROJ_WP - PROJ_W), (0, 0)))
    w_out_all = attn_w_out.astype(BF16)
    lru_mats = dict(wx=lru_w_x.astype(BF16), wy=lru_w_y.astype(BF16), wa=lru_wa.astype(BF16),
                    wi=lru_wi.astype(BF16), wo=lru_w_out.astype(BF16))
    ffn_mats = dict(wu=ffn_w_up.astype(BF16), wg=ffn_w_gate.astype(BF16), wd=ffn_w_down.astype(BF16))

    kp, vp, ikp, ksl, vsl, iksl = [], [], [], [], [], []
    hpl, cpl, hsl, csl, fpl, fsl = [], [], [], [], [], []
    last_tiles = jnp.arange(batch) * tiles_per_seq + tiles_per_seq - 1

    for layer in range(depth):
        mi = layer // 2
        g1 = norm1_g[layer][None]
        if layer % 2 == 0:
            qg = attn_q_norm_g[mi][None]
            kg = attn_k_norm_g[mi][None]
            ikg = jnp.pad(attn_idx_k_norm_g[mi], (0, LANES - IDX_DIM))[None]

            q, kf, kb, vf, vb, iq, ikf, ika, ikb, iw = _attn_project(
                hp, g1, w_in_t, mi, qg, kg, ikg, rq_p, ri_p, tm=tm_proj, nseq=batch, rows_real=t_real)
            logit_bound = (1.02 * HEAD_DIM ** 0.5) * jnp.max(jnp.abs(qg)) * jnp.max(jnp.abs(kg))
            hp = _attn_prompt(hp, q, iq, iw, kb, vb, ika, ikb, w_out_all, mi, logit_bound,
                              batch=batch, tp=tp, topk=topk_p)
            kp.append(kf.reshape(batch, t_real, N_KV, HEAD_DIM))
            vp.append(vf.reshape(batch, t_real, N_KV, HEAD_DIM))
            ikp.append(ikf)

            q, kf, kb, vf, vb, iq, ikf, ika, ikb, iw = _attn_project(
                hs, g1, w_in_t, mi, qg, kg, ikg, rq_s, ri_s, tm=ms, nseq=1, rows_real=ms)
            ksl.append(_time_major(kf.reshape(ns, nb, N_KV, HEAD_DIM)))
            vsl.append(_time_major(vf.reshape(ns, nb, N_KV, HEAD_DIM)))
            iksl.append(_time_major(ikf.reshape(ns, nb, IDX_DIM)))

            pad_q = lambda x: jnp.pad(x, [(0, 0)] * (x.ndim - 2) + [(0, SUBLANES - ns), (0, 0)])
            iq_b = pad_q(iq.reshape(ns, nb, IDX_HEADS, IDX_DIM).transpose(1, 2, 0, 3))
            iq_b = iq_b.reshape(nb, IDX_HEADS * SUBLANES, IDX_DIM)
            iw_b = jnp.pad(iw[:, :IDX_HEADS].reshape(ns, nb, IDX_HEADS).transpose(1, 2, 0), ((0, 0), (0, 0), (0, SUBLANES - ns)))
            iww = jnp.broadcast_to(iw_b.reshape(nb, IDX_HEADS * SUBLANES, 1), (nb, IDX_HEADS * SUBLANES, LANES))
            new_page = lambda x: jnp.pad(_time_major(x.reshape(ns, nb, -1)), ((0, 0), (0, PAGE - ns), (0, 0)))
            iknew = jnp.swapaxes(new_page(ika[:, :IDX_DIM]), 1, 2)
            keys, thr = _sample_index(page_table, cache_ik_t, mi, iq_b, iww, iknew, n_new=ns, topk=topk_s)

            q_b = pad_q(q.reshape(ns, nb, N_KV, GROUP, HEAD_DIM).transpose(1, 2, 3, 0, 4))
            qbd = q_b[:, :, :, :, None, :] * jnp.eye(N_KV, dtype=BF16)[None, :, None, None, :, None]
            qbd = qbd.reshape(nb, N_HEADS * SUBLANES, KV_W)
            o = _sample_attend(page_table, cache_k4, cache_v4, mi, qbd, keys, thr,
                               new_page(kb), new_page(vb), n_new=ns)
            o = o.reshape(nb, N_KV, GROUP, SUBLANES, N_KV, HEAD_DIM)
            o = jnp.stack([o[:, n, :, :ns, n, :] for n in range(N_KV)], axis=1)
            o = o.transpose(3, 0, 1, 2, 4).reshape(ms, Q_W).astype(BF16)
            hs = _matmul_res(hs, o, w_out_all, mi, tn=512)
        else:
            lw = dict(
                g=g1, cw=lru_conv_w[mi], cb=lru_conv_b[mi][None], ba=lru_ba[mi][None], bi=lru_bi[mi][None],
                lam=lru_lambda[mi][None], layer=mi, **lru_mats)
            hp, htail, utail = _lru_block(
                hp, h0=jnp.zeros((batch, SUBLANES, D_MODEL), F32),
                state=jnp.zeros((batch, SUBLANES, D_MODEL), F32),
                tm=tm, shift=1, tiles_per_seq=tiles_per_seq, tail_end=tail_end, **lw)
            hpl.append(htail[last_tiles, SUBLANES - 1])
            cpl.append(utail[last_tiles, SUBLANES - (LRU_CONV - 1):])
            d0s = (LRU_CONV - 1) * nb
            hs, htail, utail = _lru_block(
                hs, h0=state_lru_h[mi][None],
                state=_time_major(state_lru_conv[mi]).reshape(1, d0s, D_MODEL),
                tm=ms, shift=nb, tiles_per_seq=1, tail_end=ms, **lw)
            hsl.append(htail[0])
            csl.append(_time_major(utail.reshape(LRU_CONV - 1, nb, D_MODEL)))

        fw = dict(g=norm2_g[layer][None], cw=ffn_dw_w[layer], cb=ffn_dw_b[layer][None], layer=layer, **ffn_mats)
        hp, tail = _conv_ffn(hp, state=jnp.zeros((batch, SUBLANES, D_FF), F32),
                             tm=tm, tf=512, shift=1, tiles_per_seq=tiles_per_seq, tail_end=tail_end, **fw)
        fpl.append(tail[last_tiles, SUBLANES - (FFN_CONV - 1):])
        d0s = (FFN_CONV - 1) * nb
        hs, tail = _conv_ffn(hs, state=_time_major(state_ffn_conv[layer]).reshape(1, d0s, D_FF),
                             tm=ms, tf=512, shift=nb, tiles_per_seq=1, tail_end=ms, **fw)
        fsl.append(_time_major(tail.reshape(FFN_CONV - 1, nb, D_FF)))

    y_prompt = hp.reshape(batch, tp, D_MODEL)[:, N_META:t_real]
    y_sample = _time_major(hs.reshape(ns, nb, D_MODEL))
    return (y_prompt, y_sample, jnp.stack(kp), jnp.stack(vp), jnp.stack(ikp),
            jnp.stack(ksl), jnp.stack(vsl), jnp.stack(iksl),
            jnp.stack(hpl), jnp.stack(cpl), jnp.stack(hsl), jnp.stack(csl),
            jnp.stack(fpl), jnp.stack(fsl))
```

```python
import functools

import jax
import jax.numpy as jnp
from jax import lax
from jax.experimental import pallas as pl
from jax.experimental.pallas import tpu as pltpu

F32 = jnp.float32
BF16 = jnp.bfloat16
I32 = jnp.int32

D_MODEL = 2048
N_META = 16
N_HEADS = 16
HEAD_DIM = 128
N_KV = 4
GROUP = N_HEADS // N_KV
ROT = HEAD_DIM // 4
IDX_HEADS = 16
IDX_DIM = 64
IDX_ROT = IDX_DIM // 4
ROPE_THETA = 500000.0
TOPK_MAX = 256
PAGE = 128
Q_W = N_HEADS * HEAD_DIM
KV_W = N_KV * HEAD_DIM
IQ_W = IDX_HEADS * IDX_DIM
PROJ_W = Q_W + 2 * KV_W + IQ_W + IDX_DIM + IDX_HEADS
D_FF = 3 * D_MODEL
FFN_CONV = 3
LRU_CONV = 4
LRU_C = 8.0
RNN_BLOCKS = 8
RNN_BD = D_MODEL // RNN_BLOCKS
EPS = 1e-6
NEG = -1e30

LANES = 128
SUBLANES = 8
QB = 128
MXU_COLS = 256
PROJ_WP = -(-PROJ_W // LANES) * LANES
VMEM_LIMIT = 56 * 1024 * 1024

_NT = (((1,), (1,)), ((), ()))
_INT_MIN = -2 ** 31


def _round_up(x, m):
    return -(-x // m) * m


def _rms_rows(x, g):
    ms = jnp.mean(x * x, axis=-1, keepdims=True)
    return x * lax.rsqrt(ms + EPS) * g


def _gelu(x):
    return 0.5 * x * (1.0 + jnp.tanh(0.7978845608028654 * (x + 0.044715 * (x * x * x))))


def _sigmoid(x):
    return 0.5 * jnp.tanh(0.5 * x) + 0.5


def _ordered_bits_to_float(u):
    key = u ^ _INT_MIN
    b = key ^ ((key >> 31) & 0x7FFFFFFF)
    return lax.bitcast_convert_type(b, F32)


def _kth_largest(count_ge, shape, k, total):
    def bit_body(bi, carry):
        u, cnt_u = carry
        cand = u | jnp.left_shift(jnp.int32(1), 31 - bi)
        cnt = count_ge(_ordered_bits_to_float(cand))
        keep = cnt >= k
        return jnp.where(keep, cand, u), jnp.where(keep, cnt, cnt_u)
    u, cnt_u = lax.fori_loop(0, 32, bit_body, (jnp.zeros(shape, I32), jnp.full(shape, total, F32)))
    return _ordered_bits_to_float(u), cnt_u


def _rope(x, tab_ref, sh):
    return (x * tab_ref[0] + pltpu.roll(x, LANES - sh, 1) * tab_ref[1]
            + pltpu.roll(x, sh, 1) * tab_ref[2])


def _proj_kernel(h_ref, g_ref, w_ref, qg_ref, kg_ref, ikg_ref, rq_ref, ri_ref,
                 q_ref, kf_ref, kb_ref, vf_ref, vb_ref, iq_ref, ikf_ref, ika_ref, ikb_ref, iw_ref):
    xn = _rms_rows(h_ref[...], g_ref[...]).astype(BF16)

    def zcols(lo, width=MXU_COLS):
        return lax.dot_general(xn, w_ref[lo:lo + width, :], _NT, preferred_element_type=F32)

    tm = h_ref.shape[0]
    halves = [slice(s * LANES, (s + 1) * LANES) for s in range(MXU_COLS // LANES)]
    for j in range(Q_W // MXU_COLS):
        z = zcols(j * MXU_COLS)
        for sl in halves:
            x = _rope(_rms_rows(z[:, sl], qg_ref[...]), rq_ref, ROT // 2)
            q_ref[:, j * MXU_COLS + sl.start:j * MXU_COLS + sl.stop] = x.astype(BF16)
    for j in range(KV_W // MXU_COLS):
        z = zcols(Q_W + j * MXU_COLS)
        for s, sl in enumerate(halves):
            n = j * len(halves) + s
            x = _rope(_rms_rows(z[:, sl], kg_ref[...]), rq_ref, ROT // 2)
            kf_ref[0, pl.ds(n, tm, stride=N_KV), :] = x
            kb_ref[:, n * HEAD_DIM:(n + 1) * HEAD_DIM] = x.astype(BF16)
        z = zcols(Q_W + KV_W + j * MXU_COLS)
        for s, sl in enumerate(halves):
            n = j * len(halves) + s
            vf_ref[0, pl.ds(n, tm, stride=N_KV), :] = z[:, sl]
        vb_ref[:, j * MXU_COLS:(j + 1) * MXU_COLS] = z.astype(BF16)
    for j in range(IQ_W // MXU_COLS):
        z = zcols(Q_W + 2 * KV_W + j * MXU_COLS)
        for sl in halves:
            x = _rope(z[:, sl], ri_ref, IDX_ROT // 2)
            iq_ref[:, j * MXU_COLS + sl.start:j * MXU_COLS + sl.stop] = x.astype(BF16)

    x = zcols(Q_W + 2 * KV_W + IQ_W, LANES)
    lane = lax.broadcasted_iota(I32, x.shape, 1)
    ms = jnp.sum(jnp.where(lane < IDX_DIM, x * x, 0.0), axis=-1, keepdims=True) * (1.0 / IDX_DIM)
    y = _rope(x * lax.rsqrt(ms + EPS) * ikg_ref[...], ri_ref, IDX_ROT // 2)
    ikf_ref[0] = y[:, :IDX_DIM]
    ika_ref[...] = y.astype(BF16)
    ikb_ref[...] = pltpu.roll(y, IDX_DIM, 1).astype(BF16)
    iw_ref[...] = jnp.where(lane < IDX_HEADS, pltpu.roll(x, LANES - IDX_DIM, 1) * (IDX_HEADS ** -0.5), 0.0)


def _attn_project(h, g, w_t, layer, qg, kg, ikg, rq, ri, *, tm, nseq, rows_real):
    m = h.shape[0]
    tab_tiles = rq.shape[1] // tm
    assert m == nseq * tab_tiles * tm
    row = lambda i: (i, 0)
    const = lambda i: (0, 0)
    tab = lambda i: (0, i % tab_tiles, 0)
    seq_row = lambda i: (i // tab_tiles, i % tab_tiles, 0)
    sds = jax.ShapeDtypeStruct
    return pl.pallas_call(
        _proj_kernel,
        grid=(m // tm,),
        in_specs=[
            pl.BlockSpec((tm, D_MODEL), row),
            pl.BlockSpec((1, D_MODEL), const),
            pl.BlockSpec((None,) + w_t.shape[1:], lambda i: (layer, 0, 0), pipeline_mode=pl.Buffered(1)),
            pl.BlockSpec((1, LANES), const),
            pl.BlockSpec((1, LANES), const),
            pl.BlockSpec((1, LANES), const),
            pl.BlockSpec((3, tm, LANES), tab),
            pl.BlockSpec((3, tm, LANES), tab),
        ],
        out_specs=[
            pl.BlockSpec((tm, Q_W), row),
            pl.BlockSpec((1, tm * N_KV, HEAD_DIM), seq_row),
            pl.BlockSpec((tm, KV_W), row),
            pl.BlockSpec((1, tm * N_KV, HEAD_DIM), seq_row),
            pl.BlockSpec((tm, KV_W), row),
            pl.BlockSpec((tm, IQ_W), row),
            pl.BlockSpec((1, tm, IDX_DIM), seq_row),
            pl.BlockSpec((tm, LANES), row),
            pl.BlockSpec((tm, LANES), row),
            pl.BlockSpec((tm, LANES), row),
        ],
        out_shape=[
            sds((m, Q_W), BF16), sds((nseq, rows_real * N_KV, HEAD_DIM), F32), sds((m, KV_W), BF16),
            sds((nseq, rows_real * N_KV, HEAD_DIM), F32), sds((m, KV_W), BF16), sds((m, IQ_W), BF16),
            sds((nseq, rows_real, IDX_DIM), F32), sds((m, LANES), BF16), sds((m, LANES), BF16),
            sds((m, LANES), F32),
        ],
        compiler_params=pltpu.CompilerParams(
            dimension_semantics=("arbitrary",), vmem_limit_bytes=VMEM_LIMIT),
        name="attn_project",
    )(h, g, w_t, qg, kg, ikg, rq, ri)


def _attn_prompt_kernel(shift_ref, q_ref, iq_ref, iw_ref, k_ref, v_ref, ika_ref, ikb_ref,
                        wout_ref, h_ref, out_ref, keys_sc, iqt_sc, jt_sc, o_sc, m_sc, acc_sc, *,
                        kc, topk, online):
    i = pl.program_id(1)
    q0 = i * QB
    tp = k_ref.shape[0]
    nch = (q0 + QB + kc - 1) // kc
    nlt = kc // LANES
    qcol = q0 + lax.broadcasted_iota(I32, (LANES, QB), 1)
    krow = lax.broadcasted_iota(I32, (LANES, QB), 0)
    for p in range(IDX_HEADS // 2):
        sl = slice(p * LANES, (p + 1) * LANES)
        iqt_sc[:, sl] = iq_ref[:, sl].astype(F32).T.astype(BF16)
    iwt = iw_ref[...].T
    if keys_sc.shape[0] > tp:
        keys_sc[tp:, :] = jnp.full((keys_sc.shape[0] - tp, QB), NEG, F32)

    def chunk_start(c):
        return pl.multiple_of(jnp.minimum(c * kc, tp - kc), LANES)

    def score_chunk(c, carry):
        k0 = chunk_start(c)
        ka = ika_ref[pl.ds(k0, kc), :]
        kb = ikb_ref[pl.ds(k0, kc), :]
        sc = [jnp.zeros((LANES, QB), F32) for _ in range(nlt)]
        for j in range(IQ_W // MXU_COLS):
            w = iqt_sc[:, j * MXU_COLS:(j + 1) * MXU_COLS]
            sa = jnp.dot(ka, w, preferred_element_type=F32)
            sb = jnp.dot(kb, w, preferred_element_type=F32)
            for half in range(MXU_COLS // LANES):
                p = j * (MXU_COLS // LANES) + half
                cols = slice(half * LANES, (half + 1) * LANES)
                wa = iwt[2 * p:2 * p + 1, :]
                wb = iwt[2 * p + 1:2 * p + 2, :]
                for lt in range(nlt):
                    rows = slice(lt * LANES, (lt + 1) * LANES)
                    sc[lt] = (sc[lt] + wa * jnp.maximum(sa[rows, cols], 0.0)
                              + wb * jnp.maximum(sb[rows, cols], 0.0))
        for lt in range(nlt):
            s = jnp.where(k0 + lt * LANES + krow <= qcol, sc[lt] * (IDX_DIM ** -0.5), NEG)
            keys_sc[pl.ds(k0 + lt * LANES, LANES), :] = s
        return carry

    lax.fori_loop(0, nch, score_chunk, 0)

    sub8 = lax.broadcasted_iota(I32, (SUBLANES, QB), 0)

    def count_keys(pred):
        def body(c, acc):
            k0 = pl.multiple_of(c * kc, LANES)
            acc = list(acc)
            for j in range(kc // SUBLANES):
                kk = keys_sc[pl.ds(k0 + j * SUBLANES, SUBLANES), :]
                hit = pred(kk, k0 + j * SUBLANES + sub8)
                acc[j % len(acc)] = acc[j % len(acc)] + jnp.where(hit, 1.0, 0.0)
            return tuple(acc)
        acc = lax.fori_loop(0, nch, body, tuple(jnp.zeros((SUBLANES, QB), F32) for _ in range(4)))
        tot = (acc[0] + acc[1]) + (acc[2] + acc[3])
        for s in (4, 2, 1):
            tot = tot + pltpu.roll(tot, s, 0)
        return tot

    def count_ge(cand):
        return count_keys(lambda kk, kidx: kk >= cand)

    k = float(topk)
    thr8, cnt8 = _kth_largest(count_ge, (SUBLANES, QB), k, (nch * kc).astype(F32))
    tied = (cnt8 > k) & (thr8 > NEG)
    jt_sc[...] = jnp.full((SUBLANES, QB), tp, I32)

    @pl.when(jnp.max(jnp.where(tied, 1.0, 0.0)) > 0.0)
    def _():
        need = k - count_keys(lambda kk, kidx: kk > thr8)
        nbits = (keys_sc.shape[0] - 1).bit_length()

        def bit_body(bi, x):
            cand = x | jnp.left_shift(jnp.int32(1), nbits - 1 - bi)
            below = count_keys(lambda kk, kidx: (kk == thr8) & (kidx < cand))
            return jnp.where(below < need, cand, x)

        x = lax.fori_loop(0, nbits, bit_body, jnp.zeros((SUBLANES, QB), I32))
        jt_sc[...] = jnp.where(tied, x, tp)

    thr = jnp.broadcast_to(thr8[0:1, :], (LANES, QB))
    jt = jnp.broadcast_to(jt_sc[0:1, :], (LANES, QB))

    scale = HEAD_DIM ** -0.5
    unmasked = 0.0 if online else -shift_ref[0, 0]
    acc_sc[...] = jnp.zeros(acc_sc.shape, F32)
    if online:
        m_sc[...] = jnp.full(m_sc.shape, NEG, F32)
    ones_col = jnp.where(lax.broadcasted_iota(I32, (kc, LANES), 1) == 0, 1.0, 0.0).astype(BF16)

    def att_chunk(c, carry):
        k0 = chunk_start(c)
        bias = []
        for lt in range(nlt):
            kk = keys_sc[pl.ds(k0 + lt * LANES, LANES), :]
            kidx = k0 + lt * LANES + krow
            sel = ((kk > thr) | ((kk == thr) & (kidx <= jt))) & (kidx <= qcol) & (kidx >= c * kc)
            bias.append(jnp.where(sel, unmasked, NEG).T)
        bias = jnp.concatenate(bias, axis=1)
        bias = jnp.concatenate([bias] * GROUP, axis=0)
        for n in range(N_KV):
            qs = jnp.concatenate(
                [q_ref[:, (n * GROUP + g) * LANES:(n * GROUP + g + 1) * LANES] for g in range(GROUP)], axis=0)
            kt = k_ref[pl.ds(k0, kc), n * LANES:(n + 1) * LANES]
            vt = jnp.concatenate([v_ref[pl.ds(k0, kc), n * LANES:(n + 1) * LANES], ones_col], axis=1)
            s = lax.dot_general(qs, kt, _NT, preferred_element_type=F32) * scale + bias
            if online:
                m_old = m_sc[n]
                m_new = jnp.maximum(m_old, jnp.max(s, axis=1, keepdims=True))
                p = jnp.exp(s - m_new).astype(BF16)
                acc_sc[n] = jnp.exp(m_old - m_new) * acc_sc[n] + jnp.dot(p, vt, preferred_element_type=F32)
                m_sc[n] = m_new
            else:
                acc_sc[n] += jnp.dot(jnp.exp(s).astype(BF16), vt, preferred_element_type=F32)
        return carry

    lax.fori_loop(0, nch, att_chunk, 0)
    for n in range(N_KV):
        a = acc_sc[n]
        o = a[:, :LANES] * (1.0 / a[:, LANES:LANES + 1])
        for g in range(GROUP):
            hh = n * GROUP + g
            o_sc[:, hh * LANES:(hh + 1) * LANES] = o[g * QB:(g + 1) * QB].astype(BF16)

    out_ref[...] = h_ref[...] + jnp.dot(o_sc[...], wout_ref[...], preferred_element_type=F32)


MAX_LOGIT_BOUND = 40.0


def _attn_prompt(h, q, iq, iw, kb, vb, ika, ikb, wout, layer, logit_bound, *, batch, tp, topk):
    args = (logit_bound.reshape(1, 1), q, iq, iw, kb, vb, ika, ikb, wout, h)
    kc = 3 * LANES
    assert kc >= topk and (tp - kc) % LANES == 0
    kw = dict(layer=layer, batch=batch, tp=tp, kc=kc, topk=topk)
    return lax.cond(logit_bound <= MAX_LOGIT_BOUND,
                    lambda: _attn_prompt_call(*args, online=False, **kw),
                    lambda: _attn_prompt_call(*args, online=True, **kw))


def _attn_prompt_call(shift, q, iq, iw, kb, vb, ika, ikb, wout, h, *, layer, batch, tp, kc, topk, online):
    nqb = tp // QB
    qrow = lambda b, i: (b * nqb + i, 0)
    seq = lambda b, i: (b, 0)
    return pl.pallas_call(
        functools.partial(_attn_prompt_kernel, kc=kc, topk=topk, online=online),
        grid=(batch, nqb),
        in_specs=[
            pl.BlockSpec(memory_space=pltpu.SMEM),
            pl.BlockSpec((QB, Q_W), qrow),
            pl.BlockSpec((QB, IQ_W), qrow),
            pl.BlockSpec((QB, LANES), qrow),
            pl.BlockSpec((tp, KV_W), seq),
            pl.BlockSpec((tp, KV_W), seq),
            pl.BlockSpec((tp, LANES), seq),
            pl.BlockSpec((tp, LANES), seq),
            pl.BlockSpec((None, Q_W, D_MODEL), lambda b, i: (layer, 0, 0)),
            pl.BlockSpec((QB, D_MODEL), qrow),
        ],
        out_specs=pl.BlockSpec((QB, D_MODEL), qrow),
        out_shape=jax.ShapeDtypeStruct(h.shape, F32),
        scratch_shapes=[
            pltpu.VMEM((_round_up(tp, kc), QB), F32),
            pltpu.VMEM((LANES, IQ_W), BF16),
            pltpu.VMEM((SUBLANES, QB), I32),
            pltpu.VMEM((QB, Q_W), BF16),
            pltpu.VMEM((N_KV, GROUP * QB, 1), F32),
            pltpu.VMEM((N_KV, GROUP * QB, 2 * HEAD_DIM), F32),
        ],
        compiler_params=pltpu.CompilerParams(
            dimension_semantics=("arbitrary", "arbitrary"), vmem_limit_bytes=VMEM_LIMIT),
        name="attn_prompt_online" if online else "attn_prompt_bounded",
    )(shift, q, iq, iw, kb, vb, ika, ikb, wout, h)


S1_PAGES = 64
S2_PAGES = 32


def _idx_page_scores(iq, iww, page_t):
    s = jnp.dot(iq, page_t, preferred_element_type=F32)
    sc = jnp.zeros((SUBLANES, PAGE), F32)
    for hh in range(IDX_HEADS):
        sl = slice(hh * SUBLANES, (hh + 1) * SUBLANES)
        sc = sc + iww[sl, :] * jnp.maximum(s[sl, :], 0.0)
    return sc * (IDX_DIM ** -0.5)


def _sample_index_kernel(pt_ref, *refs, past, n_new, topk):
    pages = refs[:S1_PAGES]
    iq_ref, iww_ref, iknew_ref, keys_ref, thr_ref = refs[S1_PAGES:]
    s = pl.program_id(1)
    iq = iq_ref[0]
    iww = iww_ref[0]
    for r in range(S1_PAGES):
        sc = _idx_page_scores(iq, iww, pages[r][...].astype(BF16))
        off = pl.multiple_of((s * S1_PAGES + r) * PAGE, PAGE)
        keys_ref[0, :, pl.ds(off, PAGE)] = sc

    @pl.when(s == pl.num_programs(1) - 1)
    def _():
        sc = _idx_page_scores(iq, iww, iknew_ref[0])
        t = lax.broadcasted_iota(I32, sc.shape, 0)
        jn = lax.broadcasted_iota(I32, sc.shape, 1)
        sc = jnp.where((jn <= t) & (jn < n_new), sc, NEG)
        keys_ref[0, :, pl.ds(past, PAGE)] = sc
        nk = keys_ref.shape[2]
        lane = lax.broadcasted_iota(I32, (SUBLANES, LANES), 1)

        def count_keys(pred):
            acc = [jnp.zeros((SUBLANES, LANES), F32) for _ in range(4)]
            for j in range(nk // LANES):
                kk = keys_ref[0, :, j * LANES:(j + 1) * LANES]
                acc[j % 4] = acc[j % 4] + jnp.where(pred(kk, j * LANES + lane), 1.0, 0.0)
            tot = (acc[0] + acc[1]) + (acc[2] + acc[3])
            return jnp.broadcast_to(jnp.sum(tot, axis=1, keepdims=True), (SUBLANES, LANES))

        def count_ge(cand):
            return count_keys(lambda kk, kidx: kk >= cand)

        k = float(topk)
        thr, cnt = _kth_largest(count_ge, (SUBLANES, LANES), k, float(nk))
        thr_ref[0, :, :LANES] = thr
        tied = (cnt > k) & (thr > NEG)
        thr_ref[0, :, LANES:] = jnp.full((SUBLANES, LANES), float(nk), F32)

        @pl.when(jnp.max(jnp.where(tied, 1.0, 0.0)) > 0.0)
        def _():
            need = k - count_keys(lambda kk, kidx: kk > thr)
            nbits = (nk - 1).bit_length()

            def bit_body(bi, x):
                cand = x | jnp.left_shift(jnp.int32(1), nbits - 1 - bi)
                below = count_keys(lambda kk, kidx: (kk == thr) & (kidx < cand))
                return jnp.where(below < need, cand, x)

            x = lax.fori_loop(0, nbits, bit_body, jnp.zeros((SUBLANES, LANES), I32))
            thr_ref[0, :, LANES:] = jnp.where(tied, x, nk).astype(F32)


def _sample_index(page_table, cache_ik, layer, iq, iww, iknew, *, n_new, topk):
    nb, n_pages = page_table.shape
    steps = n_pages // S1_PAGES
    nk = (n_pages + 1) * PAGE
    page_spec = lambda r: pl.BlockSpec(
        (None, None, IDX_DIM, PAGE), lambda b, s, pt: (layer, pt[b, s * S1_PAGES + r], 0, 0))
    per_b = lambda b, s, pt: (b, 0, 0)
    return pl.pallas_call(
        functools.partial(_sample_index_kernel, past=n_pages * PAGE, n_new=n_new, topk=topk),
        grid_spec=pltpu.PrefetchScalarGridSpec(
            num_scalar_prefetch=1,
            grid=(nb, steps),
            in_specs=[page_spec(r) for r in range(S1_PAGES)] + [
                pl.BlockSpec((1, IDX_HEADS * SUBLANES, IDX_DIM), per_b),
                pl.BlockSpec((1, IDX_HEADS * SUBLANES, LANES), per_b),
                pl.BlockSpec((1, IDX_DIM, PAGE), per_b),
            ],
            out_specs=[
                pl.BlockSpec((1, SUBLANES, nk), per_b),
                pl.BlockSpec((1, SUBLANES, 2 * LANES), per_b),
            ],
        ),
        out_shape=[jax.ShapeDtypeStruct((nb, SUBLANES, nk), F32),
                   jax.ShapeDtypeStruct((nb, SUBLANES, 2 * LANES), F32)],
        compiler_params=pltpu.CompilerParams(
            dimension_semantics=("arbitrary", "arbitrary"), vmem_limit_bytes=VMEM_LIMIT),
        name="sample_index",
    )(page_table, *([cache_ik] * S1_PAGES), iq, iww, iknew)


def _sample_attend_kernel(pt_ref, *refs, n_new):
    kpages = refs[:S2_PAGES]
    vpages = refs[S2_PAGES:2 * S2_PAGES]
    (qbd_ref, keys_ref, keysnew_ref, thr_ref, knew_ref, vnew_ref, o_ref,
     m_sc, l_sc, acc_sc) = refs[2 * S2_PAGES:]
    s = pl.program_id(1)
    scale = HEAD_DIM ** -0.5
    qbd = qbd_ref[0]
    thr = thr_ref[0, :, :LANES]
    jt = thr_ref[0, :, LANES:].astype(I32)
    lane = lax.broadcasted_iota(I32, (SUBLANES, LANES), 1)

    def selected(kk, kidx):
        return (kk > thr) | ((kk == thr) & (kidx <= jt))

    @pl.when(s == 0)
    def _():
        m_sc[...] = jnp.full(m_sc.shape, NEG, F32)
        l_sc[...] = jnp.zeros(l_sc.shape, F32)
        acc_sc[...] = jnp.zeros(acc_sc.shape, F32)

    def page_update(kp, vp, bias):
        bias = jnp.concatenate([bias] * (N_HEADS), axis=0)
        sc = lax.dot_general(qbd, kp, _NT, preferred_element_type=F32) * scale + bias
        m_old = m_sc[...]
        m_new = jnp.maximum(m_old, jnp.max(sc, axis=1, keepdims=True))
        alpha = jnp.exp(m_old - m_new)
        p = jnp.exp(sc - m_new)
        l_sc[...] = alpha * l_sc[...] + jnp.sum(p, axis=1, keepdims=True)
        acc_sc[...] = alpha * acc_sc[...] + jnp.dot(p.astype(BF16), vp, preferred_element_type=F32)
        m_sc[...] = m_new

    def page_rows(ref):
        return jnp.concatenate(
            [ref[pl.ds(n, PAGE, stride=N_KV), :] for n in range(N_KV)], axis=1).astype(BF16)

    bias = jnp.concatenate(
        [jnp.where(selected(keys_ref[0, :, r * PAGE:(r + 1) * PAGE], (s * S2_PAGES + r) * PAGE + lane), 0.0, NEG)
         for r in range(S2_PAGES)], axis=1)
    page_update(jnp.concatenate([page_rows(kpages[r]) for r in range(S2_PAGES)], axis=0),
                jnp.concatenate([page_rows(vpages[r]) for r in range(S2_PAGES)], axis=0), bias)

    @pl.when(s == pl.num_programs(1) - 1)
    def _():
        kk = keysnew_ref[0]
        t = lax.broadcasted_iota(I32, kk.shape, 0)
        jn = lax.broadcasted_iota(I32, kk.shape, 1)
        past = pl.num_programs(1) * S2_PAGES * PAGE
        page_update(knew_ref[0], vnew_ref[0],
                    jnp.where(selected(kk, past + jn) & (jn <= t) & (jn < n_new), 0.0, NEG))
        o_ref[0] = acc_sc[...] / l_sc[...]


def _sample_attend(page_table, cache_k, cache_v, layer, qbd, keys, thr, knew, vnew, *, n_new):
    nb, n_pages = page_table.shape
    steps = n_pages // S2_PAGES
    page_spec = lambda r: pl.BlockSpec(
        (None, None, PAGE * N_KV, HEAD_DIM), lambda b, s, pt: (layer, pt[b, s * S2_PAGES + r], 0, 0))
    per_b = lambda b, s, pt: (b, 0, 0)
    rows = N_HEADS * SUBLANES
    return pl.pallas_call(
        functools.partial(_sample_attend_kernel, n_new=n_new),
        grid_spec=pltpu.PrefetchScalarGridSpec(
            num_scalar_prefetch=1,
            grid=(nb, steps),
            in_specs=[page_spec(r) for r in range(S2_PAGES)] * 2 + [
                pl.BlockSpec((1, rows, KV_W), per_b),
                pl.BlockSpec((1, SUBLANES, S2_PAGES * PAGE), lambda b, s, pt: (b, 0, s)),
                pl.BlockSpec((1, SUBLANES, PAGE), lambda b, s, pt: (b, 0, n_pages)),
                pl.BlockSpec((1, SUBLANES, 2 * LANES), per_b),
                pl.BlockSpec((1, PAGE, KV_W), per_b),
                pl.BlockSpec((1, PAGE, KV_W), per_b),
            ],
            out_specs=pl.BlockSpec((1, rows, KV_W), per_b),
            scratch_shapes=[
                pltpu.VMEM((rows, 1), F32),
                pltpu.VMEM((rows, 1), F32),
                pltpu.VMEM((rows, KV_W), F32),
            ],
        ),
        out_shape=jax.ShapeDtypeStruct((nb, rows, KV_W), F32),
        compiler_params=pltpu.CompilerParams(
            dimension_semantics=("arbitrary", "arbitrary"), vmem_limit_bytes=VMEM_LIMIT),
        name="sample_attend",
    )(page_table, *([cache_k] * S2_PAGES), *([cache_v] * S2_PAGES), qbd, keys, keys, thr, knew, vnew)


def _matmul_res_kernel(h_ref, x_ref, w_ref, o_ref):
    o_ref[...] = h_ref[...] + jnp.dot(x_ref[...], w_ref[...], preferred_element_type=F32)


def _matmul_res(h, x, w, layer, *, tn):
    m, n = h.shape
    kdim = x.shape[1]
    return pl.pallas_call(
        _matmul_res_kernel,
        grid=(n // tn,),
        in_specs=[pl.BlockSpec((m, tn), lambda j: (0, j)),
                  pl.BlockSpec((m, kdim), lambda j: (0, 0)),
                  pl.BlockSpec((None, kdim, tn), lambda j: (layer, 0, j))],
        out_specs=pl.BlockSpec((m, tn), lambda j: (0, j)),
        out_shape=jax.ShapeDtypeStruct((m, n), F32),
        compiler_params=pltpu.CompilerParams(dimension_semantics=("arbitrary",)),
        name="matmul_residual",
    )(h, x, w)


def _conv_rows(ext_sc, u, init_ref, carry_sc, c, first, cw_ref, cb_ref, tail_ref, *,
               width, shift, d0, tail_end):
    tm = u.shape[0]
    ext_sc[d0:d0 + tm, :] = u
    ext_sc[0:d0, :] = jnp.where(first, init_ref[0], carry_sc[c])

    y = cb_ref[...] + cw_ref[width - 1:width, :] * u
    for j in range(1, width):
        y = y + cw_ref[width - 1 - j:width - j, :] * ext_sc[d0 - j * shift:d0 - j * shift + tm, :]
    carry_sc[c] = ext_sc[tm:tm + d0, :]
    tail_ref[0] = ext_sc[tail_end:tail_end + d0, :]
    return y


def _ffn_kernel(h_ref, g_ref, wu_ref, wg_ref, cw_ref, cb_ref, wd_ref, st_ref, out_ref, tail_ref,
                xn_sc, acc_sc, ext_sc, carry_sc, *, shift, d0, tiles_per_seq, tail_end):
    i = pl.program_id(0)
    c = pl.program_id(1)

    @pl.when(c == 0)
    def _():
        xn_sc[...] = _rms_rows(h_ref[...], g_ref[...]).astype(BF16)
        acc_sc[...] = jnp.zeros(acc_sc.shape, F32)

    xn = xn_sc[...]
    u = jnp.dot(xn, wu_ref[...], preferred_element_type=F32)
    conv = _conv_rows(ext_sc, u, st_ref, carry_sc, c, (i % tiles_per_seq) == 0, cw_ref, cb_ref,
                      tail_ref, width=FFN_CONV, shift=shift, d0=d0, tail_end=tail_end)
    mid = _gelu(conv) * jnp.dot(xn, wg_ref[...], preferred_element_type=F32)
    acc_sc[...] += jnp.dot(mid.astype(BF16), wd_ref[...], preferred_element_type=F32)

    @pl.when(c == pl.num_programs(1) - 1)
    def _():
        out_ref[...] = h_ref[...] + acc_sc[...]


def _conv_ffn(h, g, wu, wg, cw, cb, wd, state, *, layer, tm, tf, shift, tiles_per_seq, tail_end):
    m = h.shape[0]
    d0 = state.shape[1]
    n_tiles, n_chunks = m // tm, D_FF // tf
    row = lambda i, c: (i, 0)
    return pl.pallas_call(
        functools.partial(_ffn_kernel, shift=shift, d0=d0, tiles_per_seq=tiles_per_seq, tail_end=tail_end),
        grid=(n_tiles, n_chunks),
        in_specs=[
            pl.BlockSpec((tm, D_MODEL), row),
            pl.BlockSpec((1, D_MODEL), lambda i, c: (0, 0)),
            pl.BlockSpec((None, D_MODEL, tf), lambda i, c: (layer, 0, c)),
            pl.BlockSpec((None, D_MODEL, tf), lambda i, c: (layer, 0, c)),
            pl.BlockSpec((FFN_CONV, tf), lambda i, c: (0, c)),
            pl.BlockSpec((1, tf), lambda i, c: (0, c)),
            pl.BlockSpec((None, tf, D_MODEL), lambda i, c: (layer, c, 0)),
            pl.BlockSpec((1, d0, tf), lambda i, c: (i // tiles_per_seq, 0, c)),
        ],
        out_specs=[
            pl.BlockSpec((tm, D_MODEL), row),
            pl.BlockSpec((1, d0, tf), lambda i, c: (i, 0, c)),
        ],
        out_shape=[jax.ShapeDtypeStruct((m, D_MODEL), F32),
                   jax.ShapeDtypeStruct((n_tiles, d0, D_FF), F32)],
        scratch_shapes=[
            pltpu.VMEM((tm, D_MODEL), BF16),
            pltpu.VMEM((tm, D_MODEL), F32),
            pltpu.VMEM((d0 + tm, tf), F32),
            pltpu.VMEM((n_chunks, d0, tf), F32),
        ],
        compiler_params=pltpu.CompilerParams(
            dimension_semantics=("arbitrary", "arbitrary"), vmem_limit_bytes=VMEM_LIMIT),
        name="conv_ffn",
    )(h, g, wu, wg, cw, cb, wd, state)


LRU_BLOCKS_PER_STEP = 2

def _scan_rows(a_sc, b_sc, hc, *, shift):
    tm = a_sc.shape[0]
    ngroups = tm // SUBLANES
    row = lax.broadcasted_iota(I32, (SUBLANES, a_sc.shape[1]), 0)

    def body(gi, hc):
        r0 = gi * SUBLANES
        a = a_sc[pl.ds(r0, SUBLANES), :]
        b = b_sc[pl.ds(r0, SUBLANES), :]
        if shift == 1:
            for s in (1, 2, 4):
                keep = row >= s
                a_sh = jnp.where(keep, pltpu.roll(a, s, 0), 1.0)
                b_sh = jnp.where(keep, pltpu.roll(b, s, 0), 0.0)
                b = a * b_sh + b
                a = a * a_sh
        hrows = a * hc + b
        b_sc[pl.ds(r0, SUBLANES), :] = hrows
        if shift == 1:
            return jnp.broadcast_to(hrows[SUBLANES - 1:SUBLANES, :], hrows.shape)
        return hrows

    for gi in range(ngroups):
        hc = body(gi, hc)
    return hc


def _lru_kernel(h_ref, g_ref, wx_ref, wy_ref, cw_ref, cb_ref, wa_ref, ba_ref, wi_ref, bi_ref,
                lam_ref, wo_ref, h0_ref, st_ref, out_ref, htail_ref, utail_ref,
                xn_sc, acc_sc, ext_sc, a_sc, b_sc, gate_sc, carry_u, carry_h, *,
                shift, d0, tiles_per_seq, tail_end):
    i = pl.program_id(0)
    c = pl.program_id(1)
    first = (i % tiles_per_seq) == 0

    @pl.when(c == 0)
    def _():
        xn_sc[...] = _rms_rows(h_ref[...], g_ref[...]).astype(BF16)
        acc_sc[...] = jnp.zeros(acc_sc.shape, F32)

    xn = xn_sc[...]
    u = jnp.dot(xn, wx_ref[...], preferred_element_type=F32)
    gate_sc[...] = _gelu(jnp.dot(xn, wy_ref[...], preferred_element_type=F32))
    conv = _conv_rows(ext_sc, u, st_ref, carry_u, c, first, cw_ref, cb_ref, utail_ref,
                      width=LRU_CONV, shift=shift, d0=d0, tail_end=tail_end)
    lam = lam_ref[...]
    log_sig = jnp.minimum(lam, 0.0) - jnp.log(1.0 + jnp.exp(-jnp.abs(lam)))
    for s in range(wa_ref.shape[0]):
        sl = slice(s * RNN_BD, (s + 1) * RNN_BD)
        cs = conv[:, sl]
        cbf = cs.astype(BF16)
        r = _sigmoid(jnp.dot(cbf, wa_ref[s], preferred_element_type=F32) + ba_ref[:, sl])
        ig = _sigmoid(jnp.dot(cbf, wi_ref[s], preferred_element_type=F32) + bi_ref[:, sl])
        log_a = LRU_C * r * log_sig[:, sl]
        a = jnp.exp(log_a)
        a_sc[:, sl] = a
        b_sc[:, sl] = jnp.sqrt(-jnp.tanh(log_a) * (a * a + 1.0)) * (ig * cs)

    carry_h[c] = _scan_rows(a_sc, b_sc, jnp.where(first, h0_ref[0], carry_h[c]), shift=shift)
    htail_ref[0] = b_sc[tail_end - SUBLANES:tail_end, :]
    tm = a_sc.shape[0]
    nsplit = 4 if tm % (4 * 2 * SUBLANES) == 0 else 1
    for rc in range(nsplit):
        rows = slice(rc * (tm // nsplit), (rc + 1) * (tm // nsplit))
        acc_sc[rows, :] += jnp.dot((b_sc[rows, :] * gate_sc[rows, :]).astype(BF16), wo_ref[...],
                                   preferred_element_type=F32)

    @pl.when(c == pl.num_programs(1) - 1)
    def _():
        out_ref[...] = h_ref[...] + acc_sc[...]


def _lru_block(h, g, wx, wy, cw, cb, wa, ba, wi, bi, lam, wo, h0, state, *,
               layer, tm, shift, tiles_per_seq, tail_end):
    m = h.shape[0]
    d0 = state.shape[1]
    n_tiles = m // tm
    nsub = LRU_BLOCKS_PER_STEP
    bd = nsub * RNN_BD
    nsteps = RNN_BLOCKS // nsub
    row = lambda i, c: (i, 0)
    col = lambda i, c: (0, c)
    return pl.pallas_call(
        functools.partial(_lru_kernel, shift=shift, d0=d0, tiles_per_seq=tiles_per_seq, tail_end=tail_end),
        grid=(n_tiles, nsteps),
        in_specs=[
            pl.BlockSpec((tm, D_MODEL), row),
            pl.BlockSpec((1, D_MODEL), lambda i, c: (0, 0)),
            pl.BlockSpec((None, D_MODEL, bd), lambda i, c: (layer, 0, c)),
            pl.BlockSpec((None, D_MODEL, bd), lambda i, c: (layer, 0, c)),
            pl.BlockSpec((LRU_CONV, bd), col),
            pl.BlockSpec((1, bd), col),
            pl.BlockSpec((None, nsub, RNN_BD, RNN_BD), lambda i, c: (layer, c, 0, 0)),
            pl.BlockSpec((1, bd), col),
            pl.BlockSpec((None, nsub, RNN_BD, RNN_BD), lambda i, c: (layer, c, 0, 0)),
            pl.BlockSpec((1, bd), col),
            pl.BlockSpec((1, bd), col),
            pl.BlockSpec((None, bd, D_MODEL), lambda i, c: (layer, c, 0)),
            pl.BlockSpec((1, SUBLANES, bd), lambda i, c: (i // tiles_per_seq, 0, c)),
            pl.BlockSpec((1, d0, bd), lambda i, c: (i // tiles_per_seq, 0, c)),
        ],
        out_specs=[
            pl.BlockSpec((tm, D_MODEL), row),
            pl.BlockSpec((1, SUBLANES, bd), lambda i, c: (i, 0, c)),
            pl.BlockSpec((1, d0, bd), lambda i, c: (i, 0, c)),
        ],
        out_shape=[jax.ShapeDtypeStruct((m, D_MODEL), F32),
                   jax.ShapeDtypeStruct((n_tiles, SUBLANES, D_MODEL), F32),
                   jax.ShapeDtypeStruct((n_tiles, d0, D_MODEL), F32)],
        scratch_shapes=[
            pltpu.VMEM((tm, D_MODEL), BF16),
            pltpu.VMEM((tm, D_MODEL), F32),
            pltpu.VMEM((d0 + tm, bd), F32),
            pltpu.VMEM((tm, bd), F32),
            pltpu.VMEM((tm, bd), F32),
            pltpu.VMEM((tm, bd), F32),
            pltpu.VMEM((nsteps, d0, bd), F32),
            pltpu.VMEM((nsteps, SUBLANES, bd), F32),
        ],
        compiler_params=pltpu.CompilerParams(
            dimension_semantics=("arbitrary", "arbitrary"), vmem_limit_bytes=VMEM_LIMIT),
        name="rg_lru",
    )(h, g, wx, wy, cw, cb, wa, ba, wi, bi, lam, wo, h0, state)


def _rope_tables(pos, rot, period):
    half = rot // 2
    inv = 1.0 / (ROPE_THETA ** (jnp.arange(half, dtype=F32) * (2.0 / rot)))
    ang = pos.astype(F32)[:, None] * inv[None, :]
    lane = jnp.arange(LANES) % period
    cos = jnp.cos(ang)[:, lane % half]
    sin = jnp.sin(ang)[:, lane % half]
    return jnp.stack([
        jnp.where(lane < rot, cos, 1.0),
        jnp.where(lane < half, -sin, 0.0),
        jnp.where((lane >= half) & (lane < rot), sin, 0.0),
    ])


def _time_major(x):
    return jnp.swapaxes(x, 0, 1)


def kernel(x_prompt, x_sample, cache_k, cache_v, cache_idx_k, state_lru_h, state_lru_conv, state_ffn_conv, page_table, meta_tokens, norm1_g, norm2_g, attn_w_in, attn_q_norm_g, attn_k_norm_g, attn_idx_k_norm_g, attn_w_out, lru_w_x, lru_w_y, lru_conv_w, lru_conv_b, lru_wa, lru_ba, lru_wi, lru_bi, lru_lambda, lru_w_out, ffn_w_up, ffn_w_gate, ffn_dw_w, ffn_dw_b, ffn_w_down):
    batch, seq, _ = x_prompt.shape
    nb, ns, _ = x_sample.shape
    depth = norm1_g.shape[0]
    t_real = seq + N_META
    tp = _round_up(t_real, QB)
    n_pages = page_table.shape[1]
    past = n_pages * PAGE
    topk_p = min(TOPK_MAX, t_real // 4)
    topk_s = min(TOPK_MAX, (past + ns) // 4)
    assert nb == SUBLANES and ns <= SUBLANES and n_pages % S1_PAGES == 0 and n_pages % S2_PAGES == 0

    tiles_per_seq = 6
    tm = tp // tiles_per_seq
    assert tm * tiles_per_seq == tp and tm % (2 * SUBLANES) == 0
    tail_end = t_real - (tiles_per_seq - 1) * tm
    assert tail_end % SUBLANES == 0 and tail_end >= SUBLANES
    tm_proj = 3 * LANES
    assert tp % tm_proj == 0
    ms = nb * ns

    hp = jnp.concatenate([
        jnp.broadcast_to(meta_tokens[None], (batch, N_META, D_MODEL)), x_prompt,
        jnp.zeros((batch, tp - t_real, D_MODEL), F32)], axis=1).reshape(batch * tp, D_MODEL)
    hs = _time_major(x_sample).reshape(ms, D_MODEL)

    rq_p = _rope_tables(jnp.arange(tp), ROT, HEAD_DIM)
    ri_p = _rope_tables(jnp.arange(tp), IDX_ROT, IDX_DIM)
    pos_s = jnp.repeat(past + jnp.arange(ns), nb)
    rq_s = _rope_tables(pos_s, ROT, HEAD_DIM)
    ri_s = _rope_tables(pos_s, IDX_ROT, IDX_DIM)

    cache_k4 = cache_k.reshape(cache_k.shape[:2] + (PAGE * N_KV, HEAD_DIM))
    cache_v4 = cache_v.reshape(cache_v.shape[:2] + (PAGE * N_KV, HEAD_DIM))
    cache_ik_t = jnp.swapaxes(cache_idx_k, 2, 3)

    w_in_t = jnp.pad(jnp.swapaxes(attn_w_in, 1, 2).astype(BF16), ((0, 0), (0, PROJ_WP - PROJ_W), (0, 0)))
    w_out_all = attn_w_out.astype(BF16)
    lru_mats = dict(wx=lru_w_x.astype(BF16), wy=lru_w_y.astype(BF16), wa=lru_wa.astype(BF16),
                    wi=lru_wi.astype(BF16), wo=lru_w_out.astype(BF16))
    ffn_mats = dict(wu=ffn_w_up.astype(BF16), wg=ffn_w_gate.astype(BF16), wd=ffn_w_down.astype(BF16))

    kp, vp, ikp, ksl, vsl, iksl = [], [], [], [], [], []
    hpl, cpl, hsl, csl, fpl, fsl = [], [], [], [], [], []
    last_tiles = jnp.arange(batch) * tiles_per_seq + tiles_per_seq - 1

    for layer in range(depth):
        mi = layer // 2
        g1 = norm1_g[layer][None]
        if layer % 2 == 0:
            qg = attn_q_norm_g[mi][None]
            kg = attn_k_norm_g[mi][None]
            ikg = jnp.pad(attn_idx_k_norm_g[mi], (0, LANES - IDX_DIM))[None]

            q, kf, kb, vf, vb, iq, ikf, ika, ikb, iw = _attn_project(
                hp, g1, w_in_t, mi, qg, kg, ikg, rq_p, ri_p, tm=tm_proj, nseq=batch, rows_real=t_real)
            logit_bound = (1.02 * HEAD_DIM ** 0.5) * jnp.max(jnp.abs(qg)) * jnp.max(jnp.abs(kg))
            hp = _attn_prompt(hp, q, iq, iw, kb, vb, ika, ikb, w_out_all, mi, logit_bound,
                              batch=batch, tp=tp, topk=topk_p)
            kp.append(kf.reshape(batch, t_real, N_KV, HEAD_DIM))
            vp.append(vf.reshape(batch, t_real, N_KV, HEAD_DIM))
            ikp.append(ikf)

            q, kf, kb, vf, vb, iq, ikf, ika, ikb, iw = _attn_project(
                hs, g1, w_in_t, mi, qg, kg, ikg, rq_s, ri_s, tm=ms, nseq=1, rows_real=ms)
            ksl.append(_time_major(kf.reshape(ns, nb, N_KV, HEAD_DIM)))
            vsl.append(_time_major(vf.reshape(ns, nb, N_KV, HEAD_DIM)))
            iksl.append(_time_major(ikf.reshape(ns, nb, IDX_DIM)))

            pad_q = lambda x: jnp.pad(x, [(0, 0)] * (x.ndim - 2) + [(0, SUBLANES - ns), (0, 0)])
            iq_b = pad_q(iq.reshape(ns, nb, IDX_HEADS, IDX_DIM).transpose(1, 2, 0, 3))
            iq_b = iq_b.reshape(nb, IDX_HEADS * SUBLANES, IDX_DIM)
            iw_b = jnp.pad(iw[:, :IDX_HEADS].reshape(ns, nb, IDX_HEADS).transpose(1, 2, 0), ((0, 0), (0, 0), (0, SUBLANES - ns)))
            iww = jnp.broadcast_to(iw_b.reshape(nb, IDX_HEADS * SUBLANES, 1), (nb, IDX_HEADS * SUBLANES, LANES))
            new_page = lambda x: jnp.pad(_time_major(x.reshape(ns, nb, -1)), ((0, 0), (0, PAGE - ns), (0, 0)))
            iknew = jnp.swapaxes(new_page(ika[:, :IDX_DIM]), 1, 2)
            keys, thr = _sample_index(page_table, cache_ik_t, mi, iq_b, iww, iknew, n_new=ns, topk=topk_s)

            q_b = pad_q(q.reshape(ns, nb, N_KV, GROUP, HEAD_DIM).transpose(1, 2, 3, 0, 4))
            qbd = q_b[:, :, :, :, None, :] * jnp.eye(N_KV, dtype=BF16)[None, :, None, None, :, None]
            qbd = qbd.reshape(nb, N_HEADS * SUBLANES, KV_W)
            o = _sample_attend(page_table, cache_k4, cache_v4, mi, qbd, keys, thr,
                               new_page(kb), new_page(vb), n_new=ns)
            o = o.reshape(nb, N_KV, GROUP, SUBLANES, N_KV, HEAD_DIM)
            o = jnp.stack([o[:, n, :, :ns, n, :] for n in range(N_KV)], axis=1)
            o = o.transpose(3, 0, 1, 2, 4).reshape(ms, Q_W).astype(BF16)
            hs = _matmul_res(hs, o, w_out_all, mi, tn=512)
        else:
            lw = dict(
                g=g1, cw=lru_conv_w[mi], cb=lru_conv_b[mi][None], ba=lru_ba[mi][None], bi=lru_bi[mi][None],
                lam=lru_lambda[mi][None], layer=mi, **lru_mats)
            hp, htail, utail = _lru_block(
                hp, h0=jnp.zeros((batch, SUBLANES, D_MODEL), F32),
                state=jnp.zeros((batch, SUBLANES, D_MODEL), F32),
                tm=tm, shift=1, tiles_per_seq=tiles_per_seq, tail_end=tail_end, **lw)
            hpl.append(htail[last_tiles, SUBLANES - 1])
            cpl.append(utail[last_tiles, SUBLANES - (LRU_CONV - 1):])
            d0s = (LRU_CONV - 1) * nb
            hs, htail, utail = _lru_block(
                hs, h0=state_lru_h[mi][None],
                state=_time_major(state_lru_conv[mi]).reshape(1, d0s, D_MODEL),
                tm=ms, shift=nb, tiles_per_seq=1, tail_end=ms, **lw)
            hsl.append(htail[0])
            csl.append(_time_major(utail.reshape(LRU_CONV - 1, nb, D_MODEL)))

        fw = dict(g=norm2_g[layer][None], cw=ffn_dw_w[layer], cb=ffn_dw_b[layer][None], layer=layer, **ffn_mats)
        hp, tail = _conv_ffn(hp, state=jnp.zeros((batch, SUBLANES, D_FF), F32),
                             tm=tm, tf=512, shift=1, tiles_per_seq=tiles_per_seq, tail_end=tail_end, **fw)
        fpl.append(tail[last_tiles, SUBLANES - (FFN_CONV - 1):])
        d0s = (FFN_CONV - 1) * nb
        hs, tail = _conv_ffn(hs, state=_time_major(state_ffn_conv[layer]).reshape(1, d0s, D_FF),
                             tm=ms, tf=512, shift=nb, tiles_per_seq=1, tail_end=ms, **fw)
        fsl.append(_time_major(tail.reshape(FFN_CONV - 1, nb, D_FF)))

    y_prompt = hp.reshape(batch, tp, D_MODEL)[:, N_META:t_real]
    y_sample = _time_major(hs.reshape(ns, nb, D_MODEL))
    return (y_prompt, y_sample, jnp.stack(kp), jnp.stack(vp), jnp.stack(ikp),
            jnp.stack(ksl), jnp.stack(vsl), jnp.stack(iksl),
            jnp.stack(hpl), jnp.stack(cpl), jnp.stack(hsl), jnp.stack(csl),
            jnp.stack(fpl), jnp.stack(fsl))
```

```python
import functools

import jax
import jax.numpy as jnp
from jax import lax
from jax.experimental import pallas as pl
from jax.experimental.pallas import tpu as pltpu

F32 = jnp.float32
BF16 = jnp.bfloat16
I32 = jnp.int32

D_MODEL = 2048
N_META = 16
N_HEADS = 16
HEAD_DIM = 128
N_KV = 4
GROUP = N_HEADS // N_KV
ROT = HEAD_DIM // 4
IDX_HEADS = 16
IDX_DIM = 64
IDX_ROT = IDX_DIM // 4
ROPE_THETA = 500000.0
TOPK_MAX = 256
PAGE = 128
Q_W = N_HEADS * HEAD_DIM
KV_W = N_KV * HEAD_DIM
IQ_W = IDX_HEADS * IDX_DIM
PROJ_W = Q_W + 2 * KV_W + IQ_W + IDX_DIM + IDX_HEADS
D_FF = 3 * D_MODEL
FFN_CONV = 3
LRU_CONV = 4
LRU_C = 8.0
RNN_BLOCKS = 8
RNN_BD = D_MODEL // RNN_BLOCKS
EPS = 1e-6
NEG = -1e30

LANES = 128
SUBLANES = 8
QB = 128
MXU_COLS = 256
PROJ_WP = -(-PROJ_W // LANES) * LANES
VMEM_LIMIT = 56 * 1024 * 1024

_NT = (((1,), (1,)), ((), ()))
_INT_MIN = -2 ** 31


def _round_up(x, m):
    return -(-x // m) * m


def _rms_rows(x, g):
    ms = jnp.mean(x * x, axis=-1, keepdims=True)
    return x * lax.rsqrt(ms + EPS) * g


def _gelu(x):
    return 0.5 * x * (1.0 + jnp.tanh(0.7978845608028654 * (x + 0.044715 * (x * x * x))))


def _sigmoid(x):
    return 0.5 * jnp.tanh(0.5 * x) + 0.5


def _ordered_bits_to_float(u):
    key = u ^ _INT_MIN
    b = key ^ ((key >> 31) & 0x7FFFFFFF)
    return lax.bitcast_convert_type(b, F32)


def _kth_largest(count_ge, shape, k, total):
    def bit_body(bi, carry):
        u, cnt_u = carry
        cand = u | jnp.left_shift(jnp.int32(1), 31 - bi)
        cnt = count_ge(_ordered_bits_to_float(cand))
        keep = cnt >= k
        return jnp.where(keep, cand, u), jnp.where(keep, cnt, cnt_u)
    u, cnt_u = lax.fori_loop(0, 32, bit_body, (jnp.zeros(shape, I32), jnp.full(shape, total, F32)))
    return _ordered_bits_to_float(u), cnt_u


def _rope(x, tab_ref, sh):
    return (x * tab_ref[0] + pltpu.roll(x, LANES - sh, 1) * tab_ref[1]
            + pltpu.roll(x, sh, 1) * tab_ref[2])


def _proj_kernel(h_ref, g_ref, w_ref, qg_ref, kg_ref, ikg_ref, rq_ref, ri_ref,
                 q_ref, kf_ref, kb_ref, vf_ref, vb_ref, iq_ref, ikf_ref, ika_ref, ikb_ref, iw_ref):
    xn = _rms_rows(h_ref[...], g_ref[...]).astype(BF16)

    def zcols(lo, width=MXU_COLS):
        return lax.dot_general(xn, w_ref[lo:lo + width, :], _NT, preferred_element_type=F32)

    tm = h_ref.shape[0]
    halves = [slice(s * LANES, (s + 1) * LANES) for s in range(MXU_COLS // LANES)]
    for j in range(Q_W // MXU_COLS):
        z = zcols(j * MXU_COLS)
        for sl in halves:
            x = _rope(_rms_rows(z[:, sl], qg_ref[...]), rq_ref, ROT // 2)
            q_ref[:, j * MXU_COLS + sl.start:j * MXU_COLS + sl.stop] = x.astype(BF16)
    for j in range(KV_W // MXU_COLS):
        z = zcols(Q_W + j * MXU_COLS)
        for s, sl in enumerate(halves):
            n = j * len(halves) + s
            x = _rope(_rms_rows(z[:, sl], kg_ref[...]), rq_ref, ROT // 2)
            kf_ref[0, pl.ds(n, tm, stride=N_KV), :] = x
            kb_ref[:, n * HEAD_DIM:(n + 1) * HEAD_DIM] = x.astype(BF16)
        z = zcols(Q_W + KV_W + j * MXU_COLS)
        for s, sl in enumerate(halves):
            n = j * len(halves) + s
            vf_ref[0, pl.ds(n, tm, stride=N_KV), :] = z[:, sl]
        vb_ref[:, j * MXU_COLS:(j + 1) * MXU_COLS] = z.astype(BF16)
    for j in range(IQ_W // MXU_COLS):
        z = zcols(Q_W + 2 * KV_W + j * MXU_COLS)
        for sl in halves:
            x = _rope(z[:, sl], ri_ref, IDX_ROT // 2)
            iq_ref[:, j * MXU_COLS + sl.start:j * MXU_COLS + sl.stop] = x.astype(BF16)

    x = zcols(Q_W + 2 * KV_W + IQ_W, LANES)
    lane = lax.broadcasted_iota(I32, x.shape, 1)
    ms = jnp.sum(jnp.where(lane < IDX_DIM, x * x, 0.0), axis=-1, keepdims=True) * (1.0 / IDX_DIM)
    y = _rope(x * lax.rsqrt(ms + EPS) * ikg_ref[...], ri_ref, IDX_ROT // 2)
    ikf_ref[0] = y[:, :IDX_DIM]
    ika_ref[...] = y.astype(BF16)
    ikb_ref[...] = pltpu.roll(y, IDX_DIM, 1).astype(BF16)
    iw_ref[...] = jnp.where(lane < IDX_HEADS, pltpu.roll(x, LANES - IDX_DIM, 1) * (IDX_HEADS ** -0.5), 0.0)


def _attn_project(h, g, w_t, layer, qg, kg, ikg, rq, ri, *, tm, nseq, rows_real):
    m = h.shape[0]
    tab_tiles = rq.shape[1] // tm
    assert m == nseq * tab_tiles * tm
    row = lambda i: (i, 0)
    const = lambda i: (0, 0)
    tab = lambda i: (0, i % tab_tiles, 0)
    seq_row = lambda i: (i // tab_tiles, i % tab_tiles, 0)
    sds = jax.ShapeDtypeStruct
    return pl.pallas_call(
        _proj_kernel,
        grid=(m // tm,),
        in_specs=[
            pl.BlockSpec((tm, D_MODEL), row),
            pl.BlockSpec((1, D_MODEL), const),
            pl.BlockSpec((None,) + w_t.shape[1:], lambda i: (layer, 0, 0), pipeline_mode=pl.Buffered(1)),
            pl.BlockSpec((1, LANES), const),
            pl.BlockSpec((1, LANES), const),
            pl.BlockSpec((1, LANES), const),
            pl.BlockSpec((3, tm, LANES), tab),
            pl.BlockSpec((3, tm, LANES), tab),
        ],
        out_specs=[
            pl.BlockSpec((tm, Q_W), row),
            pl.BlockSpec((1, tm * N_KV, HEAD_DIM), seq_row),
            pl.BlockSpec((tm, KV_W), row),
            pl.BlockSpec((1, tm * N_KV, HEAD_DIM), seq_row),
            pl.BlockSpec((tm, KV_W), row),
            pl.BlockSpec((tm, IQ_W), row),
            pl.BlockSpec((1, tm, IDX_DIM), seq_row),
            pl.BlockSpec((tm, LANES), row),
            pl.BlockSpec((tm, LANES), row),
            pl.BlockSpec((tm, LANES), row),
        ],
        out_shape=[
            sds((m, Q_W), BF16), sds((nseq, rows_real * N_KV, HEAD_DIM), F32), sds((m, KV_W), BF16),
            sds((nseq, rows_real * N_KV, HEAD_DIM), F32), sds((m, KV_W), BF16), sds((m, IQ_W), BF16),
            sds((nseq, rows_real, IDX_DIM), F32), sds((m, LANES), BF16), sds((m, LANES), BF16),
            sds((m, LANES), F32),
        ],
        compiler_params=pltpu.CompilerParams(
            dimension_semantics=("arbitrary",), vmem_limit_bytes=VMEM_LIMIT),
        name="attn_project",
    )(h, g, w_t, qg, kg, ikg, rq, ri)


def _attn_prompt_kernel(shift_ref, q_ref, iq_ref, iw_ref, k_ref, v_ref, ika_ref, ikb_ref,
                        wout_ref, h_ref, out_ref, keys_sc, iqt_sc, jt_sc, o_sc, m_sc, acc_sc, *,
                        kc, topk, online):
    i = pl.program_id(1)
    q0 = i * QB
    tp = k_ref.shape[0]
    nch = (q0 + QB + kc - 1) // kc
    nlt = kc // LANES
    qcol = q0 + lax.broadcasted_iota(I32, (LANES, QB), 1)
    krow = lax.broadcasted_iota(I32, (LANES, QB), 0)
    for p in range(IDX_HEADS // 2):
        sl = slice(p * LANES, (p + 1) * LANES)
        iqt_sc[:, sl] = iq_ref[:, sl].astype(F32).T.astype(BF16)
    iwt = iw_ref[...].T
    if keys_sc.shape[0] > tp:
        keys_sc[tp:, :] = jnp.full((keys_sc.shape[0] - tp, QB), NEG, F32)

    def chunk_start(c):
        return pl.multiple_of(jnp.minimum(c * kc, tp - kc), LANES)

    def score_chunk(c, carry):
        k0 = chunk_start(c)
        ka = ika_ref[pl.ds(k0, kc), :]
        kb = ikb_ref[pl.ds(k0, kc), :]
        sc = [jnp.zeros((LANES, QB), F32) for _ in range(nlt)]
        for j in range(IQ_W // MXU_COLS):
            w = iqt_sc[:, j * MXU_COLS:(j + 1) * MXU_COLS]
            sa = jnp.dot(ka, w, preferred_element_type=F32)
            sb = jnp.dot(kb, w, preferred_element_type=F32)
            for half in range(MXU_COLS // LANES):
                p = j * (MXU_COLS // LANES) + half
                cols = slice(half * LANES, (half + 1) * LANES)
                wa = iwt[2 * p:2 * p + 1, :]
                wb = iwt[2 * p + 1:2 * p + 2, :]
                for lt in range(nlt):
                    rows = slice(lt * LANES, (lt + 1) * LANES)
                    sc[lt] = (sc[lt] + wa * jnp.maximum(sa[rows, cols], 0.0)
                              + wb * jnp.maximum(sb[rows, cols], 0.0))
        for lt in range(nlt):
            s = jnp.where(k0 + lt * LANES + krow <= qcol, sc[lt] * (IDX_DIM ** -0.5), NEG)
            keys_sc[pl.ds(k0 + lt * LANES, LANES), :] = s
        return carry

    lax.fori_loop(0, nch, score_chunk, 0)

    sub8 = lax.broadcasted_iota(I32, (SUBLANES, QB), 0)

    def count_keys(pred):
        def body(c, acc):
            k0 = pl.multiple_of(c * kc, LANES)
            acc = list(acc)
            for j in range(kc // SUBLANES):
                kk = keys_sc[pl.ds(k0 + j * SUBLANES, SUBLANES), :]
                hit = pred(kk, k0 + j * SUBLANES + sub8)
                acc[j % len(acc)] = acc[j % len(acc)] + jnp.where(hit, 1.0, 0.0)
            return tuple(acc)
        acc = lax.fori_loop(0, nch, body, tuple(jnp.zeros((SUBLANES, QB), F32) for _ in range(4)))
        tot = (acc[0] + acc[1]) + (acc[2] + acc[3])
        for s in (4, 2, 1):
            tot = tot + pltpu.roll(tot, s, 0)
        return tot

    def count_ge(cand):
        return count_keys(lambda kk, kidx: kk >= cand)

    k = float(topk)
    thr8, cnt8 = _kth_largest(count_ge, (SUBLANES, QB), k, (nch * kc).astype(F32))
    tied = (cnt8 > k) & (thr8 > NEG)
    jt_sc[...] = jnp.full((SUBLANES, QB), tp, I32)

    @pl.when(jnp.max(jnp.where(tied, 1.0, 0.0)) > 0.0)
    def _():
        need = k - count_keys(lambda kk, kidx: kk > thr8)
        nbits = (keys_sc.shape[0] - 1).bit_length()

        def bit_body(bi, x):
            cand = x | jnp.left_shift(jnp.int32(1), nbits - 1 - bi)
            below = count_keys(lambda kk, kidx: (kk == thr8) & (kidx < cand))
            return jnp.where(below < need, cand, x)

        x = lax.fori_loop(0, nbits, bit_body, jnp.zeros((SUBLANES, QB), I32))
        jt_sc[...] = jnp.where(tied, x, tp)

    thr = jnp.broadcast_to(thr8[0:1, :], (LANES, QB))
    jt = jnp.broadcast_to(jt_sc[0:1, :], (LANES, QB))

    scale = HEAD_DIM ** -0.5
    unmasked = 0.0 if online else -shift_ref[0, 0]
    acc_sc[...] = jnp.zeros(acc_sc.shape, F32)
    if online:
        m_sc[...] = jnp.full(m_sc.shape, NEG, F32)
    ones_col = jnp.where(lax.broadcasted_iota(I32, (kc, LANES), 1) == 0, 1.0, 0.0).astype(BF16)

    def att_chunk(c, carry):
        k0 = chunk_start(c)
        bias = []
        for lt in range(nlt):
            kk = keys_sc[pl.ds(k0 + lt * LANES, LANES), :]
            kidx = k0 + lt * LANES + krow
            sel = ((kk > thr) | ((kk == thr) & (kidx <= jt))) & (kidx <= qcol) & (kidx >= c * kc)
            bias.append(jnp.where(sel, unmasked, NEG).T)
        bias = jnp.concatenate(bias, axis=1)
        bias = jnp.concatenate([bias] * GROUP, axis=0)
        for n in range(N_KV):
            qs = jnp.concatenate(
                [q_ref[:, (n * GROUP + g) * LANES:(n * GROUP + g + 1) * LANES] for g in range(GROUP)], axis=0)
            kt = k_ref[pl.ds(k0, kc), n * LANES:(n + 1) * LANES]
            vt = jnp.concatenate([v_ref[pl.ds(k0, kc), n * LANES:(n + 1) * LANES], ones_col], axis=1)
            s = lax.dot_general(qs, kt, _NT, preferred_element_type=F32) * scale + bias
            if online:
                m_old = m_sc[n]
                m_new = jnp.maximum(m_old, jnp.max(s, axis=1, keepdims=True))
                p = jnp.exp(s - m_new).astype(BF16)
                acc_sc[n] = jnp.exp(m_old - m_new) * acc_sc[n] + jnp.dot(p, vt, preferred_element_type=F32)
                m_sc[n] = m_new
            else:
                acc_sc[n] += jnp.dot(jnp.exp(s).astype(BF16), vt, preferred_element_type=F32)
        return carry

    lax.fori_loop(0, nch, att_chunk, 0)
    for n in range(N_KV):
        a = acc_sc[n]
        o = a[:, :LANES] * (1.0 / a[:, LANES:LANES + 1])
        for g in range(GROUP):
            hh = n * GROUP + g
            o_sc[:, hh * LANES:(hh + 1) * LANES] = o[g * QB:(g + 1) * QB].astype(BF16)

    out_ref[...] = h_ref[...] + jnp.dot(o_sc[...], wout_ref[...], preferred_element_type=F32)


MAX_LOGIT_BOUND = 40.0


def _attn_prompt(h, q, iq, iw, kb, vb, ika, ikb, wout, layer, logit_bound, *, batch, tp, topk):
    args = (logit_bound.reshape(1, 1), q, iq, iw, kb, vb, ika, ikb, wout, h)
    kc = 3 * LANES
    assert kc >= topk and (tp - kc) % LANES == 0
    kw = dict(layer=layer, batch=batch, tp=tp, kc=kc, topk=topk)
    return lax.cond(logit_bound <= MAX_LOGIT_BOUND,
                    lambda: _attn_prompt_call(*args, online=False, **kw),
                    lambda: _attn_prompt_call(*args, online=True, **kw))


def _attn_prompt_call(shift, q, iq, iw, kb, vb, ika, ikb, wout, h, *, layer, batch, tp, kc, topk, online):
    nqb = tp // QB
    qrow = lambda b, i: (b * nqb + i, 0)
    seq = lambda b, i: (b, 0)
    return pl.pallas_call(
        functools.partial(_attn_prompt_kernel, kc=kc, topk=topk, online=online),
        grid=(batch, nqb),
        in_specs=[
            pl.BlockSpec(memory_space=pltpu.SMEM),
            pl.BlockSpec((QB, Q_W), qrow),
            pl.BlockSpec((QB, IQ_W), qrow),
            pl.BlockSpec((QB, LANES), qrow),
            pl.BlockSpec((tp, KV_W), seq),
            pl.BlockSpec((tp, KV_W), seq),
            pl.BlockSpec((tp, LANES), seq),
            pl.BlockSpec((tp, LANES), seq),
            pl.BlockSpec((None, Q_W, D_MODEL), lambda b, i: (layer, 0, 0)),
            pl.BlockSpec((QB, D_MODEL), qrow),
        ],
        out_specs=pl.BlockSpec((QB, D_MODEL), qrow),
        out_shape=jax.ShapeDtypeStruct(h.shape, F32),
        scratch_shapes=[
            pltpu.VMEM((_round_up(tp, kc), QB), F32),
            pltpu.VMEM((LANES, IQ_W), BF16),
            pltpu.VMEM((SUBLANES, QB), I32),
            pltpu.VMEM((QB, Q_W), BF16),
            pltpu.VMEM((N_KV, GROUP * QB, 1), F32),
            pltpu.VMEM((N_KV, GROUP * QB, 2 * HEAD_DIM), F32),
        ],
        compiler_params=pltpu.CompilerParams(
            dimension_semantics=("arbitrary", "arbitrary"), vmem_limit_bytes=VMEM_LIMIT),
        name="attn_prompt_online" if online else "attn_prompt_bounded",
    )(shift, q, iq, iw, kb, vb, ika, ikb, wout, h)


S1_PAGES = 64
S2_PAGES = 32


def _idx_page_scores(iq, iww, page_t):
    s = jnp.dot(iq, page_t, preferred_element_type=F32)
    sc = jnp.zeros((SUBLANES, PAGE), F32)
    for hh in range(IDX_HEADS):
        sl = slice(hh * SUBLANES, (hh + 1) * SUBLANES)
        sc = sc + iww[sl, :] * jnp.maximum(s[sl, :], 0.0)
    return sc * (IDX_DIM ** -0.5)


def _sample_index_kernel(pt_ref, *refs, past, n_new, topk):
    pages = refs[:S1_PAGES]
    iq_ref, iww_ref, iknew_ref, keys_ref, thr_ref = refs[S1_PAGES:]
    s = pl.program_id(1)
    iq = iq_ref[0]
    iww = iww_ref[0]
    for r in range(S1_PAGES):
        sc = _idx_page_scores(iq, iww, pages[r][...].astype(BF16))
        off = pl.multiple_of((s * S1_PAGES + r) * PAGE, PAGE)
        keys_ref[0, :, pl.ds(off, PAGE)] = sc

    @pl.when(s == pl.num_programs(1) - 1)
    def _():
        sc = _idx_page_scores(iq, iww, iknew_ref[0])
        t = lax.broadcasted_iota(I32, sc.shape, 0)
        jn = lax.broadcasted_iota(I32, sc.shape, 1)
        sc = jnp.where((jn <= t) & (jn < n_new), sc, NEG)
        keys_ref[0, :, pl.ds(past, PAGE)] = sc
        nk = keys_ref.shape[2]
        lane = lax.broadcasted_iota(I32, (SUBLANES, LANES), 1)

        def count_keys(pred):
            acc = [jnp.zeros((SUBLANES, LANES), F32) for _ in range(4)]
            for j in range(nk // LANES):
                kk = keys_ref[0, :, j * LANES:(j + 1) * LANES]
                acc[j % 4] = acc[j % 4] + jnp.where(pred(kk, j * LANES + lane), 1.0, 0.0)
            tot = (acc[0] + acc[1]) + (acc[2] + acc[3])
            return jnp.broadcast_to(jnp.sum(tot, axis=1, keepdims=True), (SUBLANES, LANES))

        def count_ge(cand):
            return count_keys(lambda kk, kidx: kk >= cand)

        k = float(topk)
        thr, cnt = _kth_largest(count_ge, (SUBLANES, LANES), k, float(nk))
        thr_ref[0, :, :LANES] = thr
        tied = (cnt > k) & (thr > NEG)
        thr_ref[0, :, LANES:] = jnp.full((SUBLANES, LANES), float(nk), F32)

        @pl.when(jnp.max(jnp.where(tied, 1.0, 0.0)) > 0.0)
        def _():
            need = k - count_keys(lambda kk, kidx: kk > thr)
            nbits = (nk - 1).bit_length()

            def bit_body(bi, x):
                cand = x | jnp.left_shift(jnp.int32(1), nbits - 1 - bi)
                below = count_keys(lambda kk, kidx: (kk == thr) & (kidx < cand))
                return jnp.where(below < need, cand, x)

            x = lax.fori_loop(0, nbits, bit_body, jnp.zeros((SUBLANES, LANES), I32))
            thr_ref[0, :, LANES:] = jnp.where(tied, x, nk).astype(F32)


def _sample_index(page_table, cache_ik, layer, iq, iww, iknew, *, n_new, topk):
    nb, n_pages = page_table.shape
    steps = n_pages // S1_PAGES
    nk = (n_pages + 1) * PAGE
    page_spec = lambda r: pl.BlockSpec(
        (None, None, IDX_DIM, PAGE), lambda b, s, pt: (layer, pt[b, s * S1_PAGES + r], 0, 0))
    per_b = lambda b, s, pt: (b, 0, 0)
    return pl.pallas_call(
        functools.partial(_sample_index_kernel, past=n_pages * PAGE, n_new=n_new, topk=topk),
        grid_spec=pltpu.PrefetchScalarGridSpec(
            num_scalar_prefetch=1,
            grid=(nb, steps),
            in_specs=[page_spec(r) for r in range(S1_PAGES)] + [
                pl.BlockSpec((1, IDX_HEADS * SUBLANES, IDX_DIM), per_b),
                pl.BlockSpec((1, IDX_HEADS * SUBLANES, LANES), per_b),
                pl.BlockSpec((1, IDX_DIM, PAGE), per_b),
            ],
            out_specs=[
                pl.BlockSpec((1, SUBLANES, nk), per_b),
                pl.BlockSpec((1, SUBLANES, 2 * LANES), per_b),
            ],
        ),
        out_shape=[jax.ShapeDtypeStruct((nb, SUBLANES, nk), F32),
                   jax.ShapeDtypeStruct((nb, SUBLANES, 2 * LANES), F32)],
        compiler_params=pltpu.CompilerParams(
            dimension_semantics=("arbitrary", "arbitrary"), vmem_limit_bytes=VMEM_LIMIT),
        name="sample_index",
    )(page_table, *([cache_ik] * S1_PAGES), iq, iww, iknew)


def _sample_attend_kernel(pt_ref, *refs, n_new):
    kpages = refs[:S2_PAGES]
    vpages = refs[S2_PAGES:2 * S2_PAGES]
    (qbd_ref, keys_ref, keysnew_ref, thr_ref, knew_ref, vnew_ref, o_ref,
     m_sc, l_sc, acc_sc) = refs[2 * S2_PAGES:]
    s = pl.program_id(1)
    scale = HEAD_DIM ** -0.5
    qbd = qbd_ref[0]
    thr = thr_ref[0, :, :LANES]
    jt = thr_ref[0, :, LANES:].astype(I32)
    lane = lax.broadcasted_iota(I32, (SUBLANES, LANES), 1)

    def selected(kk, kidx):
        return (kk > thr) | ((kk == thr) & (kidx <= jt))

    @pl.when(s == 0)
    def _():
        m_sc[...] = jnp.full(m_sc.shape, NEG, F32)
        l_sc[...] = jnp.zeros(l_sc.shape, F32)
        acc_sc[...] = jnp.zeros(acc_sc.shape, F32)

    def page_update(kp, vp, bias):
        bias = jnp.concatenate([bias] * (N_HEADS), axis=0)
        sc = lax.dot_general(qbd, kp, _NT, preferred_element_type=F32) * scale + bias
        m_old = m_sc[...]
        m_new = jnp.maximum(m_old, jnp.max(sc, axis=1, keepdims=True))
        alpha = jnp.exp(m_old - m_new)
        p = jnp.exp(sc - m_new)
        l_sc[...] = alpha * l_sc[...] + jnp.sum(p, axis=1, keepdims=True)
        acc_sc[...] = alpha * acc_sc[...] + jnp.dot(p.astype(BF16), vp, preferred_element_type=F32)
        m_sc[...] = m_new

    def page_rows(ref):
        return jnp.concatenate(
            [ref[pl.ds(n, PAGE, stride=N_KV), :] for n in range(N_KV)], axis=1).astype(BF16)

    bias = jnp.concatenate(
        [jnp.where(selected(keys_ref[0, :, r * PAGE:(r + 1) * PAGE], (s * S2_PAGES + r) * PAGE + lane), 0.0, NEG)
         for r in range(S2_PAGES)], axis=1)
    page_update(jnp.concatenate([page_rows(kpages[r]) for r in range(S2_PAGES)], axis=0),
                jnp.concatenate([page_rows(vpages[r]) for r in range(S2_PAGES)], axis=0), bias)

    @pl.when(s == pl.num_programs(1) - 1)
    def _():
        kk = keysnew_ref[0]
        t = lax.broadcasted_iota(I32, kk.shape, 0)
        jn = lax.broadcasted_iota(I32, kk.shape, 1)
        past = pl.num_programs(1) * S2_PAGES * PAGE
        page_update(knew_ref[0], vnew_ref[0],
                    jnp.where(selected(kk, past + jn) & (jn <= t) & (jn < n_new), 0.0, NEG))
        o_ref[0] = acc_sc[...] / l_sc[...]


def _sample_attend(page_table, cache_k, cache_v, layer, qbd, keys, thr, knew, vnew, *, n_new):
    nb, n_pages = page_table.shape
    steps = n_pages // S2_PAGES
    page_spec = lambda r: pl.BlockSpec(
        (None, None, PAGE * N_KV, HEAD_DIM), lambda b, s, pt: (layer, pt[b, s * S2_PAGES + r], 0, 0))
    per_b = lambda b, s, pt: (b, 0, 0)
    rows = N_HEADS * SUBLANES
    return pl.pallas_call(
        functools.partial(_sample_attend_kernel, n_new=n_new),
        grid_spec=pltpu.PrefetchScalarGridSpec(
            num_scalar_prefetch=1,
            grid=(nb, steps),
            in_specs=[page_spec(r) for r in range(S2_PAGES)] * 2 + [
                pl.BlockSpec((1, rows, KV_W), per_b),
                pl.BlockSpec((1, SUBLANES, S2_PAGES * PAGE), lambda b, s, pt: (b, 0, s)),
                pl.BlockSpec((1, SUBLANES, PAGE), lambda b, s, pt: (b, 0, n_pages)),
                pl.BlockSpec((1, SUBLANES, 2 * LANES), per_b),
                pl.BlockSpec((1, PAGE, KV_W), per_b),
                pl.BlockSpec((1, PAGE, KV_W), per_b),
            ],
            out_specs=pl.BlockSpec((1, rows, KV_W), per_b),
            scratch_shapes=[
                pltpu.VMEM((rows, 1), F32),
                pltpu.VMEM((rows, 1), F32),
                pltpu.VMEM((rows, KV_W), F32),
            ],
        ),
        out_shape=jax.ShapeDtypeStruct((nb, rows, KV_W), F32),
        compiler_params=pltpu.CompilerParams(
            dimension_semantics=("arbitrary", "arbitrary"), vmem_limit_bytes=VMEM_LIMIT),
        name="sample_attend",
    )(page_table, *([cache_k] * S2_PAGES), *([cache_v] * S2_PAGES), qbd, keys, keys, thr, knew, vnew)


def _matmul_res_kernel(h_ref, x_ref, w_ref, o_ref):
    o_ref[...] = h_ref[...] + jnp.dot(x_ref[...], w_ref[...], preferred_element_type=F32)


def _matmul_res(h, x, w, layer, *, tn):
    m, n = h.shape
    kdim = x.shape[1]
    return pl.pallas_call(
        _matmul_res_kernel,
        grid=(n // tn,),
        in_specs=[pl.BlockSpec((m, tn), lambda j: (0, j)),
                  pl.BlockSpec((m, kdim), lambda j: (0, 0)),
                  pl.BlockSpec((None, kdim, tn), lambda j: (layer, 0, j))],
        out_specs=pl.BlockSpec((m, tn), lambda j: (0, j)),
        out_shape=jax.ShapeDtypeStruct((m, n), F32),
        compiler_params=pltpu.CompilerParams(dimension_semantics=("arbitrary",)),
        name="matmul_residual",
    )(h, x, w)


def _conv_rows(ext_sc, u, init_ref, carry_sc, c, first, cw_ref, cb_ref, tail_ref, *,
               width, shift, d0, tail_end):
    tm = u.shape[0]
    ext_sc[d0:d0 + tm, :] = u
    ext_sc[0:d0, :] = jnp.where(first, init_ref[0], carry_sc[c])

    y = cb_ref[...] + cw_ref[width - 1:width, :] * u
    for j in range(1, width):
        y = y + cw_ref[width - 1 - j:width - j, :] * ext_sc[d0 - j * shift:d0 - j * shift + tm, :]
    carry_sc[c] = ext_sc[tm:tm + d0, :]
    tail_ref[0] = ext_sc[tail_end:tail_end + d0, :]
    return y


def _ffn_kernel(h_ref, g_ref, wu_ref, wg_ref, cw_ref, cb_ref, wd_ref, st_ref, out_ref, tail_ref,
                xn_sc, acc_sc, ext_sc, carry_sc, *, shift, d0, tiles_per_seq, tail_end):
    i = pl.program_id(0)
    c = pl.program_id(1)

    @pl.when(c == 0)
    def _():
        xn_sc[...] = _rms_rows(h_ref[...], g_ref[...]).astype(BF16)
        acc_sc[...] = jnp.zeros(acc_sc.shape, F32)

    xn = xn_sc[...]
    u = jnp.dot(xn, wu_ref[...], preferred_element_type=F32)
    conv = _conv_rows(ext_sc, u, st_ref, carry_sc, c, (i % tiles_per_seq) == 0, cw_ref, cb_ref,
                      tail_ref, width=FFN_CONV, shift=shift, d0=d0, tail_end=tail_end)
    mid = _gelu(conv) * jnp.dot(xn, wg_ref[...], preferred_element_type=F32)
    acc_sc[...] += jnp.dot(mid.astype(BF16), wd_ref[...], preferred_element_type=F32)

    @pl.when(c == pl.num_programs(1) - 1)
    def _():
        out_ref[...] = h_ref[...] + acc_sc[...]


def _conv_ffn(h, g, wu, wg, cw, cb, wd, state, *, layer, tm, tf, shift, tiles_per_seq, tail_end):
    m = h.shape[0]
    d0 = state.shape[1]
    n_tiles, n_chunks = m // tm, D_FF // tf
    row = lambda i, c: (i, 0)
    return pl.pallas_call(
        functools.partial(_ffn_kernel, shift=shift, d0=d0, tiles_per_seq=tiles_per_seq, tail_end=tail_end),
        grid=(n_tiles, n_chunks),
        in_specs=[
            pl.BlockSpec((tm, D_MODEL), row),
            pl.BlockSpec((1, D_MODEL), lambda i, c: (0, 0)),
            pl.BlockSpec((None, D_MODEL, tf), lambda i, c: (layer, 0, c)),
            pl.BlockSpec((None, D_MODEL, tf), lambda i, c: (layer, 0, c)),
            pl.BlockSpec((FFN_CONV, tf), lambda i, c: (0, c)),
            pl.BlockSpec((1, tf), lambda i, c: (0, c)),
            pl.BlockSpec((None, tf, D_MODEL), lambda i, c: (layer, c, 0)),
            pl.BlockSpec((1, d0, tf), lambda i, c: (i // tiles_per_seq, 0, c)),
        ],
        out_specs=[
            pl.BlockSpec((tm, D_MODEL), row),
            pl.BlockSpec((1, d0, tf), lambda i, c: (i, 0, c)),
        ],
        out_shape=[jax.ShapeDtypeStruct((m, D_MODEL), F32),
                   jax.ShapeDtypeStruct((n_tiles, d0, D_FF), F32)],
        scratch_shapes=[
            pltpu.VMEM((tm, D_MODEL), BF16),
            pltpu.VMEM((tm, D_MODEL), F32),
            pltpu.VMEM((d0 + tm, tf), F32),
            pltpu.VMEM((n_chunks, d0, tf), F32),
        ],
        compiler_params=pltpu.CompilerParams(
            dimension_semantics=("arbitrary", "arbitrary"), vmem_limit_bytes=VMEM_LIMIT),
        name="conv_ffn",
    )(h, g, wu, wg, cw, cb, wd, state)


LRU_BLOCKS_PER_STEP = 2

def _scan_rows(a_sc, b_sc, hc, *, shift):
    tm = a_sc.shape[0]
    ngroups = tm // SUBLANES
    row = lax.broadcasted_iota(I32, (SUBLANES, a_sc.shape[1]), 0)

    def body(gi, hc):
        r0 = gi * SUBLANES
        a = a_sc[pl.ds(r0, SUBLANES), :]
        b = b_sc[pl.ds(r0, SUBLANES), :]
        if shift == 1:
            for s in (1, 2, 4):
                keep = row >= s
                a_sh = jnp.where(keep, pltpu.roll(a, s, 0), 1.0)
                b_sh = jnp.where(keep, pltpu.roll(b, s, 0), 0.0)
                b = a * b_sh + b
                a = a * a_sh
        hrows = a * hc + b
        b_sc[pl.ds(r0, SUBLANES), :] = hrows
        if shift == 1:
            return jnp.broadcast_to(hrows[SUBLANES - 1:SUBLANES, :], hrows.shape)
        return hrows

    for gi in range(ngroups):
        hc = body(gi, hc)
    return hc


def _lru_kernel(h_ref, g_ref, wx_ref, wy_ref, cw_ref, cb_ref, wa_ref, ba_ref, wi_ref, bi_ref,
                lam_ref, wo_ref, h0_ref, st_ref, out_ref, htail_ref, utail_ref,
                xn_sc, acc_sc, ext_sc, a_sc, b_sc, gate_sc, carry_u, carry_h, *,
                shift, d0, tiles_per_seq, tail_end):
    i = pl.program_id(0)
    c = pl.program_id(1)
    first = (i % tiles_per_seq) == 0

    @pl.when(c == 0)
    def _():
        xn_sc[...] = _rms_rows(h_ref[...], g_ref[...]).astype(BF16)
        acc_sc[...] = jnp.zeros(acc_sc.shape, F32)

    xn = xn_sc[...]
    u = jnp.dot(xn, wx_ref[...], preferred_element_type=F32)
    gate_sc[...] = _gelu(jnp.dot(xn, wy_ref[...], preferred_element_type=F32))
    conv = _conv_rows(ext_sc, u, st_ref, carry_u, c, first, cw_ref, cb_ref, utail_ref,
                      width=LRU_CONV, shift=shift, d0=d0, tail_end=tail_end)
    lam = lam_ref[...]
    log_sig = jnp.minimum(lam, 0.0) - jnp.log(1.0 + jnp.exp(-jnp.abs(lam)))
    for s in range(wa_ref.shape[0]):
        sl = slice(s * RNN_BD, (s + 1) * RNN_BD)
        cs = conv[:, sl]
        cbf = cs.astype(BF16)
        r = _sigmoid(jnp.dot(cbf, wa_ref[s], preferred_element_type=F32) + ba_ref[:, sl])
        ig = _sigmoid(jnp.dot(cbf, wi_ref[s], preferred_element_type=F32) + bi_ref[:, sl])
        log_a = LRU_C * r * log_sig[:, sl]
        a = jnp.exp(log_a)
        a_sc[:, sl] = a
        b_sc[:, sl] = jnp.sqrt(-jnp.tanh(log_a) * (a * a + 1.0)) * (ig * cs)

    carry_h[c] = _scan_rows(a_sc, b_sc, jnp.where(first, h0_ref[0], carry_h[c]), shift=shift)
    htail_ref[0] = b_sc[tail_end - SUBLANES:tail_end, :]
    tm = a_sc.shape[0]
    nsplit = 4 if tm % (4 * 2 * SUBLANES) == 0 else 1
    for rc in range(nsplit):
        rows = slice(rc * (tm // nsplit), (rc + 1) * (tm // nsplit))
        acc_sc[rows, :] += jnp.dot((b_sc[rows, :] * gate_sc[rows, :]).astype(BF16), wo_ref[...],
                                   preferred_element_type=F32)

    @pl.when(c == pl.num_programs(1) - 1)
    def _():
        out_ref[...] = h_ref[...] + acc_sc[...]


def _lru_block(h, g, wx, wy, cw, cb, wa, ba, wi, bi, lam, wo, h0, state, *,
               layer, tm, shift, tiles_per_seq, tail_end):
    m = h.shape[0]
    d0 = state.shape[1]
    n_tiles = m // tm
    nsub = LRU_BLOCKS_PER_STEP
    bd = nsub * RNN_BD
    nsteps = RNN_BLOCKS // nsub
    row = lambda i, c: (i, 0)
    col = lambda i, c: (0, c)
    return pl.pallas_call(
        functools.partial(_lru_kernel, shift=shift, d0=d0, tiles_per_seq=tiles_per_seq, tail_end=tail_end),
        grid=(n_tiles, nsteps),
        in_specs=[
            pl.BlockSpec((tm, D_MODEL), row),
            pl.BlockSpec((1, D_MODEL), lambda i, c: (0, 0)),
            pl.BlockSpec((None, D_MODEL, bd), lambda i, c: (layer, 0, c)),
            pl.BlockSpec((None, D_MODEL, bd), lambda i, c: (layer, 0, c)),
            pl.BlockSpec((LRU_CONV, bd), col),
            pl.BlockSpec((1, bd), col),
            pl.BlockSpec((None, nsub, RNN_BD, RNN_BD), lambda i, c: (layer, c, 0, 0)),
            pl.BlockSpec((1, bd), col),
            pl.BlockSpec((None, nsub, RNN_BD, RNN_BD), lambda i, c: (layer, c, 0, 0)),
            pl.BlockSpec((1, bd), col),
            pl.BlockSpec((1, bd), col),
            pl.BlockSpec((None, bd, D_MODEL), lambda i, c: (layer, c, 0)),
            pl.BlockSpec((1, SUBLANES, bd), lambda i, c: (i // tiles_per_seq, 0, c)),
            pl.BlockSpec((1, d0, bd), lambda i, c: (i // tiles_per_seq, 0, c)),
        ],
        out_specs=[
            pl.BlockSpec((tm, D_MODEL), row),
            pl.BlockSpec((1, SUBLANES, bd), lambda i, c: (i, 0, c)),
            pl.BlockSpec((1, d0, bd), lambda i, c: (i, 0, c)),
        ],
        out_shape=[jax.ShapeDtypeStruct((m, D_MODEL), F32),
                   jax.ShapeDtypeStruct((n_tiles, SUBLANES, D_MODEL), F32),
                   jax.ShapeDtypeStruct((n_tiles, d0, D_MODEL), F32)],
        scratch_shapes=[
            pltpu.VMEM((tm, D_MODEL), BF16),
            pltpu.VMEM((tm, D_MODEL), F32),
            pltpu.VMEM((d0 + tm, bd), F32),
            pltpu.VMEM((tm, bd), F32),
            pltpu.VMEM((tm, bd), F32),
            pltpu.VMEM((tm, bd), F32),
            pltpu.VMEM((nsteps, d0, bd), F32),
            pltpu.VMEM((nsteps, SUBLANES, bd), F32),
        ],
        compiler_params=pltpu.CompilerParams(
            dimension_semantics=("arbitrary", "arbitrary"), vmem_limit_bytes=VMEM_LIMIT),
        name="rg_lru",
    )(h, g, wx, wy, cw, cb, wa, ba, wi, bi, lam, wo, h0, state)


def _rope_tables(pos, rot, period):
    half = rot // 2
    inv = 1.0 / (ROPE_THETA ** (jnp.arange(half, dtype=F32) * (2.0 / rot)))
    ang = pos.astype(F32)[:, None] * inv[None, :]
    lane = jnp.arange(LANES) % period
    cos = jnp.cos(ang)[:, lane % half]
    sin = jnp.sin(ang)[:, lane % half]
    return jnp.stack([
        jnp.where(lane < rot, cos, 1.0),
        jnp.where(lane < half, -sin, 0.0),
        jnp.where((lane >= half) & (lane < rot), sin, 0.0),
    ])


def _time_major(x):
    return jnp.swapaxes(x, 0, 1)


def kernel(x_prompt, x_sample, cache_k, cache_v, cache_idx_k, state_lru_h, state_lru_conv, state_ffn_conv, page_table, meta_tokens, norm1_g, norm2_g, attn_w_in, attn_q_norm_g, attn_k_norm_g, attn_idx_k_norm_g, attn_w_out, lru_w_x, lru_w_y, lru_conv_w, lru_conv_b, lru_wa, lru_ba, lru_wi, lru_bi, lru_lambda, lru_w_out, ffn_w_up, ffn_w_gate, ffn_dw_w, ffn_dw_b, ffn_w_down):
    batch, seq, _ = x_prompt.shape
    nb, ns, _ = x_sample.shape
    depth = norm1_g.shape[0]
    t_real = seq + N_META
    tp = _round_up(t_real, QB)
    n_pages = page_table.shape[1]
    past = n_pages * PAGE
    topk_p = min(TOPK_MAX, t_real // 4)
    topk_s = min(TOPK_MAX, (past + ns) // 4)
    assert nb == SUBLANES and ns <= SUBLANES and n_pages % S1_PAGES == 0 and n_pages % S2_PAGES == 0

    tiles_per_seq = 6
    tm = tp // tiles_per_seq
    assert tm * tiles_per_seq == tp and tm % (2 * SUBLANES) == 0
    tail_end = t_real - (tiles_per_seq - 1) * tm
    assert tail_end % SUBLANES == 0 and tail_end >= SUBLANES
    tm_proj = 3 * LANES
    assert tp % tm_proj == 0
    ms = nb * ns

    hp = jnp.concatenate([
        jnp.broadcast_to(meta_tokens[None], (batch, N_META, D_MODEL)), x_prompt,
        jnp.zeros((batch, tp - t_real, D_MODEL), F32)], axis=1).reshape(batch * tp, D_MODEL)
    hs = _time_major(x_sample).reshape(ms, D_MODEL)

    rq_p = _rope_tables(jnp.arange(tp), ROT, HEAD_DIM)
    ri_p = _rope_tables(jnp.arange(tp), IDX_ROT, IDX_DIM)
    pos_s = jnp.repeat(past + jnp.arange(ns), nb)
    rq_s = _rope_tables(pos_s, ROT, HEAD_DIM)
    ri_s = _rope_tables(pos_s, IDX_ROT, IDX_DIM)

    cache_k4 = cache_k.reshape(cache_k.shape[:2] + (PAGE * N_KV, HEAD_DIM))
    cache_v4 = cache_v.reshape(cache_v.shape[:2] + (PAGE * N_KV, HEAD_DIM))
    cache_ik_t = jnp.swapaxes(cache_idx_k, 2, 3)

    w_in_t = jnp.pad(jnp.swapaxes(attn_w_in, 1, 2).astype(BF16), ((0, 0), (0, PROJ_WP - PROJ_W), (0, 0)))
    w_out_all = attn_w_out.astype(BF16)
    lru_mats = dict(wx=lru_w_x.astype(BF16), wy=lru_w_y.astype(BF16), wa=lru_wa.astype(BF16),
                    wi=lru_wi.astype(BF16), wo=lru_w_out.astype(BF16))
    ffn_mats = dict(wu=ffn_w_up.astype(BF16), wg=ffn_w_gate.astype(BF16), wd=ffn_w_down.astype(BF16))

    kp, vp, ikp, ksl, vsl, iksl = [], [], [], [], [], []
    hpl, cpl, hsl, csl, fpl, fsl = [], [], [], [], [], []
    last_tiles = jnp.arange(batch) * tiles_per_seq + tiles_per_seq - 1

    for layer in range(depth):
        mi = layer // 2
        g1 = norm1_g[layer][None]
        if layer % 2 == 0:
            qg = attn_q_norm_g[mi][None]
            kg = attn_k_norm_g[mi][None]
            ikg = jnp.pad(attn_idx_k_norm_g[mi], (0, LANES - IDX_DIM))[None]

            q, kf, kb, vf, vb, iq, ikf, ika, ikb, iw = _attn_project(
                hp, g1, w_in_t, mi, qg, kg, ikg, rq_p, ri_p, tm=tm_proj, nseq=batch, rows_real=t_real)
            logit_bound = (1.02 * HEAD_DIM ** 0.5) * jnp.max(jnp.abs(qg)) * jnp.max(jnp.abs(kg))
            hp = _attn_prompt(hp, q, iq, iw, kb, vb, ika, ikb, w_out_all, mi, logit_bound,
                              batch=batch, tp=tp, topk=topk_p)
            kp.append(kf.reshape(batch, t_real, N_KV, HEAD_DIM))
            vp.append(vf.reshape(batch, t_real, N_KV, HEAD_DIM))
            ikp.append(ikf)

            q, kf, kb, vf, vb, iq, ikf, ika, ikb, iw = _attn_project(
                hs, g1, w_in_t, mi, qg, kg, ikg, rq_s, ri_s, tm=ms, nseq=1, rows_real=ms)
            ksl.append(_time_major(kf.reshape(ns, nb, N_KV, HEAD_DIM)))
            vsl.append(_time_major(vf.reshape(ns, nb, N_KV, HEAD_DIM)))
            iksl.append(_time_major(ikf.reshape(ns, nb, IDX_DIM)))

            pad_q = lambda x: jnp.pad(x, [(0, 0)] * (x.ndim - 2) + [(0, SUBLANES - ns), (0, 0)])
            iq_b = pad_q(iq.reshape(ns, nb, IDX_HEADS, IDX_DIM).transpose(1, 2, 0, 3))
            iq_b = iq_b.reshape(nb, IDX_HEADS * SUBLANES, IDX_DIM)
            iw_b = jnp.pad(iw[:, :IDX_HEADS].reshape(ns, nb, IDX_HEADS).transpose(1, 2, 0), ((0, 0), (0, 0), (0, SUBLANES - ns)))
            iww = jnp.broadcast_to(iw_b.reshape(nb, IDX_HEADS * SUBLANES, 1), (nb, IDX_HEADS * SUBLANES, LANES))
            new_page = lambda x: jnp.pad(_time_major(x.reshape(ns, nb, -1)), ((0, 0), (0, PAGE - ns), (0, 0)))
            iknew = jnp.swapaxes(new_page(ika[:, :IDX_DIM]), 1, 2)
            keys, thr = _sample_index(page_table, cache_ik_t, mi, iq_b, iww, iknew, n_new=ns, topk=topk_s)

            q_b = pad_q(q.reshape(ns, nb, N_KV, GROUP, HEAD_DIM).transpose(1, 2, 3, 0, 4))
            qbd = q_b[:, :, :, :, None, :] * jnp.eye(N_KV, dtype=BF16)[None, :, None, None, :, None]
            qbd = qbd.reshape(nb, N_HEADS * SUBLANES, KV_W)
            o = _sample_attend(page_table, cache_k4, cache_v4, mi, qbd, keys, thr,
                               new_page(kb), new_page(vb), n_new=ns)
            o = o.reshape(nb, N_KV, GROUP, SUBLANES, N_KV, HEAD_DIM)
            o = jnp.stack([o[:, n, :, :ns, n, :] for n in range(N_KV)], axis=1)
            o = o.transpose(3, 0, 1, 2, 4).reshape(ms, Q_W).astype(BF16)
            hs = _matmul_res(hs, o, w_out_all, mi, tn=512)
        else:
            lw = dict(
                g=g1, cw=lru_conv_w[mi], cb=lru_conv_b[mi][None], ba=lru_ba[mi][None], bi=lru_bi[mi][None],
                lam=lru_lambda[mi][None], layer=mi, **lru_mats)
            hp, htail, utail = _lru_block(
                hp, h0=jnp.zeros((batch, SUBLANES, D_MODEL), F32),
                state=jnp.zeros((batch, SUBLANES, D_MODEL), F32),
                tm=tm, shift=1, tiles_per_seq=tiles_per_seq, tail_end=tail_end, **lw)
            hpl.append(htail[last_tiles, SUBLANES - 1])
            cpl.append(utail[last_tiles, SUBLANES - (LRU_CONV - 1):])
            d0s = (LRU_CONV - 1) * nb
            hs, htail, utail = _lru_block(
                hs, h0=state_lru_h[mi][None],
                state=_time_major(state_lru_conv[mi]).reshape(1, d0s, D_MODEL),
                tm=ms, shift=nb, tiles_per_seq=1, tail_end=ms, **lw)
            hsl.append(htail[0])
            csl.append(_time_major(utail.reshape(LRU_CONV - 1, nb, D_MODEL)))

        fw = dict(g=norm2_g[layer][None], cw=ffn_dw_w[layer], cb=ffn_dw_b[layer][None], layer=layer, **ffn_mats)
        hp, tail = _conv_ffn(hp, state=jnp.zeros((batch, SUBLANES, D_FF), F32),
                             tm=tm, tf=768, shift=1, tiles_per_seq=tiles_per_seq, tail_end=tail_end, **fw)
        fpl.append(tail[last_tiles, SUBLANES - (FFN_CONV - 1):])
        d0s = (FFN_CONV - 1) * nb
        hs, tail = _conv_ffn(hs, state=_time_major(state_ffn_conv[layer]).reshape(1, d0s, D_FF),
                             tm=ms, tf=512, shift=nb, tiles_per_seq=1, tail_end=ms, **fw)
        fsl.append(_time_major(tail.reshape(FFN_CONV - 1, nb, D_FF)))

    y_prompt = hp.reshape(batch, tp, D_MODEL)[:, N_META:t_real]
    y_sample = _time_major(hs.reshape(ns, nb, D_MODEL))
    return (y_prompt, y_sample, jnp.stack(kp), jnp.stack(vp), jnp.stack(ikp),
            jnp.stack(ksl), jnp.stack(vsl), jnp.stack(iksl),
            jnp.stack(hpl), jnp.stack(cpl), jnp.stack(hsl), jnp.stack(csl),
            jnp.stack(fpl), jnp.stack(fsl))
```
